```python
import math
import jax, jax.numpy as jnp
from jax import lax
import numpy as np

D_MODEL = 1024
BATCH = 8
SEQ = 2048
DEPTH = 1

CHUNK = 64
D_MIX = 2 * D_MODEL
D_MLSTM = D_MIX // 2
D_CONV = D_MIX - D_MLSTM
N_HEADS = 4
HEAD_DIM = D_MLSTM // N_HEADS
CONV_WIDTH = 31
D_IN_PROJ = 5 * D_MLSTM + 2 * N_HEADS + 3 * D_CONV
EPS = 1e-6
M_INIT = -1e30

kernel_name = "hybrid_mlstm_conformer_conv_block"


def rmsnorm(x, g):
    xf = x.astype(jnp.float32)
    y = xf * lax.rsqrt(jnp.mean(xf * xf, axis=-1, keepdims=True) + EPS)
    return (y * g.astype(jnp.float32)).astype(x.dtype)


def layernorm(x, g, b):
    xf = x.astype(jnp.float32)
    mu = jnp.mean(xf, axis=-1, keepdims=True)
    xc = xf - mu
    var = jnp.mean(xc * xc, axis=-1, keepdims=True)
    y = xc * lax.rsqrt(var + EPS) * g.astype(jnp.float32) + b.astype(jnp.float32)
    return y.astype(x.dtype)


def mlstm_chunkwise(q, k, v, i_pre, f_pre):
    B, S, _ = q.shape
    NC = S // CHUNK
    L = CHUNK

    def heads(t):
        return t.astype(jnp.float32).reshape(B, NC, L, N_HEADS, HEAD_DIM).transpose(0, 3, 1, 2, 4)

    def gate_heads(t):
        return t.astype(jnp.float32).reshape(B, NC, L, N_HEADS).transpose(0, 3, 1, 2)

    qh = heads(q)
    kh = heads(k) * (HEAD_DIM ** -0.5)
    vh = heads(v)
    ig = gate_heads(i_pre)
    lf = jax.nn.log_sigmoid(gate_heads(f_pre))

    b = jnp.cumsum(lf, axis=-1)
    b_last = b[..., -1]

    a = b_last[..., None] - b + ig
    a_max = jnp.max(a, axis=-1)
    w_a = jnp.exp(a - a_max[..., None])
    kv_chunk = jnp.einsum('bhcl,bhcld,bhcle->bhcde', w_a, kh, vh)
    n_chunk = jnp.einsum('bhcl,bhcld->bhcd', w_a, kh)

    def step(carry, xs):
        C, n, m = carry
        kv_c, n_c, bl, am = xs
        m_new = jnp.maximum(bl + m, am)
        s_old = jnp.exp(bl + m - m_new)
        s_new = jnp.exp(am - m_new)
        C_new = s_old[..., None, None] * C + s_new[..., None, None] * kv_c
        n_new = s_old[..., None] * n + s_new[..., None] * n_c
        return (C_new, n_new, m_new), (C, n, m)

    init = (jnp.zeros((B, N_HEADS, HEAD_DIM, HEAD_DIM), jnp.float32),
            jnp.zeros((B, N_HEADS, HEAD_DIM), jnp.float32),
            jnp.full((B, N_HEADS), M_INIT, jnp.float32))
    xs = (jnp.moveaxis(kv_chunk, 2, 0), jnp.moveaxis(n_chunk, 2, 0),
          jnp.moveaxis(b_last, 2, 0), jnp.moveaxis(a_max, 2, 0))
    _, (C_prev, n_prev, m_prev) = lax.scan(step, init, xs)
    C_prev = jnp.moveaxis(C_prev, 0, 2)
    n_prev = jnp.moveaxis(n_prev, 0, 2)
    m_prev = jnp.moveaxis(m_prev, 0, 2)

    causal = jnp.tril(jnp.ones((L, L), dtype=bool))
    g = b[..., :, None] - b[..., None, :] + ig[..., None, :]
    g = jnp.where(causal, g, -jnp.inf)
    li = b + m_prev[..., None]
    m_t = jnp.maximum(li, jnp.max(g, axis=-1))

    scores = jnp.einsum('bhcld,bhcsd->bhcls', qh, kh)
    w = jnp.exp(g - m_t[..., None]) * scores
    s_inter = jnp.exp(li - m_t)
    num = (jnp.einsum('bhcls,bhcse->bhcle', w, vh)
           + s_inter[..., None] * jnp.einsum('bhcld,bhcde->bhcle', qh, C_prev))
    den = jnp.sum(w, axis=-1) + s_inter * jnp.einsum('bhcld,bhcd->bhcl', qh, n_prev)
    h = num / jnp.maximum(jnp.abs(den), jnp.exp(-m_t))[..., None]
    return h.transpose(0, 2, 3, 1, 4).reshape(B, S, N_HEADS * HEAD_DIM)


def setup_inputs(seed: int = 0) -> dict:
    key = jax.random.key(seed)
    ks = jax.random.split(key, 12)
    f32 = jnp.float32
    x = jax.random.normal(ks[0], (BATCH, SEQ, D_MODEL), f32)
    norm_g = 1.0 + 0.02 * jax.random.normal(ks[1], (DEPTH, D_MODEL), f32)
    w_in = jax.random.normal(ks[2], (DEPTH, D_MODEL, D_IN_PROJ), f32) * D_MODEL ** -0.5
    i_bias = 0.1 * jax.random.normal(ks[3], (DEPTH, N_HEADS), f32)
    f_bias = (jnp.linspace(3.0, 6.0, N_HEADS, dtype=f32)[None, :]
              + 0.1 * jax.random.normal(ks[4], (DEPTH, N_HEADS), f32))
    b_gates = jnp.concatenate([i_bias, f_bias], axis=-1)
    mh_norm_g = 1.0 + 0.02 * jax.random.normal(ks[5], (DEPTH, D_MLSTM), f32)
    conv_w = jax.random.normal(ks[6], (DEPTH, CONV_WIDTH, D_CONV), f32) * CONV_WIDTH ** -0.5
    conv_b = 0.02 * jax.random.normal(ks[7], (DEPTH, D_CONV), f32)
    conv_ln_g = 1.0 + 0.02 * jax.random.normal(ks[8], (DEPTH, D_CONV), f32)
    conv_ln_b = 0.02 * jax.random.normal(ks[9], (DEPTH, D_CONV), f32)
    w_out = jax.random.normal(ks[10], (DEPTH, D_MIX, D_MODEL), f32) * D_MIX ** -0.5
    final_norm_g = 1.0 + 0.02 * jax.random.normal(ks[11], (D_MODEL,), f32)
    return {"x": x, "norm_g": norm_g, "w_in": w_in, "b_gates": b_gates,
            "mh_norm_g": mh_norm_g, "conv_w": conv_w, "conv_b": conv_b,
            "conv_ln_g": conv_ln_g, "conv_ln_b": conv_ln_b, "w_out": w_out,
            "final_norm_g": final_norm_g}


def reference(x, norm_g, w_in, b_gates, mh_norm_g, conv_w, conv_b, conv_ln_g,
              conv_ln_b, w_out, final_norm_g):
    B, S, _ = x.shape
    sizes = [D_MLSTM] * 5 + [N_HEADS, N_HEADS] + [D_CONV] * 3
    split_idx = [int(s) for s in np.cumsum(sizes)[:-1]]
    h = x
    for l in range(DEPTH):
        u = rmsnorm(h, norm_g[l])
        proj = jnp.einsum('bsd,de->bse', u, w_in[l])
        (q, k, v, o_pre, z_m, i_pre, f_pre,
         glu_a, glu_g, z_c) = jnp.split(proj, split_idx, axis=-1)

        i_pre = i_pre + b_gates[l, :N_HEADS]
        f_pre = f_pre + b_gates[l, N_HEADS:]
        hm = mlstm_chunkwise(q, k, v, i_pre, f_pre)
        hm = jax.nn.sigmoid(o_pre.astype(jnp.float32)) * hm
        hm = hm.reshape(B, S, N_HEADS, HEAD_DIM)
        hm = hm * lax.rsqrt(jnp.mean(hm * hm, axis=-1, keepdims=True) + EPS)
        hm = hm.reshape(B, S, D_MLSTM) * mh_norm_g[l].astype(jnp.float32)
        hm = (hm * jax.nn.silu(z_m.astype(jnp.float32))).astype(x.dtype)

        c = glu_a * jax.nn.sigmoid(glu_g)
        c = lax.conv_general_dilated(
            c, conv_w[l][:, None, :], window_strides=(1,),
            padding=[(CONV_WIDTH - 1, 0)],
            dimension_numbers=('NWC', 'WIO', 'NWC'),
            feature_group_count=D_CONV) + conv_b[l]
        c = layernorm(c, conv_ln_g[l], conv_ln_b[l])
        c = jax.nn.silu(c) * jax.nn.silu(z_c)

        mix = jnp.concatenate([hm, c.astype(x.dtype)], axis=-1)
        h = h + jnp.einsum('bse,ed->bsd', mix, w_out[l])
    return rmsnorm(h, final_norm_g)
```

```python
import functools

import jax
import jax.numpy as jnp
from jax import lax
from jax.experimental import pallas as pl
from jax.experimental.pallas import tpu as pltpu

N_HEADS = 4
CONV_WIDTH = 31
EPS = 1e-6
M_INIT = -1e30

MLSTM_CHUNK = 256
LANES = 128
SUBLANES = 8
CONV_HALO = 32
VMEM_LIMIT = 56 * 1024 * 1024

f32 = jnp.float32
bf16 = jnp.bfloat16

_NT = (((1,), (1,)), ((), ()))
_TN = (((0,), (0,)), ((), ()))


def _sigmoid(x):
    return 1.0 / (1.0 + jnp.exp(-x))


def _silu(x):
    return x * _sigmoid(x)


def _in_proj_kernel(x_ref, g_ref, w_ref, wg_ref, bg_ref, proj_ref, gates_ref, *, tn):
    x = x_ref[...]
    u = x * lax.rsqrt(jnp.mean(x * x, axis=-1, keepdims=True) + EPS) * g_ref[...]
    ub = u.astype(bf16)
    n_main = w_ref.shape[1]
    for c in range(n_main // tn):
        proj_ref[:, c * tn:(c + 1) * tn] = jnp.dot(
            ub, w_ref[:, c * tn:(c + 1) * tn], preferred_element_type=f32).astype(bf16)
    gates_ref[...] = lax.dot_general(wg_ref[...], ub, _NT, preferred_element_type=f32) + bg_ref[...]


def _in_proj(x2, g, w_main, wg_t, bg, *, tm=512, tn=512):
    t, d = x2.shape
    n_main = w_main.shape[1]
    return pl.pallas_call(
        functools.partial(_in_proj_kernel, tn=tn),
        grid=(t // tm,),
        in_specs=[
            pl.BlockSpec((tm, d), lambda i: (i, 0)),
            pl.BlockSpec((1, d), lambda i: (0, 0)),
            pl.BlockSpec((d, n_main), lambda i: (0, 0), pipeline_mode=pl.Buffered(1)),
            pl.BlockSpec((2 * N_HEADS, d), lambda i: (0, 0)),
            pl.BlockSpec((2 * N_HEADS, 1), lambda i: (0, 0)),
        ],
        out_specs=[
            pl.BlockSpec((tm, n_main), lambda i: (i, 0)),
            pl.BlockSpec((2 * N_HEADS, tm), lambda i: (0, i)),
        ],
        out_shape=[
            jax.ShapeDtypeStruct((t, n_main), bf16),
            jax.ShapeDtypeStruct((2 * N_HEADS, t), f32),
        ],
        compiler_params=pltpu.CompilerParams(
            dimension_semantics=("parallel",), vmem_limit_bytes=VMEM_LIMIT),
        name="in_proj",
    )(x2, g, w_main, wg_t, bg)


def _cumsum_lanes(x):
    n = x.shape[-1]
    lane = lax.broadcasted_iota(jnp.int32, x.shape, x.ndim - 1)
    s = 1
    while s < n:
        x = x + jnp.where(lane >= s, pltpu.roll(x, s, x.ndim - 1), 0.0)
        s *= 2
    return x


def _mlstm_kernel(q_ref, k_ref, v_ref, o_ref, z_ref, gates_ref, mhg_ref, out_ref,
                  c_ref, n_ref, m_ref, *, head_dim):
    L = q_ref.shape[0]
    scale = head_dim ** -0.5

    @pl.when(pl.program_id(1) == 0)
    def _():
        c_ref[...] = jnp.zeros_like(c_ref)
        n_ref[...] = jnp.zeros_like(n_ref)
        m_ref[...] = jnp.full_like(m_ref, M_INIT)

    gr = gates_ref[...]
    lf = jnp.minimum(gr, 0.0) - jnp.log1p(jnp.exp(-jnp.abs(gr)))
    bfull = _cumsum_lanes(lf)
    ig = gr[0:N_HEADS]
    b = bfull[N_HEADS:2 * N_HEADS]
    b_last = b[:, L - 1:L]
    m_prev = m_ref[0:N_HEADS, 0:1]
    a = b_last - b + ig
    a_max = jnp.max(a, axis=1, keepdims=True)
    m_new = jnp.maximum(b_last + m_prev, a_max)
    s_old = jnp.exp(b_last + m_prev - m_new)
    w_a = jnp.exp(a - m_new)
    c_row = ig - b

    cols = jnp.transpose(jnp.concatenate([b, w_a], axis=0))

    n_all = jnp.dot(jnp.concatenate([w_a, w_a], axis=0).astype(bf16), k_ref[...],
                    preferred_element_type=f32) * scale

    row_id = lax.broadcasted_iota(jnp.int32, (L, L), 0)
    col_id = lax.broadcasted_iota(jnp.int32, (L, L), 1)
    causal = col_id <= row_id

    for h in range(N_HEADS):
        sl = slice(h * head_dim, (h + 1) * head_dim)
        q = q_ref[:, sl]
        k = k_ref[:, sl]
        v = v_ref[:, sl]
        b_col = cols[:, h:h + 1]
        w_col = cols[:, N_HEADS + h:N_HEADS + h + 1]
        mp = m_prev[h:h + 1, :]

        s = lax.dot_general(q, k, _NT, preferred_element_type=f32) * scale
        g = jnp.where(causal, b_col + c_row[h:h + 1, :], -jnp.inf)
        li = b_col + mp
        m_t = jnp.maximum(li, jnp.max(g, axis=1, keepdims=True))
        p = jnp.exp(g - m_t) * s
        s_inter = jnp.exp(li - m_t)
        c_prev = c_ref[h]
        n_prev = n_ref[h:h + 1, :]
        num = (jnp.dot(p.astype(bf16), v, preferred_element_type=f32)
               + s_inter * jnp.dot(q, c_prev.astype(bf16), preferred_element_type=f32))
        den = (jnp.sum(p, axis=1, keepdims=True)
               + s_inter * jnp.sum(q.astype(f32) * n_prev, axis=1, keepdims=True))
        hh = num / jnp.maximum(jnp.abs(den), jnp.exp(-m_t))

        hh = _sigmoid(o_ref[:, sl].astype(f32)) * hh
        hh = hh * lax.rsqrt(jnp.mean(hh * hh, axis=-1, keepdims=True) + EPS)
        hh = hh * mhg_ref[:, sl]
        out_ref[:, sl] = (hh * _silu(z_ref[:, sl].astype(f32))).astype(out_ref.dtype)

        vw = (v.astype(f32) * w_col).astype(bf16)
        kv = lax.dot_general(k, vw, _TN, preferred_element_type=f32) * scale
        so = s_old[h:h + 1, :]
        c_ref[h] = so * c_prev + kv
        n_ref[h:h + 1, :] = so * n_prev + n_all[h:h + 1, sl]

    m_ref[0:N_HEADS, :] = jnp.broadcast_to(m_new, (N_HEADS, m_ref.shape[1]))


def _mlstm(proj, gates, mhg, *, batch, seq, d_mlstm):
    t = batch * seq
    L = MLSTM_CHUNK
    nc = seq // L
    hd = d_mlstm // N_HEADS

    def col(j):
        return pl.BlockSpec((L, d_mlstm), lambda b, c, j=j: (b * nc + c, j))

    return pl.pallas_call(
        functools.partial(_mlstm_kernel, head_dim=hd),
        grid=(batch, nc),
        in_specs=[col(0), col(1), col(2), col(3), col(4),
                  pl.BlockSpec((2 * N_HEADS, L), lambda b, c: (0, b * nc + c)),
                  pl.BlockSpec((1, d_mlstm), lambda b, c: (0, 0))],
        out_specs=pl.BlockSpec((L, d_mlstm), lambda b, c: (b * nc + c, 0)),
        out_shape=jax.ShapeDtypeStruct((t, d_mlstm), bf16),
        scratch_shapes=[pltpu.VMEM((N_HEADS, hd, hd), f32),
                        pltpu.VMEM((SUBLANES, hd), f32),
                        pltpu.VMEM((SUBLANES, LANES), f32)],
        compiler_params=pltpu.CompilerParams(
            dimension_semantics=("parallel", "arbitrary"), vmem_limit_bytes=VMEM_LIMIT),
        name="mlstm",
    )(proj, proj, proj, proj, proj, gates, mhg)


def _conv_kernel(a_ref, g_ref, z_ref, w_ref, cb_ref, lng_ref, lnb_ref, out_ref,
                 buf_ref, acc_ref, *, rb, lb):
    ts, dc = a_ref.shape
    halo = CONV_HALO
    lead = halo - (CONV_WIDTH - 1)

    @pl.when(pl.program_id(1) == 0)
    def _():
        buf_ref[0:halo, :] = jnp.zeros((halo, dc), f32)

    @pl.when(pl.program_id(1) > 0)
    def _():
        buf_ref[0:halo, :] = buf_ref[ts:ts + halo, :]

    buf_ref[halo:halo + ts, :] = a_ref[...].astype(f32) * _sigmoid(g_ref[...].astype(f32))

    def lane_block(c, carry):
        cs = pl.multiple_of(c * lb, lb)
        for r in range(ts // rb):
            acc = jnp.zeros((rb, lb), f32)
            for j in range(CONV_WIDTH):
                r0 = r * rb + lead + j
                acc = acc + w_ref[j:j + 1, pl.ds(cs, lb)] * buf_ref[r0:r0 + rb, pl.ds(cs, lb)]
            acc_ref[r * rb:(r + 1) * rb, pl.ds(cs, lb)] = acc + cb_ref[:, pl.ds(cs, lb)]
        return carry

    lax.fori_loop(0, dc // lb, lane_block, 0)

    def row_block(r, carry):
        rs = pl.multiple_of(r * rb, rb)
        y = acc_ref[pl.ds(rs, rb), :]
        mu = jnp.mean(y, axis=-1, keepdims=True)
        yc = y - mu
        var = jnp.mean(yc * yc, axis=-1, keepdims=True)
        yn = yc * lax.rsqrt(var + EPS) * lng_ref[...] + lnb_ref[...]
        out = _silu(yn) * _silu(z_ref[pl.ds(rs, rb), :].astype(f32))
        out_ref[pl.ds(rs, rb), :] = out.astype(out_ref.dtype)
        return carry

    lax.fori_loop(0, ts // rb, row_block, 0)


def _conv(proj, conv_w, conv_b, ln_g, ln_b, *, batch, seq, d_conv, col0, ts=256, rb=64, lb=128):
    t = batch * seq
    nt = seq // ts

    def col(j):
        return pl.BlockSpec((ts, d_conv), lambda b, i, j=j: (b * nt + i, col0 + j))

    vec = pl.BlockSpec((1, d_conv), lambda b, i: (0, 0))
    return pl.pallas_call(
        functools.partial(_conv_kernel, rb=rb, lb=lb),
        grid=(batch, nt),
        in_specs=[col(0), col(1), col(2),
                  pl.BlockSpec((CONV_WIDTH, d_conv), lambda b, i: (0, 0)),
                  vec, vec, vec],
        out_specs=pl.BlockSpec((ts, d_conv), lambda b, i: (b * nt + i, 0)),
        out_shape=jax.ShapeDtypeStruct((t, d_conv), bf16),
        scratch_shapes=[pltpu.VMEM((ts + CONV_HALO, d_conv), f32),
                        pltpu.VMEM((ts, d_conv), f32)],
        compiler_params=pltpu.CompilerParams(
            dimension_semantics=("parallel", "arbitrary"), vmem_limit_bytes=VMEM_LIMIT),
        name="conv",
    )(proj, proj, proj, conv_w, conv_b, ln_g, ln_b)


def _out_proj_kernel(x_ref, hm_ref, c_ref, wa_ref, wc_ref, fg_ref, out_ref, *, final_norm):
    y = (x_ref[...]
         + jnp.dot(hm_ref[...], wa_ref[...], preferred_element_type=f32)
         + jnp.dot(c_ref[...], wc_ref[...], preferred_element_type=f32))
    if final_norm:
        y = y * lax.rsqrt(jnp.mean(y * y, axis=-1, keepdims=True) + EPS) * fg_ref[...]
    out_ref[...] = y


def _out_proj(x2, hm, c, w_a, w_c, fg, *, final_norm, tm=512):
    t, d = x2.shape
    da = hm.shape[1]
    dcv = c.shape[1]
    return pl.pallas_call(
        functools.partial(_out_proj_kernel, final_norm=final_norm),
        grid=(t // tm,),
        in_specs=[
            pl.BlockSpec((tm, d), lambda i: (i, 0)),
            pl.BlockSpec((tm, da), lambda i: (i, 0)),
            pl.BlockSpec((tm, dcv), lambda i: (i, 0)),
            pl.BlockSpec((da, d), lambda i: (0, 0)),
            pl.BlockSpec((dcv, d), lambda i: (0, 0)),
            pl.BlockSpec((1, d), lambda i: (0, 0)),
        ],
        out_specs=pl.BlockSpec((tm, d), lambda i: (i, 0)),
        out_shape=jax.ShapeDtypeStruct((t, d), f32),
        compiler_params=pltpu.CompilerParams(
            dimension_semantics=("parallel",), vmem_limit_bytes=VMEM_LIMIT),
        name="out_proj",
    )(x2, hm, c, w_a, w_c, fg)


def kernel(x, norm_g, w_in, b_gates, mh_norm_g, conv_w, conv_b, conv_ln_g, conv_ln_b, w_out, final_norm_g):
    batch, seq, d_model = x.shape
    depth = norm_g.shape[0]
    d_mlstm = mh_norm_g.shape[1]
    d_conv = conv_b.shape[1]
    n_gate = 2 * N_HEADS
    g0 = 5 * d_mlstm

    h = x.reshape(batch * seq, d_model)
    for l in range(depth):
        w_main = jnp.concatenate([w_in[l, :, :g0], w_in[l, :, g0 + n_gate:]], axis=1).astype(bf16)
        wg_t = w_in[l, :, g0:g0 + n_gate].T.astype(bf16)
        proj, gates = _in_proj(h, norm_g[l][None, :], w_main, wg_t, b_gates[l][:, None])
        hm = _mlstm(proj, gates, mh_norm_g[l][None, :], batch=batch, seq=seq, d_mlstm=d_mlstm)
        c = _conv(proj, conv_w[l], conv_b[l][None, :], conv_ln_g[l][None, :], conv_ln_b[l][None, :],
                  batch=batch, seq=seq, d_conv=d_conv, col0=5 * d_mlstm // d_conv)
        wo = w_out[l].astype(bf16)
        h = _out_proj(h, hm, c, wo[:d_mlstm], wo[d_mlstm:], final_norm_g[None, :],
                      final_norm=(l == depth - 1))
    return h.reshape(batch, seq, d_model)
```

```python
import functools

import jax
import jax.numpy as jnp
from jax import lax
from jax.experimental import pallas as pl
from jax.experimental.pallas import tpu as pltpu

N_HEADS = 4
CONV_WIDTH = 31
EPS = 1e-6
M_INIT = -1e30

MLSTM_CHUNK = 256
LANES = 128
SUBLANES = 8
CONV_HALO = 32
VMEM_LIMIT = 56 * 1024 * 1024

f32 = jnp.float32
bf16 = jnp.bfloat16

_NT = (((1,), (1,)), ((), ()))
_TN = (((0,), (0,)), ((), ()))


def _sigmoid(x):
    return 0.5 * jnp.tanh(0.5 * x) + 0.5


def _silu(x):
    hx = 0.5 * x
    return hx * jnp.tanh(hx) + hx


def _in_proj_kernel(x_ref, g_ref, w_ref, wg_ref, bg_ref, proj_ref, gates_ref, *, tn):
    x = x_ref[...]
    u = x * lax.rsqrt(jnp.mean(x * x, axis=-1, keepdims=True) + EPS) * g_ref[...]
    ub = u.astype(bf16)
    n_main = w_ref.shape[1]
    for c in range(n_main // tn):
        proj_ref[:, c * tn:(c + 1) * tn] = jnp.dot(
            ub, w_ref[:, c * tn:(c + 1) * tn], preferred_element_type=f32).astype(bf16)
    gates_ref[...] = lax.dot_general(wg_ref[...], ub, _NT, preferred_element_type=f32) + bg_ref[...]


def _in_proj(x2, g, w_main, wg_t, bg, *, tm=512, tn=512):
    t, d = x2.shape
    n_main = w_main.shape[1]
    return pl.pallas_call(
        functools.partial(_in_proj_kernel, tn=tn),
        grid=(t // tm,),
        in_specs=[
            pl.BlockSpec((tm, d), lambda i: (i, 0)),
            pl.BlockSpec((1, d), lambda i: (0, 0)),
            pl.BlockSpec((d, n_main), lambda i: (0, 0), pipeline_mode=pl.Buffered(1)),
            pl.BlockSpec((2 * N_HEADS, d), lambda i: (0, 0)),
            pl.BlockSpec((2 * N_HEADS, 1), lambda i: (0, 0)),
        ],
        out_specs=[
            pl.BlockSpec((tm, n_main), lambda i: (i, 0)),
            pl.BlockSpec((2 * N_HEADS, tm), lambda i: (0, i)),
        ],
        out_shape=[
            jax.ShapeDtypeStruct((t, n_main), bf16),
            jax.ShapeDtypeStruct((2 * N_HEADS, t), f32),
        ],
        compiler_params=pltpu.CompilerParams(
            dimension_semantics=("parallel",), vmem_limit_bytes=VMEM_LIMIT),
        name="in_proj",
    )(x2, g, w_main, wg_t, bg)


def _cumsum_lanes(x):
    n = x.shape[-1]
    lane = lax.broadcasted_iota(jnp.int32, x.shape, x.ndim - 1)
    s = 1
    while s < n:
        x = x + jnp.where(lane >= s, pltpu.roll(x, s, x.ndim - 1), 0.0)
        s *= 2
    return x


def _mlstm_kernel(q_ref, k_ref, v_ref, o_ref, z_ref, gates_ref, mhg_ref, out_ref,
                  c_ref, n_ref, m_ref, *, head_dim):
    L = q_ref.shape[0]
    scale = head_dim ** -0.5

    @pl.when(pl.program_id(1) == 0)
    def _():
        c_ref[...] = jnp.zeros_like(c_ref)
        n_ref[...] = jnp.zeros_like(n_ref)
        m_ref[...] = jnp.full_like(m_ref, M_INIT)

    gr = gates_ref[...]
    lf = jnp.minimum(gr, 0.0) - jnp.log1p(jnp.exp(-jnp.abs(gr)))
    bfull = _cumsum_lanes(lf)
    ig = gr[0:N_HEADS]
    b = bfull[N_HEADS:2 * N_HEADS]
    b_last = b[:, L - 1:L]
    m_prev = m_ref[0:N_HEADS, 0:1]
    a = b_last - b + ig
    a_max = jnp.max(a, axis=1, keepdims=True)
    m_new = jnp.maximum(b_last + m_prev, a_max)
    s_old = jnp.exp(b_last + m_prev - m_new)
    w_a = jnp.exp(a - m_new)
    c_row = ig - b

    cols = jnp.transpose(jnp.concatenate([b, w_a], axis=0))

    n_all = jnp.dot(jnp.concatenate([w_a, w_a], axis=0).astype(bf16), k_ref[...],
                    preferred_element_type=f32) * scale

    row_id = lax.broadcasted_iota(jnp.int32, (L, L), 0)
    col_id = lax.broadcasted_iota(jnp.int32, (L, L), 1)
    causal = col_id <= row_id

    for h in range(N_HEADS):
        sl = slice(h * head_dim, (h + 1) * head_dim)
        q = q_ref[:, sl]
        k = k_ref[:, sl]
        v = v_ref[:, sl]
        b_col = cols[:, h:h + 1]
        w_col = cols[:, N_HEADS + h:N_HEADS + h + 1]
        mp = m_prev[h:h + 1, :]

        s = lax.dot_general(q, k, _NT, preferred_element_type=f32) * scale
        g = jnp.where(causal, b_col + c_row[h:h + 1, :], -jnp.inf)
        li = b_col + mp
        m_t = jnp.maximum(li, jnp.max(g, axis=1, keepdims=True))
        p = jnp.exp(g - m_t) * s
        s_inter = jnp.exp(li - m_t)
        c_prev = c_ref[h]
        n_prev = n_ref[h:h + 1, :]
        num = (jnp.dot(p.astype(bf16), v, preferred_element_type=f32)
               + s_inter * jnp.dot(q, c_prev.astype(bf16), preferred_element_type=f32))
        den = (jnp.sum(p, axis=1, keepdims=True)
               + s_inter * jnp.sum(q.astype(f32) * n_prev, axis=1, keepdims=True))
        hh = num * (1.0 / jnp.maximum(jnp.abs(den), jnp.exp(-m_t)))

        hh = _sigmoid(o_ref[:, sl].astype(f32)) * hh
        hh = hh * lax.rsqrt(jnp.mean(hh * hh, axis=-1, keepdims=True) + EPS)
        hh = hh * mhg_ref[:, sl]
        out_ref[:, sl] = (hh * _silu(z_ref[:, sl].astype(f32))).astype(out_ref.dtype)

        vw = (v.astype(f32) * w_col).astype(bf16)
        kv = lax.dot_general(k, vw, _TN, preferred_element_type=f32) * scale
        so = s_old[h:h + 1, :]
        c_ref[h] = so * c_prev + kv
        n_ref[h:h + 1, :] = so * n_prev + n_all[h:h + 1, sl]

    m_ref[0:N_HEADS, :] = jnp.broadcast_to(m_new, (N_HEADS, m_ref.shape[1]))


def _mlstm(proj, gates, mhg, *, batch, seq, d_mlstm):
    t = batch * seq
    L = MLSTM_CHUNK
    nc = seq // L
    hd = d_mlstm // N_HEADS

    def col(j):
        return pl.BlockSpec((L, d_mlstm), lambda b, c, j=j: (b * nc + c, j))

    return pl.pallas_call(
        functools.partial(_mlstm_kernel, head_dim=hd),
        grid=(batch, nc),
        in_specs=[col(0), col(1), col(2), col(3), col(4),
                  pl.BlockSpec((2 * N_HEADS, L), lambda b, c: (0, b * nc + c)),
                  pl.BlockSpec((1, d_mlstm), lambda b, c: (0, 0))],
        out_specs=pl.BlockSpec((L, d_mlstm), lambda b, c: (b * nc + c, 0)),
        out_shape=jax.ShapeDtypeStruct((t, d_mlstm), bf16),
        scratch_shapes=[pltpu.VMEM((N_HEADS, hd, hd), f32),
                        pltpu.VMEM((SUBLANES, hd), f32),
                        pltpu.VMEM((SUBLANES, LANES), f32)],
        compiler_params=pltpu.CompilerParams(
            dimension_semantics=("parallel", "arbitrary"), vmem_limit_bytes=VMEM_LIMIT),
        name="mlstm",
    )(proj, proj, proj, proj, proj, gates, mhg)


def _conv_rows(buf_ref, w_ref, lanes, t0, rb):
    acc = None
    for r in range(SUBLANES):
        nq = (CONV_WIDTH - 1 - r) // SUBLANES + 1
        lo = 0 if r == 0 else SUBLANES
        p = None
        for q in range(nq):
            j = CONV_WIDTH - 1 - (SUBLANES * q + r)
            start = CONV_HALO + t0 - lo - SUBLANES * q
            term = w_ref[j:j + 1, lanes] * buf_ref[start:start + rb + lo, lanes]
            p = term if p is None else p + term
        acc = p if r == 0 else acc + p[SUBLANES - r:SUBLANES - r + rb, :]
    return acc


def _conv_kernel(a_ref, g_ref, z_ref, w_ref, cb_ref, lng_ref, lnb_ref, out_ref,
                 buf_ref, acc_ref, *, rb, lb):
    ts, dc = a_ref.shape
    halo = CONV_HALO

    @pl.when(pl.program_id(1) == 0)
    def _():
        buf_ref[0:halo, :] = jnp.zeros((halo, dc), f32)

    @pl.when(pl.program_id(1) > 0)
    def _():
        buf_ref[0:halo, :] = buf_ref[ts:ts + halo, :]

    buf_ref[halo:halo + ts, :] = a_ref[...].astype(f32) * _sigmoid(g_ref[...].astype(f32))

    def lane_block(c, carry):
        cs = pl.multiple_of(c * lb, lb)
        for r in range(ts // rb):
            acc = _conv_rows(buf_ref, w_ref, pl.ds(cs, lb), r * rb, rb)
            acc_ref[r * rb:(r + 1) * rb, pl.ds(cs, lb)] = acc + cb_ref[:, pl.ds(cs, lb)]
        return carry

    lax.fori_loop(0, dc // lb, lane_block, 0)

    def row_block(r, carry):
        rs = pl.multiple_of(r * rb, rb)
        y = acc_ref[pl.ds(rs, rb), :]
        mu = jnp.mean(y, axis=-1, keepdims=True)
        yc = y - mu
        var = jnp.mean(yc * yc, axis=-1, keepdims=True)
        yn = yc * lax.rsqrt(var + EPS) * lng_ref[...] + lnb_ref[...]
        out = _silu(yn) * _silu(z_ref[pl.ds(rs, rb), :].astype(f32))
        out_ref[pl.ds(rs, rb), :] = out.astype(out_ref.dtype)
        return carry

    lax.fori_loop(0, ts // rb, row_block, 0)


def _conv(proj, conv_w, conv_b, ln_g, ln_b, *, batch, seq, d_conv, col0, ts=256, rb=64, lb=128):
    t = batch * seq
    nt = seq // ts

    def col(j):
        return pl.BlockSpec((ts, d_conv), lambda b, i, j=j: (b * nt + i, col0 + j))

    vec = pl.BlockSpec((1, d_conv), lambda b, i: (0, 0))
    return pl.pallas_call(
        functools.partial(_conv_kernel, rb=rb, lb=lb),
        grid=(batch, nt),
        in_specs=[col(0), col(1), col(2),
                  pl.BlockSpec((CONV_WIDTH, d_conv), lambda b, i: (0, 0)),
                  vec, vec, vec],
        out_specs=pl.BlockSpec((ts, d_conv), lambda b, i: (b * nt + i, 0)),
        out_shape=jax.ShapeDtypeStruct((t, d_conv), bf16),
        scratch_shapes=[pltpu.VMEM((ts + CONV_HALO, d_conv), f32),
                        pltpu.VMEM((ts, d_conv), f32)],
        compiler_params=pltpu.CompilerParams(
            dimension_semantics=("parallel", "arbitrary"), vmem_limit_bytes=VMEM_LIMIT),
        name="conv",
    )(proj, proj, proj, conv_w, conv_b, ln_g, ln_b)


def _out_proj_kernel(x_ref, hm_ref, c_ref, wa_ref, wc_ref, fg_ref, out_ref, *, final_norm):
    y = (x_ref[...]
         + jnp.dot(hm_ref[...], wa_ref[...], preferred_element_type=f32)
         + jnp.dot(c_ref[...], wc_ref[...], preferred_element_type=f32))
    if final_norm:
        y = y * lax.rsqrt(jnp.mean(y * y, axis=-1, keepdims=True) + EPS) * fg_ref[...]
    out_ref[...] = y


def _out_proj(x2, hm, c, w_a, w_c, fg, *, final_norm, tm=512):
    t, d = x2.shape
    da = hm.shape[1]
    dcv = c.shape[1]
    return pl.pallas_call(
        functools.partial(_out_proj_kernel, final_norm=final_norm),
        grid=(t // tm,),
        in_specs=[
            pl.BlockSpec((tm, d), lambda i: (i, 0)),
            pl.BlockSpec((tm, da), lambda i: (i, 0)),
            pl.BlockSpec((tm, dcv), lambda i: (i, 0)),
            pl.BlockSpec((da, d), lambda i: (0, 0)),
            pl.BlockSpec((dcv, d), lambda i: (0, 0)),
            pl.BlockSpec((1, d), lambda i: (0, 0)),
        ],
        out_specs=pl.BlockSpec((tm, d), lambda i: (i, 0)),
        out_shape=jax.ShapeDtypeStruct((t, d), f32),
        compiler_params=pltpu.CompilerParams(
            dimension_semantics=("parallel",), vmem_limit_bytes=VMEM_LIMIT),
        name="out_proj",
    )(x2, hm, c, w_a, w_c, fg)


def kernel(x, norm_g, w_in, b_gates, mh_norm_g, conv_w, conv_b, conv_ln_g, conv_ln_b, w_out, final_norm_g):
    batch, seq, d_model = x.shape
    depth = norm_g.shape[0]
    d_mlstm = mh_norm_g.shape[1]
    d_conv = conv_b.shape[1]
    n_gate = 2 * N_HEADS
    g0 = 5 * d_mlstm

    h = x.reshape(batch * seq, d_model)
    for l in range(depth):
        w_main = jnp.concatenate([w_in[l, :, :g0], w_in[l, :, g0 + n_gate:]], axis=1).astype(bf16)
        wg_t = w_in[l, :, g0:g0 + n_gate].T.astype(bf16)
        proj, gates = _in_proj(h, norm_g[l][None, :], w_main, wg_t, b_gates[l][:, None])
        hm = _mlstm(proj, gates, mh_norm_g[l][None, :], batch=batch, seq=seq, d_mlstm=d_mlstm)
        c = _conv(proj, conv_w[l], conv_b[l][None, :], conv_ln_g[l][None, :], conv_ln_b[l][None, :],
                  batch=batch, seq=seq, d_conv=d_conv, col0=5 * d_mlstm // d_conv)
        wo = w_out[l].astype(bf16)
        h = _out_proj(h, hm, c, wo[:d_mlstm], wo[d_mlstm:], final_norm_g[None, :],
                      final_norm=(l == depth - 1))
    return h.reshape(batch, seq, d_model)
```

```python
import functools

import jax
import jax.numpy as jnp
from jax import lax
from jax.experimental import pallas as pl
from jax.experimental.pallas import tpu as pltpu

N_HEADS = 4
CONV_WIDTH = 31
EPS = 1e-6
M_INIT = -1e30

MLSTM_CHUNK = 256
LANES = 128
SUBLANES = 8
CONV_HALO = 32
GLU_CHAINS = 3
VMEM_LIMIT = 56 * 1024 * 1024

f32 = jnp.float32
bf16 = jnp.bfloat16

_NT = (((1,), (1,)), ((), ()))
_TN = (((0,), (0,)), ((), ()))


def _zero_bits(x):
    b = pltpu.bitcast(x, jnp.uint32)
    return lax.shift_right_logical(lax.shift_right_logical(b, jnp.uint32(16)), jnp.uint32(16))


def _sigmoid(x):
    return 0.5 * jnp.tanh(0.5 * x) + 0.5


def _silu(x):
    hx = 0.5 * x
    return hx * jnp.tanh(hx) + hx


def _conv_rows(x, w_ref, lanes, rb, dep):
    acc = None
    for r in range(SUBLANES):
        nq = (CONV_WIDTH - 1 - r) // SUBLANES + 1
        lo = 0 if r == 0 else SUBLANES
        p = None
        for q in range(nq):
            j = CONV_WIDTH - 1 - (SUBLANES * q + r)
            start = CONV_HALO - lo - SUBLANES * q
            wj = w_ref[j:j + 1, lanes]
            if dep is not None:
                wj = pltpu.bitcast(pltpu.bitcast(wj, jnp.uint32) | dep, f32)
            term = wj * x[start:start + rb + lo, :]
            p = term if p is None else p + term
        acc = p if r == 0 else acc + p[SUBLANES - r:SUBLANES - r + rb, :]
        dep = _zero_bits(acc[0:1, :])
    return acc, dep


def _proj_conv_kernel(x_ref, g_ref, w_ref, wg_ref, bg_ref, cw_ref, cb_ref, lng_ref, lnb_ref,
                      proj_ref, gates_ref, c_ref,
                      u_ref, ag_ref, z_ref, buf_ref, acc_ref, *, tiles_per_seq, tn, rb):
    i = pl.program_id(0)
    tm, d = x_ref.shape
    dc = c_ref.shape[1]
    halo = CONV_HALO
    wr = i % 2
    rd = 1 - wr

    @pl.when(i == 0)
    def _():
        ag_ref[...] = jnp.zeros_like(ag_ref)
        z_ref[...] = jnp.zeros_like(z_ref)
        buf_ref[...] = jnp.zeros_like(buf_ref)

    first_of_seq = ((i + tiles_per_seq - 1) % tiles_per_seq) == 0

    @pl.when(first_of_seq)
    def _():
        buf_ref[0:halo, :] = jnp.zeros((halo, dc), f32)

    @pl.when(jnp.logical_not(first_of_seq))
    def _():
        buf_ref[0:halo, :] = buf_ref[tm:tm + halo, :]

    x = x_ref[...]
    u = x * lax.rsqrt(jnp.mean(x * x, axis=-1, keepdims=True) + EPS) * g_ref[...]
    u_ref[...] = u.astype(bf16)
    gates_ref[...] = lax.dot_general(wg_ref[...], u_ref[...], _NT, preferred_element_type=f32) + bg_ref[...]

    def matmul(col):
        return jnp.dot(u_ref[...], w_ref[:, col:col + tn], preferred_element_type=f32).astype(bf16)

    def glu_item(l0, n):
        def compute():
            halves = [jnp.full((1, n), 0.5, f32)] * GLU_CHAINS
            ys = []
            for j, r0 in enumerate(range(0, tm, rb)):
                half = halves[j % GLU_CHAINS]
                a = ag_ref[r0:r0 + rb, l0:l0 + n].astype(f32)
                g = ag_ref[r0:r0 + rb, dc + l0:dc + l0 + n].astype(f32)
                y = a * (half * jnp.tanh(half * g) + half)
                halves[j % GLU_CHAINS] = pltpu.bitcast(
                    pltpu.bitcast(half, jnp.uint32) | _zero_bits(y[0:1, :]), f32)
                ys.append(y)
            return ys

        def store(ys):
            for j, y in enumerate(ys):
                buf_ref[halo + j * rb:halo + (j + 1) * rb, l0:l0 + n] = y
        return compute, store

    def conv_item(units):
        def compute():
            ys, dep = [], None
            for t0, l0 in units:
                x = buf_ref[t0:t0 + halo + rb, l0:l0 + LANES]
                y, dep = _conv_rows(x, cw_ref, slice(l0, l0 + LANES), rb, dep)
                ys.append(y + cb_ref[:, l0:l0 + LANES])
            return ys

        def store(ys):
            for y, (t0, l0) in zip(ys, units):
                acc_ref[t0:t0 + rb, l0:l0 + LANES] = y
        return compute, store

    def ln_item(r0, n):
        def compute():
            y = acc_ref[r0:r0 + n, :]
            mu = jnp.mean(y, axis=-1, keepdims=True)
            yc = y - mu
            var = jnp.mean(yc * yc, axis=-1, keepdims=True)
            yn = yc * lax.rsqrt(var + EPS) * lng_ref[...] + lnb_ref[...]
            return (_silu(yn) * _silu(z_ref[rd, r0:r0 + n, :].astype(f32))).astype(c_ref.dtype)

        def store(out):
            c_ref[r0:r0 + n, :] = out
        return compute, store

    def to_proj(col, val):
        proj_ref[:, col:col + tn] = val

    def to_ag(col, val):
        ag_ref[:, col - a0:col - a0 + tn] = val

    def to_z(col, val):
        z_ref[wr, :, col - zc0:col - zc0 + tn] = val

    q0, k0, v0, o0, zm0, a0, g0, zc0 = (j * d for j in range(8))
    n_grp = d // tn
    n_chunks = 8 * n_grp
    mm_order = ([(to_proj, q0 + j * tn) for j in range(2 * n_grp)]
                + [(to_ag, a0 + j * tn) for j in range(2 * n_grp)]
                + [(to_proj, v0 + j * tn) for j in range(2 * n_grp)]
                + [(to_proj, zm0 + j * tn) for j in range(n_grp)]
                + [(to_z, zc0 + j * tn) for j in range(n_grp)])
    n_glu = max(n_grp // 2, 1)
    n_ln = 2 * n_grp
    n_conv = n_chunks - n_glu - n_ln
    assert n_glu <= 2 * n_grp
    units = [(t0, l0) for l0 in range(0, dc, LANES) for t0 in range(0, tm, rb)]
    cuts = [len(units) * j // n_conv for j in range(n_conv + 1)]
    work = ([glu_item(j * (dc // n_glu), dc // n_glu) for j in range(n_glu)]
            + [conv_item(units[cuts[j]:cuts[j + 1]]) for j in range(n_conv)]
            + [ln_item(j * (tm // n_ln), tm // n_ln) for j in range(n_ln)])

    once = jnp.minimum(i, 0) + 1

    for (mm_store, col), (compute, store) in zip(mm_order, work):
        def region(_, carry, mm_store=mm_store, col=col, compute=compute, store=store):
            out = compute()
            res = matmul(col)
            mm_store(col, res)
            store(out)
            return carry
        lax.fori_loop(0, once, region, 0)


def _proj_conv(x2, g, w_main, wg_t, bg, conv_w, conv_b, ln_g, ln_b, *, seq, d_conv, tm=512, tn=512, rb=64):
    t, d = x2.shape
    n_main = w_main.shape[1]
    dm = n_main - 3 * d_conv
    assert d == d_conv and n_main == 8 * d and d % tn == 0 and seq % tm == 0 and tm % rb == 0
    n = t // tm
    last = n - 1
    const = lambda i: (0, 0)
    vec = pl.BlockSpec((1, d_conv), const)
    return pl.pallas_call(
        functools.partial(_proj_conv_kernel, tiles_per_seq=seq // tm, tn=tn, rb=rb),
        grid=(n + 1,),
        in_specs=[
            pl.BlockSpec((tm, d), lambda i: (jnp.minimum(i, last), 0)),
            pl.BlockSpec((1, d), const),
            pl.BlockSpec((d, n_main), const, pipeline_mode=pl.Buffered(1)),
            pl.BlockSpec((2 * N_HEADS, d), const),
            pl.BlockSpec((2 * N_HEADS, 1), const),
            pl.BlockSpec((CONV_WIDTH, d_conv), const),
            vec, vec, vec,
        ],
        out_specs=[
            pl.BlockSpec((tm, dm), lambda i: (jnp.minimum(i, last), 0)),
            pl.BlockSpec((2 * N_HEADS, tm), lambda i: (0, jnp.minimum(i, last))),
            pl.BlockSpec((tm, d_conv), lambda i: (jnp.maximum(i - 1, 0), 0)),
        ],
        out_shape=[
            jax.ShapeDtypeStruct((t, dm), bf16),
            jax.ShapeDtypeStruct((2 * N_HEADS, t), f32),
            jax.ShapeDtypeStruct((t, d_conv), bf16),
        ],
        scratch_shapes=[
            pltpu.VMEM((tm, d), bf16),
            pltpu.VMEM((tm, 2 * d_conv), bf16),
            pltpu.VMEM((2, tm, d_conv), bf16),
            pltpu.VMEM((tm + CONV_HALO, d_conv), f32),
            pltpu.VMEM((tm, d_conv), f32),
        ],
        compiler_params=pltpu.CompilerParams(
            dimension_semantics=("arbitrary",), vmem_limit_bytes=VMEM_LIMIT),
        name="proj_conv",
    )(x2, g, w_main, wg_t, bg, conv_w, conv_b, ln_g, ln_b)


def _cumsum_lanes(x):
    n = x.shape[-1]
    lane = lax.broadcasted_iota(jnp.int32, x.shape, x.ndim - 1)
    s = 1
    while s < n:
        x = x + jnp.where(lane >= s, pltpu.roll(x, s, x.ndim - 1), 0.0)
        s *= 2
    return x


def _mlstm_kernel(q_ref, k_ref, v_ref, o_ref, z_ref, gates_ref, mhg_ref, out_ref,
                  c_ref, n_ref, m_ref, *, head_dim):
    L = q_ref.shape[0]
    scale = head_dim ** -0.5

    @pl.when(pl.program_id(1) == 0)
    def _():
        c_ref[...] = jnp.zeros_like(c_ref)
        n_ref[...] = jnp.zeros_like(n_ref)
        m_ref[...] = jnp.full_like(m_ref, M_INIT)

    gr = gates_ref[...]
    lf = jnp.minimum(gr, 0.0) - jnp.log1p(jnp.exp(-jnp.abs(gr)))
    bfull = _cumsum_lanes(lf)
    ig = gr[0:N_HEADS]
    b = bfull[N_HEADS:2 * N_HEADS]
    b_last = b[:, L - 1:L]
    m_prev = m_ref[0:N_HEADS, 0:1]
    a = b_last - b + ig
    a_max = jnp.max(a, axis=1, keepdims=True)
    m_new = jnp.maximum(b_last + m_prev, a_max)
    s_old = jnp.exp(b_last + m_prev - m_new)
    w_a = jnp.exp(a - m_new)
    c_row = ig - b

    cols = jnp.transpose(jnp.concatenate([b, w_a], axis=0))

    n_all = jnp.dot(jnp.concatenate([w_a, w_a], axis=0).astype(bf16), k_ref[...],
                    preferred_element_type=f32) * scale

    row_id = lax.broadcasted_iota(jnp.int32, (L, L), 0)
    col_id = lax.broadcasted_iota(jnp.int32, (L, L), 1)
    causal = col_id <= row_id

    for h in range(N_HEADS):
        sl = slice(h * head_dim, (h + 1) * head_dim)
        q = q_ref[:, sl]
        k = k_ref[:, sl]
        v = v_ref[:, sl]
        b_col = cols[:, h:h + 1]
        w_col = cols[:, N_HEADS + h:N_HEADS + h + 1]
        mp = m_prev[h:h + 1, :]

        s = lax.dot_general(q, k, _NT, preferred_element_type=f32) * scale
        g = jnp.where(causal, b_col + c_row[h:h + 1, :], -jnp.inf)
        li = b_col + mp
        m_t = jnp.maximum(li, jnp.max(g, axis=1, keepdims=True))
        p = jnp.exp(g - m_t) * s
        s_inter = jnp.exp(li - m_t)
        c_prev = c_ref[h]
        n_prev = n_ref[h:h + 1, :]
        num = (jnp.dot(p.astype(bf16), v, preferred_element_type=f32)
               + s_inter * jnp.dot(q, c_prev.astype(bf16), preferred_element_type=f32))
        den = (jnp.sum(p, axis=1, keepdims=True)
               + s_inter * jnp.sum(q.astype(f32) * n_prev, axis=1, keepdims=True))
        hh = num * (1.0 / jnp.maximum(jnp.abs(den), jnp.exp(-m_t)))

        hh = _sigmoid(o_ref[:, sl].astype(f32)) * hh
        hh = hh * lax.rsqrt(jnp.mean(hh * hh, axis=-1, keepdims=True) + EPS)
        hh = hh * mhg_ref[:, sl]
        out_ref[:, sl] = (hh * _silu(z_ref[:, sl].astype(f32))).astype(out_ref.dtype)

        vw = (v.astype(f32) * w_col).astype(bf16)
        kv = lax.dot_general(k, vw, _TN, preferred_element_type=f32) * scale
        so = s_old[h:h + 1, :]
        c_ref[h] = so * c_prev + kv
        n_ref[h:h + 1, :] = so * n_prev + n_all[h:h + 1, sl]

    m_ref[0:N_HEADS, :] = jnp.broadcast_to(m_new, (N_HEADS, m_ref.shape[1]))


def _mlstm(proj, gates, mhg, *, batch, seq, d_mlstm):
    t = batch * seq
    L = MLSTM_CHUNK
    nc = seq // L
    hd = d_mlstm // N_HEADS

    def col(j):
        return pl.BlockSpec((L, d_mlstm), lambda b, c, j=j: (b * nc + c, j))

    return pl.pallas_call(
        functools.partial(_mlstm_kernel, head_dim=hd),
        grid=(batch, nc),
        in_specs=[col(0), col(1), col(2), col(3), col(4),
                  pl.BlockSpec((2 * N_HEADS, L), lambda b, c: (0, b * nc + c)),
                  pl.BlockSpec((1, d_mlstm), lambda b, c: (0, 0))],
        out_specs=pl.BlockSpec((L, d_mlstm), lambda b, c: (b * nc + c, 0)),
        out_shape=jax.ShapeDtypeStruct((t, d_mlstm), bf16),
        scratch_shapes=[pltpu.VMEM((N_HEADS, hd, hd), f32),
                        pltpu.VMEM((SUBLANES, hd), f32),
                        pltpu.VMEM((SUBLANES, LANES), f32)],
        compiler_params=pltpu.CompilerParams(
            dimension_semantics=("parallel", "arbitrary"), vmem_limit_bytes=VMEM_LIMIT),
        name="mlstm",
    )(proj, proj, proj, proj, proj, gates, mhg)


def _out_proj_kernel(x_ref, hm_ref, c_ref, wa_ref, wc_ref, fg_ref, out_ref, *, final_norm):
    y = (x_ref[...]
         + jnp.dot(hm_ref[...], wa_ref[...], preferred_element_type=f32)
         + jnp.dot(c_ref[...], wc_ref[...], preferred_element_type=f32))
    if final_norm:
        y = y * lax.rsqrt(jnp.mean(y * y, axis=-1, keepdims=True) + EPS) * fg_ref[...]
    out_ref[...] = y


def _out_proj(x2, hm, c, w_a, w_c, fg, *, final_norm, tm=512):
    t, d = x2.shape
    da = hm.shape[1]
    dcv = c.shape[1]
    return pl.pallas_call(
        functools.partial(_out_proj_kernel, final_norm=final_norm),
        grid=(t // tm,),
        in_specs=[
            pl.BlockSpec((tm, d), lambda i: (i, 0)),
            pl.BlockSpec((tm, da), lambda i: (i, 0)),
            pl.BlockSpec((tm, dcv), lambda i: (i, 0)),
            pl.BlockSpec((da, d), lambda i: (0, 0)),
            pl.BlockSpec((dcv, d), lambda i: (0, 0)),
            pl.BlockSpec((1, d), lambda i: (0, 0)),
        ],
        out_specs=pl.BlockSpec((tm, d), lambda i: (i, 0)),
        out_shape=jax.ShapeDtypeStruct((t, d), f32),
        compiler_params=pltpu.CompilerParams(
            dimension_semantics=("parallel",), vmem_limit_bytes=VMEM_LIMIT),
        name="out_proj",
    )(x2, hm, c, w_a, w_c, fg)


def kernel(x, norm_g, w_in, b_gates, mh_norm_g, conv_w, conv_b, conv_ln_g, conv_ln_b, w_out, final_norm_g):
    batch, seq, d_model = x.shape
    depth = norm_g.shape[0]
    d_mlstm = mh_norm_g.shape[1]
    d_conv = conv_b.shape[1]
    n_gate = 2 * N_HEADS
    g0 = 5 * d_mlstm

    h = x.reshape(batch * seq, d_model)
    for l in range(depth):
        w_main = jnp.concatenate([w_in[l, :, :g0], w_in[l, :, g0 + n_gate:]], axis=1).astype(bf16)
        wg_t = w_in[l, :, g0:g0 + n_gate].T.astype(bf16)
        proj, gates, c = _proj_conv(h, norm_g[l][None, :], w_main, wg_t, b_gates[l][:, None],
                                    conv_w[l], conv_b[l][None, :], conv_ln_g[l][None, :],
                                    conv_ln_b[l][None, :], seq=seq, d_conv=d_conv)
        hm = _mlstm(proj, gates, mh_norm_g[l][None, :], batch=batch, seq=seq, d_mlstm=d_mlstm)
        wo = w_out[l].astype(bf16)
        h = _out_proj(h, hm, c, wo[:d_mlstm], wo[d_mlstm:], final_norm_g[None, :],
                      final_norm=(l == depth - 1))
    return h.reshape(batch, seq, d_model)
```

```python
import functools

import jax
import jax.numpy as jnp
from jax import lax
from jax.experimental import pallas as pl
from jax.experimental.pallas import tpu as pltpu

N_HEADS = 4
CONV_WIDTH = 31
EPS = 1e-6
M_INIT = -1e30

MLSTM_CHUNK = 256
LANES = 128
SUBLANES = 8
CONV_HALO = 32
GLU_CHAINS = 3
VMEM_LIMIT = 56 * 1024 * 1024

f32 = jnp.float32
bf16 = jnp.bfloat16

_NT = (((1,), (1,)), ((), ()))
_TN = (((0,), (0,)), ((), ()))


def _zero_bits(x):
    b = pltpu.bitcast(x, jnp.uint32)
    return lax.shift_right_logical(lax.shift_right_logical(b, jnp.uint32(16)), jnp.uint32(16))


def _sigmoid(x):
    return 0.5 * jnp.tanh(0.5 * x) + 0.5


def _silu(x):
    hx = 0.5 * x
    return hx * jnp.tanh(hx) + hx


def _conv_rows(x, w_ref, lanes, rb, dep):
    acc = None
    for r in range(SUBLANES):
        nq = (CONV_WIDTH - 1 - r) // SUBLANES + 1
        lo = 0 if r == 0 else SUBLANES
        p = None
        for q in range(nq):
            j = CONV_WIDTH - 1 - (SUBLANES * q + r)
            start = CONV_HALO - lo - SUBLANES * q
            wj = w_ref[j:j + 1, lanes]
            if dep is not None:
                wj = pltpu.bitcast(pltpu.bitcast(wj, jnp.uint32) | dep, f32)
            term = wj * x[start:start + rb + lo, :]
            p = term if p is None else p + term
        acc = p if r == 0 else acc + p[SUBLANES - r:SUBLANES - r + rb, :]
        dep = _zero_bits(acc[0:1, :])
    return acc, dep


def _proj_conv_kernel(x_ref, g_ref, wm_ref, wc_ref, wg_ref, bg_ref, cw_ref, cb_ref, lng_ref, lnb_ref,
                      proj_ref, gates_ref, c_ref,
                      u_ref, ag_ref, z_ref, buf_ref, acc_ref, *, tiles_per_seq, tn, rb):
    i = pl.program_id(0)
    tm, d = x_ref.shape
    dc = c_ref.shape[1]
    halo = CONV_HALO
    wr = i % 2
    rd = 1 - wr

    @pl.when(i == 0)
    def _():
        ag_ref[...] = jnp.zeros_like(ag_ref)
        z_ref[...] = jnp.zeros_like(z_ref)
        buf_ref[...] = jnp.zeros_like(buf_ref)

    first_of_seq = ((i + tiles_per_seq - 1) % tiles_per_seq) == 0

    @pl.when(first_of_seq)
    def _():
        buf_ref[0:halo, :] = jnp.zeros((halo, dc), f32)

    @pl.when(jnp.logical_not(first_of_seq))
    def _():
        buf_ref[0:halo, :] = buf_ref[tm:tm + halo, :]

    x = x_ref[...]
    u = x * lax.rsqrt(jnp.mean(x * x, axis=-1, keepdims=True) + EPS) * g_ref[...]
    u_ref[...] = u.astype(bf16)
    gates_ref[...] = lax.dot_general(wg_ref[...], u_ref[...], _NT, preferred_element_type=f32) + bg_ref[...]

    dm = wm_ref.shape[1]

    def matmul(col):
        w = wm_ref[:, col:col + tn] if col < dm else wc_ref[:, col - dm:col - dm + tn]
        return jnp.dot(u_ref[...], w, preferred_element_type=f32).astype(bf16)

    def glu_item(l0, n):
        def compute():
            halves = [jnp.full((1, n), 0.5, f32)] * GLU_CHAINS
            ys = []
            for j, r0 in enumerate(range(0, tm, rb)):
                half = halves[j % GLU_CHAINS]
                a = ag_ref[r0:r0 + rb, l0:l0 + n].astype(f32)
                g = ag_ref[r0:r0 + rb, dc + l0:dc + l0 + n].astype(f32)
                y = a * (half * jnp.tanh(half * g) + half)
                halves[j % GLU_CHAINS] = pltpu.bitcast(
                    pltpu.bitcast(half, jnp.uint32) | _zero_bits(y[0:1, :]), f32)
                ys.append(y)
            return ys

        def store(ys):
            for j, y in enumerate(ys):
                buf_ref[halo + j * rb:halo + (j + 1) * rb, l0:l0 + n] = y
        return compute, store

    def conv_item(units):
        def compute():
            ys, dep = [], None
            for t0, l0 in units:
                x = buf_ref[t0:t0 + halo + rb, l0:l0 + LANES]
                y, dep = _conv_rows(x, cw_ref, slice(l0, l0 + LANES), rb, dep)
                ys.append(y + cb_ref[:, l0:l0 + LANES])
            return ys

        def store(ys):
            for y, (t0, l0) in zip(ys, units):
                acc_ref[t0:t0 + rb, l0:l0 + LANES] = y
        return compute, store

    def ln_item(r0, n):
        def compute():
            y = acc_ref[r0:r0 + n, :]
            mu = jnp.mean(y, axis=-1, keepdims=True)
            yc = y - mu
            var = jnp.mean(yc * yc, axis=-1, keepdims=True)
            yn = yc * lax.rsqrt(var + EPS) * lng_ref[...] + lnb_ref[...]
            return (_silu(yn) * _silu(z_ref[rd, r0:r0 + n, :].astype(f32))).astype(c_ref.dtype)

        def store(out):
            c_ref[r0:r0 + n, :] = out
        return compute, store

    def to_proj(col, val):
        proj_ref[:, col:col + tn] = val

    def to_ag(col, val):
        ag_ref[:, col - a0:col - a0 + tn] = val

    def to_z(col, val):
        z_ref[wr, :, col - zc0:col - zc0 + tn] = val

    q0, k0, v0, o0, zm0, a0, g0, zc0 = (j * d for j in range(8))
    n_grp = d // tn
    n_chunks = 8 * n_grp
    mm_order = ([(to_proj, q0 + j * tn) for j in range(2 * n_grp)]
                + [(to_ag, a0 + j * tn) for j in range(2 * n_grp)]
                + [(to_proj, v0 + j * tn) for j in range(2 * n_grp)]
                + [(to_proj, zm0 + j * tn) for j in range(n_grp)]
                + [(to_z, zc0 + j * tn) for j in range(n_grp)])
    n_glu = max(n_grp // 2, 1)
    n_ln = 2 * n_grp
    n_conv = n_chunks - n_glu - n_ln
    assert n_glu <= 2 * n_grp
    units = [(t0, l0) for l0 in range(0, dc, LANES) for t0 in range(0, tm, rb)]
    cuts = [len(units) * j // n_conv for j in range(n_conv + 1)]
    work = ([glu_item(j * (dc // n_glu), dc // n_glu) for j in range(n_glu)]
            + [conv_item(units[cuts[j]:cuts[j + 1]]) for j in range(n_conv)]
            + [ln_item(j * (tm // n_ln), tm // n_ln) for j in range(n_ln)])

    once = jnp.minimum(i, 0) + 1

    for (mm_store, col), (compute, store) in zip(mm_order, work):
        def region(_, carry, mm_store=mm_store, col=col, compute=compute, store=store):
            out = compute()
            res = matmul(col)
            mm_store(col, res)
            store(out)
            return carry
        lax.fori_loop(0, once, region, 0)


def _proj_conv(x2, g, w_mlstm, w_cv, wg_t, bg, conv_w, conv_b, ln_g, ln_b, *, seq, d_conv,
               tm=512, tn=512, rb=64):
    t, d = x2.shape
    dm = w_mlstm.shape[1]
    assert (d == d_conv and dm == 5 * d and w_cv.shape[1] == 3 * d_conv and d % tn == 0
            and seq % tm == 0 and tm % rb == 0)
    n = t // tm
    last = n - 1
    const = lambda i: (0, 0)
    vec = pl.BlockSpec((1, d_conv), const)
    return pl.pallas_call(
        functools.partial(_proj_conv_kernel, tiles_per_seq=seq // tm, tn=tn, rb=rb),
        grid=(n + 1,),
        in_specs=[
            pl.BlockSpec((tm, d), lambda i: (jnp.minimum(i, last), 0)),
            pl.BlockSpec((1, d), const),
            pl.BlockSpec(w_mlstm.shape, const, pipeline_mode=pl.Buffered(1)),
            pl.BlockSpec(w_cv.shape, const, pipeline_mode=pl.Buffered(1)),
            pl.BlockSpec((2 * N_HEADS, d), const),
            pl.BlockSpec((2 * N_HEADS, 1), const),
            pl.BlockSpec((CONV_WIDTH, d_conv), const),
            vec, vec, vec,
        ],
        out_specs=[
            pl.BlockSpec((tm, dm), lambda i: (jnp.minimum(i, last), 0)),
            pl.BlockSpec((2 * N_HEADS, tm), lambda i: (0, jnp.minimum(i, last))),
            pl.BlockSpec((tm, d_conv), lambda i: (jnp.maximum(i - 1, 0), 0)),
        ],
        out_shape=[
            jax.ShapeDtypeStruct((t, dm), bf16),
            jax.ShapeDtypeStruct((2 * N_HEADS, t), f32),
            jax.ShapeDtypeStruct((t, d_conv), bf16),
        ],
        scratch_shapes=[
            pltpu.VMEM((tm, d), bf16),
            pltpu.VMEM((tm, 2 * d_conv), bf16),
            pltpu.VMEM((2, tm, d_conv), bf16),
            pltpu.VMEM((tm + CONV_HALO, d_conv), f32),
            pltpu.VMEM((tm, d_conv), f32),
        ],
        compiler_params=pltpu.CompilerParams(
            dimension_semantics=("arbitrary",), vmem_limit_bytes=VMEM_LIMIT),
        name="proj_conv",
    )(x2, g, w_mlstm, w_cv, wg_t, bg, conv_w, conv_b, ln_g, ln_b)


def _cumsum_lanes(x):
    n = x.shape[-1]
    lane = lax.broadcasted_iota(jnp.int32, x.shape, x.ndim - 1)
    s = 1
    while s < n:
        x = x + jnp.where(lane >= s, pltpu.roll(x, s, x.ndim - 1), 0.0)
        s *= 2
    return x


def _mlstm_out_kernel(q_ref, k_ref, v_ref, o_ref, z_ref, gates_ref, mhg_ref,
                      x_ref, cc_ref, wa_ref, wc_ref, fg_ref, out_ref,
                      c_ref, n_ref, m_ref, hm_ref, y_ref, *, head_dim, chunks_per_seq, final_norm):
    L = q_ref.shape[0]
    scale = head_dim ** -0.5
    step = pl.program_id(0)
    wr = step % 2
    rd = 1 - wr

    @pl.when(step == 0)
    def _():
        hm_ref[...] = jnp.zeros_like(hm_ref)

    @pl.when(step % chunks_per_seq == 0)
    def _():
        c_ref[...] = jnp.zeros_like(c_ref)
        n_ref[...] = jnp.zeros_like(n_ref)
        m_ref[...] = jnp.full_like(m_ref, M_INIT)

    gr = gates_ref[...]
    lf = jnp.minimum(gr, 0.0) - jnp.log1p(jnp.exp(-jnp.abs(gr)))
    bfull = _cumsum_lanes(lf)
    ig = gr[0:N_HEADS]
    b = bfull[N_HEADS:2 * N_HEADS]
    b_last = b[:, L - 1:L]
    m_prev = m_ref[0:N_HEADS, 0:1]
    a = b_last - b + ig
    a_max = jnp.max(a, axis=1, keepdims=True)
    m_new = jnp.maximum(b_last + m_prev, a_max)
    s_old = jnp.exp(b_last + m_prev - m_new)
    w_a = jnp.exp(a - m_new)
    c_row = ig - b

    cols = jnp.transpose(jnp.concatenate([b, w_a], axis=0))

    n_all = jnp.dot(jnp.concatenate([w_a, w_a], axis=0).astype(bf16), k_ref[...],
                    preferred_element_type=f32) * scale

    row_id = lax.broadcasted_iota(jnp.int32, (L, L), 0)
    col_id = lax.broadcasted_iota(jnp.int32, (L, L), 1)
    causal = col_id <= row_id

    d_out = out_ref.shape[1]
    tn = d_out // N_HEADS

    for h in range(N_HEADS):
        sl = slice(h * head_dim, (h + 1) * head_dim)
        q = q_ref[:, sl]
        k = k_ref[:, sl]
        v = v_ref[:, sl]
        b_col = cols[:, h:h + 1]
        w_col = cols[:, N_HEADS + h:N_HEADS + h + 1]
        mp = m_prev[h:h + 1, :]

        s = lax.dot_general(q, k, _NT, preferred_element_type=f32) * scale
        g = jnp.where(causal, b_col + c_row[h:h + 1, :], -jnp.inf)
        li = b_col + mp
        m_t = jnp.maximum(li, jnp.max(g, axis=1, keepdims=True))
        p = jnp.exp(g - m_t) * s
        s_inter = jnp.exp(li - m_t)
        c_prev = c_ref[h]
        n_prev = n_ref[h:h + 1, :]
        num = (jnp.dot(p.astype(bf16), v, preferred_element_type=f32)
               + s_inter * jnp.dot(q, c_prev.astype(bf16), preferred_element_type=f32))
        den = (jnp.sum(p, axis=1, keepdims=True)
               + s_inter * jnp.sum(q.astype(f32) * n_prev, axis=1, keepdims=True))
        hh = num * (1.0 / jnp.maximum(jnp.abs(den), jnp.exp(-m_t)))

        hh = _sigmoid(o_ref[:, sl].astype(f32)) * hh
        hh = hh * lax.rsqrt(jnp.mean(hh * hh, axis=-1, keepdims=True) + EPS)
        hh = hh * mhg_ref[:, sl]
        hm_new = (hh * _silu(z_ref[:, sl].astype(f32))).astype(bf16)

        vw = (v.astype(f32) * w_col).astype(bf16)
        kv = lax.dot_general(k, vw, _TN, preferred_element_type=f32) * scale
        so = s_old[h:h + 1, :]
        c_new = so * c_prev + kv
        n_new = so * n_prev + n_all[h:h + 1, sl]

        ol = slice(h * tn, (h + 1) * tn)
        y_part = (jnp.dot(hm_ref[rd], wa_ref[:, ol], preferred_element_type=f32)
                  + jnp.dot(cc_ref[...], wc_ref[:, ol], preferred_element_type=f32))

        y_ref[:, ol] = y_part
        hm_ref[wr, :, sl] = hm_new
        c_ref[h] = c_new
        n_ref[h:h + 1, :] = n_new

    m_ref[0:N_HEADS, :] = jnp.broadcast_to(m_new, (N_HEADS, m_ref.shape[1]))

    y = x_ref[...] + y_ref[...]
    if final_norm:
        y = y * lax.rsqrt(jnp.mean(y * y, axis=-1, keepdims=True) + EPS) * fg_ref[...]
    out_ref[...] = y


def _mlstm_out(proj, gates, mhg, x2, cc, w_a, w_c, fg, *, seq, d_mlstm, final_norm):
    t, d = x2.shape
    L = MLSTM_CHUNK
    nc = seq // L
    n = t // L
    last = n - 1
    hd = d_mlstm // N_HEADS
    assert d % N_HEADS == 0 and seq % L == 0

    def col(j):
        return pl.BlockSpec((L, d_mlstm), lambda s, j=j: (jnp.minimum(s, last), j))

    prev = lambda s: (jnp.maximum(s - 1, 0), 0)
    const = lambda s: (0, 0)
    return pl.pallas_call(
        functools.partial(_mlstm_out_kernel, head_dim=hd, chunks_per_seq=nc, final_norm=final_norm),
        grid=(n + 1,),
        in_specs=[col(0), col(1), col(2), col(3), col(4),
                  pl.BlockSpec((2 * N_HEADS, L), lambda s: (0, jnp.minimum(s, last))),
                  pl.BlockSpec((1, d_mlstm), const),
                  pl.BlockSpec((L, d), prev),
                  pl.BlockSpec((L, cc.shape[1]), prev),
                  pl.BlockSpec(w_a.shape, const, pipeline_mode=pl.Buffered(1)),
                  pl.BlockSpec(w_c.shape, const, pipeline_mode=pl.Buffered(1)),
                  pl.BlockSpec((1, d), const)],
        out_specs=pl.BlockSpec((L, d), prev),
        out_shape=jax.ShapeDtypeStruct((t, d), f32),
        scratch_shapes=[pltpu.VMEM((N_HEADS, hd, hd), f32),
                        pltpu.VMEM((SUBLANES, hd), f32),
                        pltpu.VMEM((SUBLANES, LANES), f32),
                        pltpu.VMEM((2, L, d_mlstm), bf16),
                        pltpu.VMEM((L, d), f32)],
        compiler_params=pltpu.CompilerParams(
            dimension_semantics=("arbitrary",), vmem_limit_bytes=VMEM_LIMIT),
        name="mlstm_out",
    )(proj, proj, proj, proj, proj, gates, mhg, x2, cc, w_a, w_c, fg)


def kernel(x, norm_g, w_in, b_gates, mh_norm_g, conv_w, conv_b, conv_ln_g, conv_ln_b, w_out, final_norm_g):
    batch, seq, d_model = x.shape
    depth = norm_g.shape[0]
    d_mlstm = mh_norm_g.shape[1]
    d_conv = conv_b.shape[1]
    n_gate = 2 * N_HEADS
    g0 = 5 * d_mlstm

    h = x.reshape(batch * seq, d_model)
    for l in range(depth):
        w_mlstm = w_in[l, :, :g0].astype(bf16)
        w_cv = w_in[l, :, g0 + n_gate:].astype(bf16)
        wg_t = w_in[l, :, g0:g0 + n_gate].T.astype(bf16)
        proj, gates, c = _proj_conv(h, norm_g[l][None, :], w_mlstm, w_cv, wg_t, b_gates[l][:, None],
                                    conv_w[l], conv_b[l][None, :], conv_ln_g[l][None, :],
                                    conv_ln_b[l][None, :], seq=seq, d_conv=d_conv)
        wo = w_out[l].astype(bf16)
        h = _mlstm_out(proj, gates, mh_norm_g[l][None, :], h, c, wo[:d_mlstm], wo[d_mlstm:],
                       final_norm_g[None, :], seq=seq, d_mlstm=d_mlstm, final_norm=(l == depth - 1))
    return h.reshape(batch, seq, d_model)
```

```python
import functools

import jax
import jax.numpy as jnp
from jax import lax
from jax.experimental import pallas as pl
from jax.experimental.pallas import tpu as pltpu

N_HEADS = 4
CONV_WIDTH = 31
EPS = 1e-6
M_INIT = -1e30

MLSTM_CHUNK = 256
LANES = 128
SUBLANES = 8
CONV_HALO = 32
GLU_CHAINS = 3
VMEM_LIMIT = 56 * 1024 * 1024

f32 = jnp.float32
bf16 = jnp.bfloat16

_NT = (((1,), (1,)), ((), ()))
_TN = (((0,), (0,)), ((), ()))


def _zero_bits(x):
    b = pltpu.bitcast(x, jnp.uint32)
    return lax.shift_right_logical(lax.shift_right_logical(b, jnp.uint32(16)), jnp.uint32(16))


def _sigmoid(x):
    return 0.5 * jnp.tanh(0.5 * x) + 0.5


def _silu(x):
    hx = 0.5 * x
    return hx * jnp.tanh(hx) + hx


def _conv_rows(x, w_ref, lanes, rb, dep):
    acc = None
    for r in range(SUBLANES):
        nq = (CONV_WIDTH - 1 - r) // SUBLANES + 1
        lo = 0 if r == 0 else SUBLANES
        p = None
        for q in range(nq):
            j = CONV_WIDTH - 1 - (SUBLANES * q + r)
            start = CONV_HALO - lo - SUBLANES * q
            wj = w_ref[j:j + 1, lanes]
            if dep is not None:
                wj = pltpu.bitcast(pltpu.bitcast(wj, jnp.uint32) | dep, f32)
            term = wj * x[start:start + rb + lo, :]
            p = term if p is None else p + term
        acc = p if r == 0 else acc + p[SUBLANES - r:SUBLANES - r + rb, :]
        dep = _zero_bits(acc[0:1, :])
    return acc, dep


def _proj_conv_kernel(x_ref, g_ref, wm_ref, wc_ref, wg_ref, bg_ref, cw_ref, cb_ref, lng_ref, lnb_ref,
                      proj_ref, gates_ref, c_ref,
                      u_ref, ag_ref, z_ref, buf_ref, acc_ref, *, tiles_per_seq, tn, rb):
    i = pl.program_id(0)
    tm, d = x_ref.shape
    dc = c_ref.shape[1]
    halo = CONV_HALO
    wr = i % 2
    rd = 1 - wr

    @pl.when(i == 0)
    def _():
        ag_ref[...] = jnp.zeros_like(ag_ref)
        z_ref[...] = jnp.zeros_like(z_ref)
        buf_ref[...] = jnp.zeros_like(buf_ref)

    first_of_seq = ((i + tiles_per_seq - 1) % tiles_per_seq) == 0

    @pl.when(first_of_seq)
    def _():
        buf_ref[0:halo, :] = jnp.zeros((halo, dc), f32)

    @pl.when(jnp.logical_not(first_of_seq))
    def _():
        buf_ref[0:halo, :] = buf_ref[tm:tm + halo, :]

    x = x_ref[...]
    u = x * lax.rsqrt(jnp.mean(x * x, axis=-1, keepdims=True) + EPS) * g_ref[...]
    u_ref[...] = u.astype(bf16)
    gates_ref[...] = lax.dot_general(wg_ref[...], u_ref[...], _NT, preferred_element_type=f32) + bg_ref[...]

    dm = wm_ref.shape[1]

    def matmul(col):
        w = wm_ref[:, col:col + tn] if col < dm else wc_ref[:, col - dm:col - dm + tn]
        return jnp.dot(u_ref[...], w, preferred_element_type=f32).astype(bf16)

    def glu_item(l0, n):
        def compute():
            halves = [jnp.full((1, n), 0.5, f32)] * GLU_CHAINS
            ys = []
            for j, r0 in enumerate(range(0, tm, rb)):
                half = halves[j % GLU_CHAINS]
                a = ag_ref[r0:r0 + rb, l0:l0 + n].astype(f32)
                g = ag_ref[r0:r0 + rb, dc + l0:dc + l0 + n].astype(f32)
                y = a * (half * jnp.tanh(half * g) + half)
                halves[j % GLU_CHAINS] = pltpu.bitcast(
                    pltpu.bitcast(half, jnp.uint32) | _zero_bits(y[0:1, :]), f32)
                ys.append(y)
            return ys

        def store(ys):
            for j, y in enumerate(ys):
                buf_ref[halo + j * rb:halo + (j + 1) * rb, l0:l0 + n] = y
        return compute, store

    def conv_item(units):
        def compute():
            ys, dep = [], None
            for t0, l0 in units:
                x = buf_ref[t0:t0 + halo + rb, l0:l0 + LANES]
                y, dep = _conv_rows(x, cw_ref, slice(l0, l0 + LANES), rb, dep)
                ys.append(y + cb_ref[:, l0:l0 + LANES])
            return ys

        def store(ys):
            for y, (t0, l0) in zip(ys, units):
                acc_ref[t0:t0 + rb, l0:l0 + LANES] = y
        return compute, store

    def ln_item(r0, n):
        def compute():
            y = acc_ref[r0:r0 + n, :]
            mu = jnp.mean(y, axis=-1, keepdims=True)
            yc = y - mu
            var = jnp.mean(yc * yc, axis=-1, keepdims=True)
            yn = yc * lax.rsqrt(var + EPS) * lng_ref[...] + lnb_ref[...]
            return (_silu(yn) * _silu(z_ref[rd, r0:r0 + n, :].astype(f32))).astype(c_ref.dtype)

        def store(out):
            c_ref[r0:r0 + n, :] = out
        return compute, store

    def to_proj(col, val):
        proj_ref[:, col:col + tn] = val

    def to_ag(col, val):
        ag_ref[:, col - a0:col - a0 + tn] = val

    def to_z(col, val):
        z_ref[wr, :, col - zc0:col - zc0 + tn] = val

    q0, k0, v0, o0, zm0, a0, g0, zc0 = (j * d for j in range(8))
    n_grp = d // tn
    n_chunks = 8 * n_grp
    mm_order = ([(to_proj, q0 + j * tn) for j in range(2 * n_grp)]
                + [(to_ag, a0 + j * tn) for j in range(2 * n_grp)]
                + [(to_proj, v0 + j * tn) for j in range(2 * n_grp)]
                + [(to_proj, zm0 + j * tn) for j in range(n_grp)]
                + [(to_z, zc0 + j * tn) for j in range(n_grp)])
    n_glu = max(n_grp // 2, 1)
    n_ln = 2 * n_grp
    n_conv = n_chunks - n_glu - n_ln
    assert n_glu <= 2 * n_grp
    units = [(t0, l0) for l0 in range(0, dc, LANES) for t0 in range(0, tm, rb)]
    cuts = [len(units) * j // n_conv for j in range(n_conv + 1)]
    work = ([glu_item(j * (dc // n_glu), dc // n_glu) for j in range(n_glu)]
            + [conv_item(units[cuts[j]:cuts[j + 1]]) for j in range(n_conv)]
            + [ln_item(j * (tm // n_ln), tm // n_ln) for j in range(n_ln)])

    once = jnp.minimum(i, 0) + 1

    for (mm_store, col), (compute, store) in zip(mm_order, work):
        def region(_, carry, mm_store=mm_store, col=col, compute=compute, store=store):
            out = compute()
            res = matmul(col)
            mm_store(col, res)
            store(out)
            return carry
        lax.fori_loop(0, once, region, 0)


def _proj_conv(x2, g, w_mlstm, w_cv, wg_t, bg, conv_w, conv_b, ln_g, ln_b, *, seq, d_conv,
               tm=512, tn=512, rb=64):
    t, d = x2.shape
    dm = w_mlstm.shape[1]
    assert (d == d_conv and dm == 5 * d and w_cv.shape[1] == 3 * d_conv and d % tn == 0
            and seq % tm == 0 and tm % rb == 0)
    n = t // tm
    last = n - 1
    const = lambda i: (0, 0)
    vec = pl.BlockSpec((1, d_conv), const)
    return pl.pallas_call(
        functools.partial(_proj_conv_kernel, tiles_per_seq=seq // tm, tn=tn, rb=rb),
        grid=(n + 1,),
        in_specs=[
            pl.BlockSpec((tm, d), lambda i: (jnp.minimum(i, last), 0)),
            pl.BlockSpec((1, d), const),
            pl.BlockSpec(w_mlstm.shape, const, pipeline_mode=pl.Buffered(1)),
            pl.BlockSpec(w_cv.shape, const, pipeline_mode=pl.Buffered(1)),
            pl.BlockSpec((2 * N_HEADS, d), const),
            pl.BlockSpec((2 * N_HEADS, 1), const),
            pl.BlockSpec((CONV_WIDTH, d_conv), const),
            vec, vec, vec,
        ],
        out_specs=[
            pl.BlockSpec((tm, dm), lambda i: (jnp.minimum(i, last), 0)),
            pl.BlockSpec((2 * N_HEADS, tm), lambda i: (0, jnp.minimum(i, last))),
            pl.BlockSpec((tm, d_conv), lambda i: (jnp.maximum(i - 1, 0), 0)),
        ],
        out_shape=[
            jax.ShapeDtypeStruct((t, dm), bf16),
            jax.ShapeDtypeStruct((2 * N_HEADS, t), f32),
            jax.ShapeDtypeStruct((t, d_conv), bf16),
        ],
        scratch_shapes=[
            pltpu.VMEM((tm, d), bf16),
            pltpu.VMEM((tm, 2 * d_conv), bf16),
            pltpu.VMEM((2, tm, d_conv), bf16),
            pltpu.VMEM((tm + CONV_HALO, d_conv), f32),
            pltpu.VMEM((tm, d_conv), f32),
        ],
        compiler_params=pltpu.CompilerParams(
            dimension_semantics=("arbitrary",), vmem_limit_bytes=VMEM_LIMIT),
        name="proj_conv",
    )(x2, g, w_mlstm, w_cv, wg_t, bg, conv_w, conv_b, ln_g, ln_b)


def _cumsum_lanes(x):
    rows, n = x.shape
    tri = (lax.broadcasted_iota(jnp.int32, (n, n), 0)
           <= lax.broadcasted_iota(jnp.int32, (n, n), 1)).astype(bf16)
    hi = x.astype(bf16).astype(f32)
    mid = (x - hi).astype(bf16).astype(f32)
    lo = x - hi - mid
    parts = jnp.dot(jnp.concatenate([hi, mid, lo], axis=0).astype(bf16), tri, preferred_element_type=f32)
    return parts[0:rows] + parts[rows:2 * rows] + parts[2 * rows:3 * rows]


def _mlstm_out_kernel(q_ref, k_ref, v_ref, o_ref, z_ref, gates_ref, mhg_ref,
                      x_ref, cc_ref, wa_ref, wc_ref, fg_ref, out_ref,
                      c_ref, n_ref, m_ref, hm_ref, *, head_dim, chunks_per_seq, final_norm):
    L = q_ref.shape[0]
    scale = head_dim ** -0.5
    step = pl.program_id(0)
    wr = step % 2
    rd = 1 - wr

    @pl.when(step == 0)
    def _():
        hm_ref[...] = jnp.zeros_like(hm_ref)

    @pl.when(step % chunks_per_seq == 0)
    def _():
        c_ref[...] = jnp.zeros_like(c_ref)
        n_ref[...] = jnp.zeros_like(n_ref)
        m_ref[...] = jnp.full_like(m_ref, M_INIT)

    gr = gates_ref[...]
    lf = jnp.minimum(gr, 0.0) - jnp.log1p(jnp.exp(-jnp.abs(gr)))
    bfull = _cumsum_lanes(lf)
    ig = gr[0:N_HEADS]
    b = bfull[N_HEADS:2 * N_HEADS]
    b_last = b[:, L - 1:L]
    m_prev = m_ref[0:N_HEADS, 0:1]
    a = b_last - b + ig
    a_max = jnp.max(a, axis=1, keepdims=True)
    m_new = jnp.maximum(b_last + m_prev, a_max)
    s_old = jnp.exp(b_last + m_prev - m_new)
    w_a = jnp.exp(a - m_new)
    c_row = ig - b

    cols = jnp.transpose(jnp.concatenate([b, w_a], axis=0))

    row_id = lax.broadcasted_iota(jnp.int32, (L, L), 0)
    col_id = lax.broadcasted_iota(jnp.int32, (L, L), 1)
    causal = col_id <= row_id

    d_out = out_ref.shape[1]
    tn = d_out // N_HEADS

    heads = range(N_HEADS)
    hs = [slice(h * head_dim, (h + 1) * head_dim) for h in heads]
    ys = []

    def out_proj_part(j):
        ol = slice(j * tn, (j + 1) * tn)
        ys.append(jnp.dot(hm_ref[rd], wa_ref[:, ol], preferred_element_type=f32)
                  + jnp.dot(cc_ref[...], wc_ref[:, ol], preferred_element_type=f32))

    s_ = [lax.dot_general(q_ref[:, hs[h]], k_ref[:, hs[h]], _NT, preferred_element_type=f32) * scale
          for h in heads]
    c_prev = [c_ref[h] for h in heads]
    qc = [jnp.dot(q_ref[:, hs[h]], c_prev[h].astype(bf16), preferred_element_type=f32) for h in heads]

    p_, m_t_, s_inter_ = [], [], []
    for h in heads:
        b_col = cols[:, h:h + 1]
        g = jnp.where(causal, b_col + c_row[h:h + 1, :], -jnp.inf)
        li = b_col + m_prev[h:h + 1, :]
        m_t = jnp.maximum(li, jnp.max(g, axis=1, keepdims=True))
        p_.append(jnp.exp(g - m_t) * s_[h])
        m_t_.append(m_t)
        s_inter_.append(jnp.exp(li - m_t))

    pv = [jnp.dot(p_[h].astype(bf16), v_ref[:, hs[h]], preferred_element_type=f32) for h in heads]
    out_proj_part(0)
    out_proj_part(1)

    hm_new, vw = [], []
    for h in heads:
        n_prev = n_ref[h:h + 1, :]
        num = pv[h] + s_inter_[h] * qc[h]
        den = (jnp.sum(p_[h], axis=1, keepdims=True)
               + s_inter_[h] * jnp.sum(q_ref[:, hs[h]].astype(f32) * n_prev, axis=1, keepdims=True))
        hh = num * (1.0 / jnp.maximum(jnp.abs(den), jnp.exp(-m_t_[h])))
        hh = _sigmoid(o_ref[:, hs[h]].astype(f32)) * hh
        hh = hh * lax.rsqrt(jnp.mean(hh * hh, axis=-1, keepdims=True) + EPS)
        hh = hh * mhg_ref[:, hs[h]]
        hm_new.append((hh * _silu(z_ref[:, hs[h]].astype(f32))).astype(bf16))
        w_col = cols[:, N_HEADS + h:N_HEADS + h + 1]
        vw.append((v_ref[:, hs[h]].astype(f32) * w_col).astype(bf16))

    out_proj_part(2)
    kv = [lax.dot_general(k_ref[:, hs[h]], vw[h], _TN, preferred_element_type=f32) * scale
          for h in heads]
    out_proj_part(3)
    n_all = jnp.dot(jnp.concatenate([w_a, w_a], axis=0).astype(bf16), k_ref[...],
                    preferred_element_type=f32) * scale
    c_new =[s_old[h:h + 1, :] * c_prev[h] + kv[h] for h in heads]
    n_new = [s_old[h:h + 1, :] * n_ref[h:h + 1, :] + n_all[h:h + 1, hs[h]] for h in heads]

    y = x_ref[...] + jnp.concatenate(ys, axis=1)
    if final_norm:
        y = y * lax.rsqrt(jnp.mean(y * y, axis=-1, keepdims=True) + EPS) * fg_ref[...]
    out_ref[...] = y
    for h in heads:
        hm_ref[wr, :, hs[h]] = hm_new[h]
        c_ref[h] = c_new[h]
        n_ref[h:h + 1, :] = n_new[h]
    m_ref[0:N_HEADS, :] = jnp.broadcast_to(m_new, (N_HEADS, m_ref.shape[1]))


def _mlstm_out(proj, gates, mhg, x2, cc, w_a, w_c, fg, *, seq, d_mlstm, final_norm):
    t, d = x2.shape
    L = MLSTM_CHUNK
    nc = seq // L
    n = t // L
    last = n - 1
    hd = d_mlstm // N_HEADS
    assert d % N_HEADS == 0 and seq % L == 0

    def col(j):
        return pl.BlockSpec((L, d_mlstm), lambda s, j=j: (jnp.minimum(s, last), j))

    prev = lambda s: (jnp.maximum(s - 1, 0), 0)
    const = lambda s: (0, 0)
    return pl.pallas_call(
        functools.partial(_mlstm_out_kernel, head_dim=hd, chunks_per_seq=nc, final_norm=final_norm),
        grid=(n + 1,),
        in_specs=[col(0), col(1), col(2), col(3), col(4),
                  pl.BlockSpec((2 * N_HEADS, L), lambda s: (0, jnp.minimum(s, last))),
                  pl.BlockSpec((1, d_mlstm), const),
                  pl.BlockSpec((L, d), prev),
                  pl.BlockSpec((L, cc.shape[1]), prev),
                  pl.BlockSpec(w_a.shape, const, pipeline_mode=pl.Buffered(1)),
                  pl.BlockSpec(w_c.shape, const, pipeline_mode=pl.Buffered(1)),
                  pl.BlockSpec((1, d), const)],
        out_specs=pl.BlockSpec((L, d), prev),
        out_shape=jax.ShapeDtypeStruct((t, d), f32),
        scratch_shapes=[pltpu.VMEM((N_HEADS, hd, hd), f32),
                        pltpu.VMEM((SUBLANES, hd), f32),
                        pltpu.VMEM((SUBLANES, LANES), f32),
                        pltpu.VMEM((2, L, d_mlstm), bf16)],
        compiler_params=pltpu.CompilerParams(
            dimension_semantics=("arbitrary",), vmem_limit_bytes=VMEM_LIMIT),
        name="mlstm_out",
    )(proj, proj, proj, proj, proj, gates, mhg, x2, cc, w_a, w_c, fg)


def kernel(x, norm_g, w_in, b_gates, mh_norm_g, conv_w, conv_b, conv_ln_g, conv_ln_b, w_out, final_norm_g):
    batch, seq, d_model = x.shape
    depth = norm_g.shape[0]
    d_mlstm = mh_norm_g.shape[1]
    d_conv = conv_b.shape[1]
    n_gate = 2 * N_HEADS
    g0 = 5 * d_mlstm

    h = x.reshape(batch * seq, d_model)
    for l in range(depth):
        w_bf = w_in[l].astype(bf16)
        w_mlstm = w_bf[:, :g0]
        w_cv = w_bf[:, g0 + n_gate:]
        wg_t = w_bf[:, g0:g0 + n_gate].T
        proj, gates, c = _proj_conv(h, norm_g[l][None, :], w_mlstm, w_cv, wg_t, b_gates[l][:, None],
                                    conv_w[l], conv_b[l][None, :], conv_ln_g[l][None, :],
                                    conv_ln_b[l][None, :], seq=seq, d_conv=d_conv)
        wo = w_out[l].astype(bf16)
        h = _mlstm_out(proj, gates, mh_norm_g[l][None, :], h, c, wo[:d_mlstm], wo[d_mlstm:],
                       final_norm_g[None, :], seq=seq, d_mlstm=d_mlstm, final_norm=(l == depth - 1))
    return h.reshape(batch, seq, d_model)
```

```python
import functools

import jax
import jax.numpy as jnp
from jax import lax
from jax.experimental import pallas as pl
from jax.experimental.pallas import tpu as pltpu

N_HEADS = 4
CONV_WIDTH = 31
EPS = 1e-6
M_INIT = -1e30

MLSTM_CHUNK = 256
LANES = 128
SUBLANES = 8
CONV_HALO = 32
GLU_CHAINS = 3
VMEM_LIMIT = 56 * 1024 * 1024

f32 = jnp.float32
bf16 = jnp.bfloat16

_NT = (((1,), (1,)), ((), ()))
_TN = (((0,), (0,)), ((), ()))


def _zero_bits(x):
    b = pltpu.bitcast(x, jnp.uint32)
    return lax.shift_right_logical(lax.shift_right_logical(b, jnp.uint32(16)), jnp.uint32(16))


def _sigmoid(x):
    return 0.5 * jnp.tanh(0.5 * x) + 0.5


def _silu(x):
    hx = 0.5 * x
    return hx * jnp.tanh(hx) + hx


def _conv_rows(x, w_ref, lanes, rb, dep):
    acc = None
    for r in range(SUBLANES):
        nq = (CONV_WIDTH - 1 - r) // SUBLANES + 1
        lo = 0 if r == 0 else SUBLANES
        p = None
        for q in range(nq):
            j = CONV_WIDTH - 1 - (SUBLANES * q + r)
            start = CONV_HALO - lo - SUBLANES * q
            wj = w_ref[j:j + 1, lanes]
            if dep is not None:
                wj = pltpu.bitcast(pltpu.bitcast(wj, jnp.uint32) | dep, f32)
            term = wj * x[start:start + rb + lo, :]
            p = term if p is None else p + term
        acc = p if r == 0 else acc + p[SUBLANES - r:SUBLANES - r + rb, :]
        dep = _zero_bits(acc[0:1, :])
    return acc, dep


def _proj_conv_kernel(x_ref, g_ref, wm_ref, wc_ref, wg_ref, bg_ref, cw_ref, cb_ref, lng_ref, lnb_ref,
                      proj_ref, gates_ref, c_ref,
                      u_ref, ag_ref, z_ref, buf_ref, acc_ref, *, tiles_per_seq, tn, rb):
    i = pl.program_id(0)
    tm, d = x_ref.shape
    dc = c_ref.shape[1]
    halo = CONV_HALO
    wr = i % 2
    rd = 1 - wr

    @pl.when(i == 0)
    def _():
        ag_ref[...] = jnp.zeros_like(ag_ref)
        z_ref[...] = jnp.zeros_like(z_ref)
        buf_ref[...] = jnp.zeros_like(buf_ref)

    first_of_seq = ((i + tiles_per_seq - 1) % tiles_per_seq) == 0

    @pl.when(first_of_seq)
    def _():
        buf_ref[0:halo, :] = jnp.zeros((halo, dc), f32)

    @pl.when(jnp.logical_not(first_of_seq))
    def _():
        buf_ref[0:halo, :] = buf_ref[tm:tm + halo, :]

    x = x_ref[...]
    u = x * lax.rsqrt(jnp.mean(x * x, axis=-1, keepdims=True) + EPS) * g_ref[...]
    u_ref[...] = u.astype(bf16)
    gates_ref[...] = lax.dot_general(wg_ref[...], u_ref[...], _NT, preferred_element_type=f32) + bg_ref[...]

    dm = wm_ref.shape[1]

    def matmul(col):
        w = wm_ref[:, col:col + tn] if col < dm else wc_ref[:, col - dm:col - dm + tn]
        return jnp.dot(u_ref[...], w, preferred_element_type=f32).astype(bf16)

    def glu_item(l0, n):
        def compute():
            halves = [jnp.full((1, n), 0.5, f32)] * GLU_CHAINS
            ys = []
            for j, r0 in enumerate(range(0, tm, rb)):
                half = halves[j % GLU_CHAINS]
                a = ag_ref[r0:r0 + rb, l0:l0 + n].astype(f32)
                g = ag_ref[r0:r0 + rb, dc + l0:dc + l0 + n].astype(f32)
                y = a * (half * jnp.tanh(half * g) + half)
                halves[j % GLU_CHAINS] = pltpu.bitcast(
                    pltpu.bitcast(half, jnp.uint32) | _zero_bits(y[0:1, :]), f32)
                ys.append(y)
            return ys

        def store(ys):
            for j, y in enumerate(ys):
                buf_ref[halo + j * rb:halo + (j + 1) * rb, l0:l0 + n] = y
        return compute, store

    def conv_item(units):
        def compute():
            ys, dep = [], None
            for t0, l0 in units:
                x = buf_ref[t0:t0 + halo + rb, l0:l0 + LANES]
                y, dep = _conv_rows(x, cw_ref, slice(l0, l0 + LANES), rb, dep)
                ys.append(y + cb_ref[:, l0:l0 + LANES])
            return ys

        def store(ys):
            for y, (t0, l0) in zip(ys, units):
                acc_ref[t0:t0 + rb, l0:l0 + LANES] = y
        return compute, store

    def ln_item(r0, n):
        def compute():
            y = acc_ref[r0:r0 + n, :]
            mu = jnp.mean(y, axis=-1, keepdims=True)
            yc = y - mu
            var = jnp.mean(yc * yc, axis=-1, keepdims=True)
            yn = yc * lax.rsqrt(var + EPS) * lng_ref[...] + lnb_ref[...]
            return (_silu(yn) * _silu(z_ref[rd, r0:r0 + n, :].astype(f32))).astype(c_ref.dtype)

        def store(out):
            c_ref[r0:r0 + n, :] = out
        return compute, store

    def to_proj(col, val):
        proj_ref[:, col:col + tn] = val

    def to_ag(col, val):
        ag_ref[:, col - a0:col - a0 + tn] = val

    def to_z(col, val):
        z_ref[wr, :, col - zc0:col - zc0 + tn] = val

    q0, k0, v0, o0, zm0, a0, g0, zc0 = (j * d for j in range(8))
    n_grp = d // tn
    n_chunks = 8 * n_grp
    mm_order = ([(to_proj, q0 + j * tn) for j in range(2 * n_grp)]
                + [(to_ag, a0 + j * tn) for j in range(2 * n_grp)]
                + [(to_proj, v0 + j * tn) for j in range(2 * n_grp)]
                + [(to_proj, zm0 + j * tn) for j in range(n_grp)]
                + [(to_z, zc0 + j * tn) for j in range(n_grp)])
    n_glu = max(n_grp // 2, 1)
    n_ln = 2 * n_grp
    n_conv = n_chunks - n_glu - n_ln
    assert n_glu <= 2 * n_grp
    units = [(t0, l0) for l0 in range(0, dc, LANES) for t0 in range(0, tm, rb)]
    cuts = [len(units) * j // n_conv for j in range(n_conv + 1)]
    work = ([glu_item(j * (dc // n_glu), dc // n_glu) for j in range(n_glu)]
            + [conv_item(units[cuts[j]:cuts[j + 1]]) for j in range(n_conv)]
            + [ln_item(j * (tm // n_ln), tm // n_ln) for j in range(n_ln)])

    once = jnp.minimum(i, 0) + 1

    for (mm_store, col), (compute, store) in zip(mm_order, work):
        def region(_, carry, mm_store=mm_store, col=col, compute=compute, store=store):
            out = compute()
            res = matmul(col)
            mm_store(col, res)
            store(out)
            return carry
        lax.fori_loop(0, once, region, 0)


def _proj_conv(x2, g, w_mlstm, w_cv, wg_t, bg, conv_w, conv_b, ln_g, ln_b, *, seq, d_conv,
               tm=512, tn=512, rb=64):
    t, d = x2.shape
    dm = w_mlstm.shape[1]
    assert (d == d_conv and dm == 5 * d and w_cv.shape[1] == 3 * d_conv and d % tn == 0
            and seq % tm == 0 and tm % rb == 0)
    n = t // tm
    last = n - 1
    const = lambda i: (0, 0)
    vec = pl.BlockSpec((1, d_conv), const)
    return pl.pallas_call(
        functools.partial(_proj_conv_kernel, tiles_per_seq=seq // tm, tn=tn, rb=rb),
        grid=(n + 1,),
        in_specs=[
            pl.BlockSpec((tm, d), lambda i: (jnp.minimum(i, last), 0)),
            pl.BlockSpec((1, d), const),
            pl.BlockSpec(w_mlstm.shape, const, pipeline_mode=pl.Buffered(1)),
            pl.BlockSpec(w_cv.shape, const, pipeline_mode=pl.Buffered(1)),
            pl.BlockSpec((2 * N_HEADS, d), const),
            pl.BlockSpec((2 * N_HEADS, 1), const),
            pl.BlockSpec((CONV_WIDTH, d_conv), const),
            vec, vec, vec,
        ],
        out_specs=[
            pl.BlockSpec((tm, dm), lambda i: (jnp.minimum(i, last), 0)),
            pl.BlockSpec((2 * N_HEADS, tm), lambda i: (0, jnp.minimum(i, last))),
            pl.BlockSpec((tm, d_conv), lambda i: (jnp.maximum(i - 1, 0), 0)),
        ],
        out_shape=[
            jax.ShapeDtypeStruct((t, dm), bf16),
            jax.ShapeDtypeStruct((2 * N_HEADS, t), f32),
            jax.ShapeDtypeStruct((t, d_conv), bf16),
        ],
        scratch_shapes=[
            pltpu.VMEM((tm, d), bf16),
            pltpu.VMEM((tm, 2 * d_conv), bf16),
            pltpu.VMEM((2, tm, d_conv), bf16),
            pltpu.VMEM((tm + CONV_HALO, d_conv), f32),
            pltpu.VMEM((tm, d_conv), f32),
        ],
        compiler_params=pltpu.CompilerParams(
            dimension_semantics=("arbitrary",), vmem_limit_bytes=VMEM_LIMIT),
        name="proj_conv",
    )(x2, g, w_mlstm, w_cv, wg_t, bg, conv_w, conv_b, ln_g, ln_b)


def _cumsum_lanes(x):
    rows, n = x.shape
    tri = (lax.broadcasted_iota(jnp.int32, (n, n), 0)
           <= lax.broadcasted_iota(jnp.int32, (n, n), 1)).astype(bf16)
    hi = x.astype(bf16).astype(f32)
    mid = (x - hi).astype(bf16).astype(f32)
    lo = x - hi - mid
    parts = jnp.dot(jnp.concatenate([hi, mid, lo], axis=0).astype(bf16), tri, preferred_element_type=f32)
    return parts[0:rows] + parts[rows:2 * rows] + parts[2 * rows:3 * rows]


def _mlstm_out_kernel(q_ref, k_ref, v_ref, o_ref, z_ref, gates_ref, mhg_ref,
                      x_ref, cc_ref, wa_ref, wc_ref, fg_ref, out_ref,
                      c_ref, n_ref, m_ref, hm_ref, *, head_dim, chunks_per_seq, final_norm):
    L = q_ref.shape[0]
    scale = head_dim ** -0.5
    step = pl.program_id(0)
    wr = step % 2
    rd = 1 - wr

    @pl.when(step == 0)
    def _():
        hm_ref[...] = jnp.zeros_like(hm_ref)

    @pl.when(step % chunks_per_seq == 0)
    def _():
        c_ref[...] = jnp.zeros_like(c_ref)
        n_ref[...] = jnp.zeros_like(n_ref)
        m_ref[...] = jnp.full_like(m_ref, M_INIT)

    gr = gates_ref[...]
    lf = jnp.minimum(gr, 0.0) - jnp.log1p(jnp.exp(-jnp.abs(gr)))
    bfull = _cumsum_lanes(lf)
    ig = gr[0:N_HEADS]
    b = bfull[N_HEADS:2 * N_HEADS]
    b_last = b[:, L - 1:L]
    m_prev = m_ref[0:N_HEADS, 0:1]
    a = b_last - b + ig
    a_max = jnp.max(a, axis=1, keepdims=True)
    m_new = jnp.maximum(b_last + m_prev, a_max)
    s_old = jnp.exp(b_last + m_prev - m_new)
    w_a = jnp.exp(a - m_new)
    c_row = ig - b

    cols = jnp.transpose(jnp.concatenate([b, w_a], axis=0))

    row_id = lax.broadcasted_iota(jnp.int32, (L, L), 0)
    col_id = lax.broadcasted_iota(jnp.int32, (L, L), 1)
    causal = col_id <= row_id

    d_out = out_ref.shape[1]
    tn = d_out // N_HEADS

    heads = range(N_HEADS)
    hs = [slice(h * head_dim, (h + 1) * head_dim) for h in heads]
    ys = []

    def out_proj_part(j):
        ol = slice(j * tn, (j + 1) * tn)
        ys.append(jnp.dot(hm_ref[rd], wa_ref[:, ol], preferred_element_type=f32)
                  + jnp.dot(cc_ref[...], wc_ref[:, ol], preferred_element_type=f32))

    s_ = [lax.dot_general(q_ref[:, hs[h]], k_ref[:, hs[h]], _NT, preferred_element_type=f32) * scale
          for h in heads]
    c_prev = [c_ref[h] for h in heads]
    qc = [jnp.dot(q_ref[:, hs[h]], c_prev[h].astype(bf16), preferred_element_type=f32) for h in heads]

    p_, m_t_, s_inter_ = [], [], []
    for h in heads:
        b_col = cols[:, h:h + 1]
        g = jnp.where(causal, b_col + c_row[h:h + 1, :], -jnp.inf)
        li = b_col + m_prev[h:h + 1, :]
        m_t = jnp.maximum(li, jnp.max(g, axis=1, keepdims=True))
        p_.append(jnp.exp(g - m_t) * s_[h])
        m_t_.append(m_t)
        s_inter_.append(jnp.exp(li - m_t))

    pv = [jnp.dot(p_[h].astype(bf16), v_ref[:, hs[h]], preferred_element_type=f32) for h in heads]
    out_proj_part(0)
    out_proj_part(1)

    hm_new, vw = [], []
    for h in heads:
        n_prev = n_ref[h:h + 1, :]
        num = pv[h] + s_inter_[h] * qc[h]
        den = (jnp.sum(p_[h], axis=1, keepdims=True)
               + s_inter_[h] * jnp.sum(q_ref[:, hs[h]].astype(f32) * n_prev, axis=1, keepdims=True))
        hh = num * (1.0 / jnp.maximum(jnp.abs(den), jnp.exp(-m_t_[h])))
        hh = _sigmoid(o_ref[:, hs[h]].astype(f32)) * hh
        hh = hh * lax.rsqrt(jnp.mean(hh * hh, axis=-1, keepdims=True) + EPS)
        hh = hh * mhg_ref[:, hs[h]]
        hm_new.append((hh * _silu(z_ref[:, hs[h]].astype(f32))).astype(bf16))
        w_col = cols[:, N_HEADS + h:N_HEADS + h + 1]
        vw.append((v_ref[:, hs[h]].astype(f32) * w_col).astype(bf16))

    out_proj_part(2)
    kv = [lax.dot_general(k_ref[:, hs[h]], vw[h], _TN, preferred_element_type=f32) * scale
          for h in heads]
    out_proj_part(3)
    n_all = jnp.dot(jnp.concatenate([w_a, w_a], axis=0).astype(bf16), k_ref[...],
                    preferred_element_type=f32) * scale
    c_new =[s_old[h:h + 1, :] * c_prev[h] + kv[h] for h in heads]
    n_new = [s_old[h:h + 1, :] * n_ref[h:h + 1, :] + n_all[h:h + 1, hs[h]] for h in heads]

    y = x_ref[...] + jnp.concatenate(ys, axis=1)
    if final_norm:
        y = y * lax.rsqrt(jnp.mean(y * y, axis=-1, keepdims=True) + EPS) * fg_ref[...]
    out_ref[...] = y
    for h in heads:
        hm_ref[wr, :, hs[h]] = hm_new[h]
        c_ref[h] = c_new[h]
        n_ref[h:h + 1, :] = n_new[h]
    m_ref[0:N_HEADS, :] = jnp.broadcast_to(m_new, (N_HEADS, m_ref.shape[1]))


def _mlstm_out(proj, gates, mhg, x2, cc, w_a, w_c, fg, *, seq, d_mlstm, final_norm):
    t, d = x2.shape
    L = MLSTM_CHUNK
    nc = seq // L
    n = t // L
    last = n - 1
    hd = d_mlstm // N_HEADS
    assert d % N_HEADS == 0 and seq % L == 0

    def col(j):
        return pl.BlockSpec((L, d_mlstm), lambda s, j=j: (jnp.minimum(s, last), j))

    prev = lambda s: (jnp.maximum(s - 1, 0), 0)
    const = lambda s: (0, 0)
    return pl.pallas_call(
        functools.partial(_mlstm_out_kernel, head_dim=hd, chunks_per_seq=nc, final_norm=final_norm),
        grid=(n + 1,),
        in_specs=[col(0), col(1), col(2), col(3), col(4),
                  pl.BlockSpec((2 * N_HEADS, L), lambda s: (0, jnp.minimum(s, last))),
                  pl.BlockSpec((1, d_mlstm), const),
                  pl.BlockSpec((L, d), prev),
                  pl.BlockSpec((L, cc.shape[1]), prev),
                  pl.BlockSpec(w_a.shape, const, pipeline_mode=pl.Buffered(1)),
                  pl.BlockSpec(w_c.shape, const, pipeline_mode=pl.Buffered(1)),
                  pl.BlockSpec((1, d), const)],
        out_specs=pl.BlockSpec((L, d), prev),
        out_shape=jax.ShapeDtypeStruct((t, d), f32),
        scratch_shapes=[pltpu.VMEM((N_HEADS, hd, hd), f32),
                        pltpu.VMEM((SUBLANES, hd), f32),
                        pltpu.VMEM((SUBLANES, LANES), f32),
                        pltpu.VMEM((2, L, d_mlstm), bf16)],
        compiler_params=pltpu.CompilerParams(
            dimension_semantics=("arbitrary",), vmem_limit_bytes=VMEM_LIMIT),
        name="mlstm_out",
    )(proj, proj, proj, proj, proj, gates, mhg, x2, cc, w_a, w_c, fg)


def _split_w_in_kernel(w_ref, wm_ref, wc_ref, wg_ref):
    dm = wm_ref.shape[1]
    ng = wg_ref.shape[1]
    wm_ref[...] = w_ref[:, 0:dm].astype(bf16)
    wg_ref[...] = w_ref[:, dm:dm + ng]
    wc_ref[...] = w_ref[:, dm + ng:].astype(bf16)


def _split_w_in(w, dm, n_gate, *, tk=256):
    k, n_all = w.shape
    dcv = n_all - dm - n_gate
    return pl.pallas_call(
        _split_w_in_kernel,
        grid=(k // tk,),
        in_specs=[pl.BlockSpec((tk, n_all), lambda i: (i, 0))],
        out_specs=[pl.BlockSpec((tk, dm), lambda i: (i, 0)),
                   pl.BlockSpec((tk, dcv), lambda i: (i, 0)),
                   pl.BlockSpec((tk, n_gate), lambda i: (i, 0))],
        out_shape=[jax.ShapeDtypeStruct((k, dm), bf16),
                   jax.ShapeDtypeStruct((k, dcv), bf16),
                   jax.ShapeDtypeStruct((k, n_gate), w.dtype)],
        compiler_params=pltpu.CompilerParams(
            dimension_semantics=("parallel",), vmem_limit_bytes=VMEM_LIMIT),
        name="split_w_in",
    )(w)


def kernel(x, norm_g, w_in, b_gates, mh_norm_g, conv_w, conv_b, conv_ln_g, conv_ln_b, w_out, final_norm_g):
    batch, seq, d_model = x.shape
    depth = norm_g.shape[0]
    d_mlstm = mh_norm_g.shape[1]
    d_conv = conv_b.shape[1]
    n_gate = 2 * N_HEADS
    g0 = 5 * d_mlstm

    h = x.reshape(batch * seq, d_model)
    for l in range(depth):
        w_mlstm, w_cv, wg = _split_w_in(w_in[l], g0, n_gate)
        wg_t = wg.T.astype(bf16)
        proj, gates, c = _proj_conv(h, norm_g[l][None, :], w_mlstm, w_cv, wg_t, b_gates[l][:, None],
                                    conv_w[l], conv_b[l][None, :], conv_ln_g[l][None, :],
                                    conv_ln_b[l][None, :], seq=seq, d_conv=d_conv)
        wo = w_out[l].astype(bf16)
        h = _mlstm_out(proj, gates, mh_norm_g[l][None, :], h, c, wo[:d_mlstm], wo[d_mlstm:],
                       final_norm_g[None, :], seq=seq, d_mlstm=d_mlstm, final_norm=(l == depth - 1))
    return h.reshape(batch, seq, d_model)
```

```python
import functools

import jax
import jax.numpy as jnp
from jax import lax
from jax.experimental import pallas as pl
from jax.experimental.pallas import tpu as pltpu

N_HEADS = 4
CONV_WIDTH = 31
EPS = 1e-6
M_INIT = -1e30

MLSTM_CHUNK = 256
LANES = 128
SUBLANES = 8
CONV_HALO = 32
GLU_CHAINS = 3
VMEM_LIMIT = 56 * 1024 * 1024

f32 = jnp.float32
bf16 = jnp.bfloat16

_NT = (((1,), (1,)), ((), ()))
_TN = (((0,), (0,)), ((), ()))


def _zero_bits(x):
    b = pltpu.bitcast(x, jnp.uint32)
    return lax.shift_right_logical(lax.shift_right_logical(b, jnp.uint32(16)), jnp.uint32(16))


def _sigmoid(x):
    return 0.5 * jnp.tanh(0.5 * x) + 0.5


def _silu(x):
    hx = 0.5 * x
    return hx * jnp.tanh(hx) + hx


def _conv_rows(x, w_ref, lanes, rb, dep):
    acc = None
    for r in range(SUBLANES):
        nq = (CONV_WIDTH - 1 - r) // SUBLANES + 1
        lo = 0 if r == 0 else SUBLANES
        p = None
        for q in range(nq):
            j = CONV_WIDTH - 1 - (SUBLANES * q + r)
            start = CONV_HALO - lo - SUBLANES * q
            wj = w_ref[j:j + 1, lanes]
            if dep is not None:
                wj = pltpu.bitcast(pltpu.bitcast(wj, jnp.uint32) | dep, f32)
            term = wj * x[start:start + rb + lo, :]
            p = term if p is None else p + term
        acc = p if r == 0 else acc + p[SUBLANES - r:SUBLANES - r + rb, :]
        dep = _zero_bits(acc[0:1, :])
    return acc, dep


def _proj_conv_kernel(x_ref, g_ref, wm_ref, wc_ref, wg_ref, bg_ref, cw_ref, cb_ref, lng_ref, lnb_ref,
                      proj_ref, gates_ref, c_ref,
                      u_ref, ag_ref, z_ref, buf_ref, acc_ref, *, tiles_per_seq, tn, rb):
    i = pl.program_id(0)
    tm, d = x_ref.shape
    dc = c_ref.shape[1]
    halo = CONV_HALO
    wr = i % 2
    rd = 1 - wr

    @pl.when(i == 0)
    def _():
        ag_ref[...] = jnp.zeros_like(ag_ref)
        z_ref[...] = jnp.zeros_like(z_ref)
        buf_ref[...] = jnp.zeros_like(buf_ref)

    first_of_seq = ((i + tiles_per_seq - 1) % tiles_per_seq) == 0

    @pl.when(first_of_seq)
    def _():
        buf_ref[0:halo, :] = jnp.zeros((halo, dc), f32)

    @pl.when(jnp.logical_not(first_of_seq))
    def _():
        buf_ref[0:halo, :] = buf_ref[tm:tm + halo, :]

    x = x_ref[...]
    u = x * lax.rsqrt(jnp.mean(x * x, axis=-1, keepdims=True) + EPS) * g_ref[...]
    u_ref[...] = u.astype(bf16)
    gates_ref[...] = lax.dot_general(wg_ref[...], u_ref[...], _NT, preferred_element_type=f32) + bg_ref[...]

    dm = wm_ref.shape[1]

    def matmul(col):
        w = wm_ref[:, col:col + tn] if col < dm else wc_ref[:, col - dm:col - dm + tn]
        return jnp.dot(u_ref[...], w, preferred_element_type=f32).astype(bf16)

    def glu_item(l0, n):
        def compute():
            halves = [jnp.full((1, n), 0.5, f32)] * GLU_CHAINS
            ys = []
            for j, r0 in enumerate(range(0, tm, rb)):
                half = halves[j % GLU_CHAINS]
                a = ag_ref[r0:r0 + rb, l0:l0 + n].astype(f32)
                g = ag_ref[r0:r0 + rb, dc + l0:dc + l0 + n].astype(f32)
                y = a * (half * jnp.tanh(half * g) + half)
                halves[j % GLU_CHAINS] = pltpu.bitcast(
                    pltpu.bitcast(half, jnp.uint32) | _zero_bits(y[0:1, :]), f32)
                ys.append(y)
            return ys

        def store(ys):
            for j, y in enumerate(ys):
                buf_ref[halo + j * rb:halo + (j + 1) * rb, l0:l0 + n] = y
        return compute, store

    def conv_item(units):
        def compute():
            ys, dep = [], None
            for t0, l0 in units:
                x = buf_ref[t0:t0 + halo + rb, l0:l0 + LANES]
                y, dep = _conv_rows(x, cw_ref, slice(l0, l0 + LANES), rb, dep)
                ys.append(y + cb_ref[:, l0:l0 + LANES])
            return ys

        def store(ys):
            for y, (t0, l0) in zip(ys, units):
                acc_ref[t0:t0 + rb, l0:l0 + LANES] = y
        return compute, store

    def ln_item(r0, n):
        def compute():
            y = acc_ref[r0:r0 + n, :]
            mu = jnp.mean(y, axis=-1, keepdims=True)
            yc = y - mu
            var = jnp.mean(yc * yc, axis=-1, keepdims=True)
            yn = yc * lax.rsqrt(var + EPS) * lng_ref[...] + lnb_ref[...]
            return (_silu(yn) * _silu(z_ref[rd, r0:r0 + n, :].astype(f32))).astype(c_ref.dtype)

        def store(out):
            c_ref[r0:r0 + n, :] = out
        return compute, store

    def to_proj(col, val):
        proj_ref[:, col:col + tn] = val

    def to_ag(col, val):
        ag_ref[:, col - a0:col - a0 + tn] = val

    def to_z(col, val):
        z_ref[wr, :, col - zc0:col - zc0 + tn] = val

    q0, k0, v0, o0, zm0, a0, g0, zc0 = (j * d for j in range(8))
    n_grp = d // tn
    n_chunks = 8 * n_grp
    mm_order = ([(to_proj, q0 + j * tn) for j in range(2 * n_grp)]
                + [(to_ag, a0 + j * tn) for j in range(2 * n_grp)]
                + [(to_proj, v0 + j * tn) for j in range(2 * n_grp)]
                + [(to_proj, zm0 + j * tn) for j in range(n_grp)]
                + [(to_z, zc0 + j * tn) for j in range(n_grp)])
    n_glu = max(n_grp // 2, 1)
    n_ln = 2 * n_grp
    n_conv = n_chunks - n_glu - n_ln
    assert n_glu <= 2 * n_grp
    units = [(t0, l0) for l0 in range(0, dc, LANES) for t0 in range(0, tm, rb)]
    cuts = [len(units) * j // n_conv for j in range(n_conv + 1)]
    work = ([glu_item(j * (dc // n_glu), dc // n_glu) for j in range(n_glu)]
            + [conv_item(units[cuts[j]:cuts[j + 1]]) for j in range(n_conv)]
            + [ln_item(j * (tm // n_ln), tm // n_ln) for j in range(n_ln)])

    once = jnp.minimum(i, 0) + 1

    for (mm_store, col), (compute, store) in zip(mm_order, work):
        def region(_, carry, mm_store=mm_store, col=col, compute=compute, store=store):
            out = compute()
            res = matmul(col)
            mm_store(col, res)
            store(out)
            return carry
        lax.fori_loop(0, once, region, 0)


def _proj_conv(x2, g, w_mlstm, w_cv, wg_t, bg, conv_w, conv_b, ln_g, ln_b, *, seq, d_conv,
               tm=512, tn=512, rb=64):
    t, d = x2.shape
    dm = w_mlstm.shape[1]
    assert (d == d_conv and dm == 5 * d and w_cv.shape[1] == 3 * d_conv and d % tn == 0
            and seq % tm == 0 and tm % rb == 0)
    n = t // tm
    last = n - 1
    const = lambda i: (0, 0)
    vec = pl.BlockSpec((1, d_conv), const)
    return pl.pallas_call(
        functools.partial(_proj_conv_kernel, tiles_per_seq=seq // tm, tn=tn, rb=rb),
        grid=(n + 1,),
        in_specs=[
            pl.BlockSpec((tm, d), lambda i: (jnp.minimum(i, last), 0)),
            pl.BlockSpec((1, d), const),
            pl.BlockSpec(w_mlstm.shape, const, pipeline_mode=pl.Buffered(1)),
            pl.BlockSpec(w_cv.shape, const, pipeline_mode=pl.Buffered(1)),
            pl.BlockSpec((2 * N_HEADS, d), const),
            pl.BlockSpec((2 * N_HEADS, 1), const),
            pl.BlockSpec((CONV_WIDTH, d_conv), const),
            vec, vec, vec,
        ],
        out_specs=[
            pl.BlockSpec((tm, dm), lambda i: (jnp.minimum(i, last), 0)),
            pl.BlockSpec((2 * N_HEADS, tm), lambda i: (0, jnp.minimum(i, last))),
            pl.BlockSpec((tm, d_conv), lambda i: (jnp.maximum(i - 1, 0), 0)),
        ],
        out_shape=[
            jax.ShapeDtypeStruct((t, dm), bf16),
            jax.ShapeDtypeStruct((2 * N_HEADS, t), f32),
            jax.ShapeDtypeStruct((t, d_conv), bf16),
        ],
        scratch_shapes=[
            pltpu.VMEM((tm, d), bf16),
            pltpu.VMEM((tm, 2 * d_conv), bf16),
            pltpu.VMEM((2, tm, d_conv), bf16),
            pltpu.VMEM((tm + CONV_HALO, d_conv), f32),
            pltpu.VMEM((tm, d_conv), f32),
        ],
        compiler_params=pltpu.CompilerParams(
            dimension_semantics=("arbitrary",), vmem_limit_bytes=VMEM_LIMIT),
        name="proj_conv",
    )(x2, g, w_mlstm, w_cv, wg_t, bg, conv_w, conv_b, ln_g, ln_b)


def _cumsum_lanes(x):
    rows, n = x.shape
    tri = (lax.broadcasted_iota(jnp.int32, (n, n), 0)
           <= lax.broadcasted_iota(jnp.int32, (n, n), 1)).astype(bf16)
    hi = x.astype(bf16).astype(f32)
    mid = (x - hi).astype(bf16).astype(f32)
    lo = x - hi - mid
    parts = jnp.dot(jnp.concatenate([hi, mid, lo], axis=0).astype(bf16), tri, preferred_element_type=f32)
    return parts[0:rows] + parts[rows:2 * rows] + parts[2 * rows:3 * rows]


def _mlstm_out_kernel(q_ref, k_ref, v_ref, o_ref, z_ref, gates_ref, mhg_ref,
                      x_ref, cc_ref, wa_ref, wc_ref, fg_ref, out_ref,
                      c_ref, n_ref, m_ref, hm_ref, *, head_dim, chunks_per_seq, final_norm):
    L = q_ref.shape[0]
    scale = head_dim ** -0.5
    step = pl.program_id(0)
    wr = step % 2
    rd = 1 - wr

    @pl.when(step == 0)
    def _():
        hm_ref[...] = jnp.zeros_like(hm_ref)

    @pl.when(step % chunks_per_seq == 0)
    def _():
        c_ref[...] = jnp.zeros_like(c_ref)
        n_ref[...] = jnp.zeros_like(n_ref)
        m_ref[...] = jnp.full_like(m_ref, M_INIT)

    gr = gates_ref[...]
    lf = jnp.minimum(gr, 0.0) - jnp.log1p(jnp.exp(-jnp.abs(gr)))
    bfull = _cumsum_lanes(lf)
    ig = gr[0:N_HEADS]
    b = bfull[N_HEADS:2 * N_HEADS]
    b_last = b[:, L - 1:L]
    m_prev = m_ref[0:N_HEADS, 0:1]
    a = b_last - b + ig
    a_max = jnp.max(a, axis=1, keepdims=True)
    m_new = jnp.maximum(b_last + m_prev, a_max)
    s_old = jnp.exp(b_last + m_prev - m_new)
    w_a = jnp.exp(a - m_new)
    c_row = ig - b

    cols = jnp.transpose(jnp.concatenate([b, w_a], axis=0))

    row_id = lax.broadcasted_iota(jnp.int32, (L, L), 0)
    col_id = lax.broadcasted_iota(jnp.int32, (L, L), 1)
    causal = col_id <= row_id

    d_out = out_ref.shape[1]
    tn = d_out // N_HEADS

    heads = range(N_HEADS)
    hs = [slice(h * head_dim, (h + 1) * head_dim) for h in heads]
    ys = []

    def out_proj_part(j):
        ol = slice(j * tn, (j + 1) * tn)
        ys.append(jnp.dot(hm_ref[rd], wa_ref[:, ol], preferred_element_type=f32)
                  + jnp.dot(cc_ref[...], wc_ref[:, ol], preferred_element_type=f32))

    s_ = [lax.dot_general(q_ref[:, hs[h]], k_ref[:, hs[h]], _NT, preferred_element_type=f32) * scale
          for h in heads]
    c_prev = [c_ref[h] for h in heads]
    qc = [jnp.dot(q_ref[:, hs[h]], c_prev[h].astype(bf16), preferred_element_type=f32) for h in heads]

    p_, m_t_, s_inter_ = [], [], []
    for h in heads:
        b_col = cols[:, h:h + 1]
        g = jnp.where(causal, b_col + c_row[h:h + 1, :], -jnp.inf)
        li = b_col + m_prev[h:h + 1, :]
        m_t = jnp.maximum(li, jnp.max(g, axis=1, keepdims=True))
        p_.append(jnp.exp(g - m_t) * s_[h])
        m_t_.append(m_t)
        s_inter_.append(jnp.exp(li - m_t))

    pv = [jnp.dot(p_[h].astype(bf16), v_ref[:, hs[h]], preferred_element_type=f32) for h in heads]
    out_proj_part(0)
    out_proj_part(1)

    hm_new, vw = [], []
    for h in heads:
        n_prev = n_ref[h:h + 1, :]
        num = pv[h] + s_inter_[h] * qc[h]
        den = (jnp.sum(p_[h], axis=1, keepdims=True)
               + s_inter_[h] * jnp.sum(q_ref[:, hs[h]].astype(f32) * n_prev, axis=1, keepdims=True))
        hh = num * (1.0 / jnp.maximum(jnp.abs(den), jnp.exp(-m_t_[h])))
        hh = _sigmoid(o_ref[:, hs[h]].astype(f32)) * hh
        hh = hh * lax.rsqrt(jnp.mean(hh * hh, axis=-1, keepdims=True) + EPS)
        hh = hh * mhg_ref[:, hs[h]]
        hm_new.append((hh * _silu(z_ref[:, hs[h]].astype(f32))).astype(bf16))
        w_col = cols[:, N_HEADS + h:N_HEADS + h + 1]
        vw.append((v_ref[:, hs[h]].astype(f32) * w_col).astype(bf16))

    out_proj_part(2)
    kv = [lax.dot_general(k_ref[:, hs[h]], vw[h], _TN, preferred_element_type=f32) * scale
          for h in heads]
    out_proj_part(3)
    n_all = jnp.dot(jnp.concatenate([w_a, w_a], axis=0).astype(bf16), k_ref[...],
                    preferred_element_type=f32) * scale
    c_new =[s_old[h:h + 1, :] * c_prev[h] + kv[h] for h in heads]
    n_new = [s_old[h:h + 1, :] * n_ref[h:h + 1, :] + n_all[h:h + 1, hs[h]] for h in heads]

    y = x_ref[...] + jnp.concatenate(ys, axis=1)
    if final_norm:
        y = y * lax.rsqrt(jnp.mean(y * y, axis=-1, keepdims=True) + EPS) * fg_ref[...]
    out_ref[...] = y
    for h in heads:
        hm_ref[wr, :, hs[h]] = hm_new[h]
        c_ref[h] = c_new[h]
        n_ref[h:h + 1, :] = n_new[h]
    m_ref[0:N_HEADS, :] = jnp.broadcast_to(m_new, (N_HEADS, m_ref.shape[1]))


def _mlstm_out(proj, gates, mhg, x2, cc, w_a, w_c, fg, *, seq, d_mlstm, final_norm):
    t, d = x2.shape
    L = MLSTM_CHUNK
    nc = seq // L
    n = t // L
    last = n - 1
    hd = d_mlstm // N_HEADS
    assert d % N_HEADS == 0 and seq % L == 0

    def col(j):
        return pl.BlockSpec((L, d_mlstm), lambda s, j=j: (jnp.minimum(s, last), j))

    prev = lambda s: (jnp.maximum(s - 1, 0), 0)
    const = lambda s: (0, 0)
    return pl.pallas_call(
        functools.partial(_mlstm_out_kernel, head_dim=hd, chunks_per_seq=nc, final_norm=final_norm),
        grid=(n + 1,),
        in_specs=[col(0), col(1), col(2), col(3), col(4),
                  pl.BlockSpec((2 * N_HEADS, L), lambda s: (0, jnp.minimum(s, last))),
                  pl.BlockSpec((1, d_mlstm), const),
                  pl.BlockSpec((L, d), prev),
                  pl.BlockSpec((L, cc.shape[1]), prev),
                  pl.BlockSpec(w_a.shape, const, pipeline_mode=pl.Buffered(1)),
                  pl.BlockSpec(w_c.shape, const, pipeline_mode=pl.Buffered(1)),
                  pl.BlockSpec((1, d), const)],
        out_specs=pl.BlockSpec((L, d), prev),
        out_shape=jax.ShapeDtypeStruct((t, d), f32),
        scratch_shapes=[pltpu.VMEM((N_HEADS, hd, hd), f32),
                        pltpu.VMEM((SUBLANES, hd), f32),
                        pltpu.VMEM((SUBLANES, LANES), f32),
                        pltpu.VMEM((2, L, d_mlstm), bf16)],
        compiler_params=pltpu.CompilerParams(
            dimension_semantics=("arbitrary",), vmem_limit_bytes=VMEM_LIMIT),
        name="mlstm_out",
    )(proj, proj, proj, proj, proj, gates, mhg, x2, cc, w_a, w_c, fg)


def _split_w_in_kernel(w_ref, wm_ref, wc_ref, wg_ref):
    dm = wm_ref.shape[1]
    ng = wg_ref.shape[1]
    wm_ref[...] = w_ref[:, 0:dm].astype(bf16)
    wg_ref[...] = w_ref[:, dm:dm + ng]
    wc_ref[...] = w_ref[:, dm + ng:].astype(bf16)


def _split_w_in(w, layer, dm, n_gate, *, tk=256):
    _, k, n_all = w.shape
    dcv = n_all - dm - n_gate
    return pl.pallas_call(
        _split_w_in_kernel,
        grid=(k // tk,),
        in_specs=[pl.BlockSpec((None, tk, n_all), lambda i: (layer, i, 0))],
        out_specs=[pl.BlockSpec((tk, dm), lambda i: (i, 0)),
                   pl.BlockSpec((tk, dcv), lambda i: (i, 0)),
                   pl.BlockSpec((tk, n_gate), lambda i: (i, 0))],
        out_shape=[jax.ShapeDtypeStruct((k, dm), bf16),
                   jax.ShapeDtypeStruct((k, dcv), bf16),
                   jax.ShapeDtypeStruct((k, n_gate), w.dtype)],
        compiler_params=pltpu.CompilerParams(
            dimension_semantics=("parallel",), vmem_limit_bytes=VMEM_LIMIT),
        name="split_w_in",
    )(w)


def kernel(x, norm_g, w_in, b_gates, mh_norm_g, conv_w, conv_b, conv_ln_g, conv_ln_b, w_out, final_norm_g):
    batch, seq, d_model = x.shape
    depth = norm_g.shape[0]
    d_mlstm = mh_norm_g.shape[1]
    d_conv = conv_b.shape[1]
    n_gate = 2 * N_HEADS
    g0 = 5 * d_mlstm

    h = x.reshape(batch * seq, d_model)
    for l in range(depth):
        w_mlstm, w_cv, wg = _split_w_in(w_in, l, g0, n_gate)
        wg_t = wg.T.astype(bf16)
        proj, gates, c = _proj_conv(h, norm_g[l][None, :], w_mlstm, w_cv, wg_t, b_gates[l][:, None],
                                    conv_w[l], conv_b[l][None, :], conv_ln_g[l][None, :],
                                    conv_ln_b[l][None, :], seq=seq, d_conv=d_conv)
        wo = w_out[l].astype(bf16)
        h = _mlstm_out(proj, gates, mh_norm_g[l][None, :], h, c, wo[:d_mlstm], wo[d_mlstm:],
                       final_norm_g[None, :], seq=seq, d_mlstm=d_mlstm, final_norm=(l == depth - 1))
    return h.reshape(batch, seq, d_model)
```

```python
import functools

import jax
import jax.numpy as jnp
from jax import lax
from jax.experimental import pallas as pl
from jax.experimental.pallas import tpu as pltpu

N_HEADS = 4
CONV_WIDTH = 31
EPS = 1e-6
M_INIT = -1e30

MLSTM_CHUNK = 256
LANES = 128
SUBLANES = 8
CONV_HALO = 32
GLU_CHAINS = 3
VMEM_LIMIT = 56 * 1024 * 1024

f32 = jnp.float32
bf16 = jnp.bfloat16

_NT = (((1,), (1,)), ((), ()))
_TN = (((0,), (0,)), ((), ()))


def _zero_bits(x):
    b = pltpu.bitcast(x, jnp.uint32)
    return lax.shift_right_logical(lax.shift_right_logical(b, jnp.uint32(16)), jnp.uint32(16))


def _sigmoid(x):
    return 0.5 * jnp.tanh(0.5 * x) + 0.5


def _silu(x):
    hx = 0.5 * x
    return hx * jnp.tanh(hx) + hx


def _conv_rows(x, w_ref, lanes, rb, dep):
    acc = None
    for r in range(SUBLANES):
        nq = (CONV_WIDTH - 1 - r) // SUBLANES + 1
        lo = 0 if r == 0 else SUBLANES
        p = None
        for q in range(nq):
            j = CONV_WIDTH - 1 - (SUBLANES * q + r)
            start = CONV_HALO - lo - SUBLANES * q
            wj = w_ref[j:j + 1, lanes]
            if dep is not None:
                wj = pltpu.bitcast(pltpu.bitcast(wj, jnp.uint32) | dep, f32)
            term = wj * x[start:start + rb + lo, :]
            p = term if p is None else p + term
        acc = p if r == 0 else acc + p[SUBLANES - r:SUBLANES - r + rb, :]
        dep = _zero_bits(acc[0:1, :])
    return acc, dep


def _proj_conv_kernel(x_ref, g_ref, wm_ref, wc_ref, wg_ref, bg_ref, cw_ref, cb_ref, lng_ref, lnb_ref,
                      proj_ref, gates_ref, c_ref,
                      u_ref, ag_ref, z_ref, buf_ref, acc_ref, *, tiles_per_seq, tn, rb):
    i = pl.program_id(0)
    tm, d = x_ref.shape
    dc = c_ref.shape[1]
    halo = CONV_HALO
    wr = i % 2
    rd = 1 - wr

    @pl.when(i == 0)
    def _():
        ag_ref[...] = jnp.zeros_like(ag_ref)
        z_ref[...] = jnp.zeros_like(z_ref)
        buf_ref[...] = jnp.zeros_like(buf_ref)

    first_of_seq = ((i + tiles_per_seq - 1) % tiles_per_seq) == 0

    @pl.when(first_of_seq)
    def _():
        buf_ref[0:halo, :] = jnp.zeros((halo, dc), f32)

    @pl.when(jnp.logical_not(first_of_seq))
    def _():
        buf_ref[0:halo, :] = buf_ref[tm:tm + halo, :]

    x = x_ref[...]
    u = x * lax.rsqrt(jnp.mean(x * x, axis=-1, keepdims=True) + EPS) * g_ref[...]
    u_ref[...] = u.astype(bf16)
    gates_ref[...] = lax.dot_general(wg_ref[...], u_ref[...], _NT, preferred_element_type=f32) + bg_ref[...]

    dm = wm_ref.shape[0]

    def matmul(col):
        w = wm_ref[col:col + tn, :] if col < dm else wc_ref[col - dm:col - dm + tn, :]
        return lax.dot_general(u_ref[...], w, _NT, preferred_element_type=f32).astype(bf16)

    def glu_item(l0, n):
        def compute():
            halves = [jnp.full((1, n), 0.5, f32)] * GLU_CHAINS
            ys = []
            for j, r0 in enumerate(range(0, tm, rb)):
                half = halves[j % GLU_CHAINS]
                a = ag_ref[r0:r0 + rb, l0:l0 + n].astype(f32)
                g = ag_ref[r0:r0 + rb, dc + l0:dc + l0 + n].astype(f32)
                y = a * (half * jnp.tanh(half * g) + half)
                halves[j % GLU_CHAINS] = pltpu.bitcast(
                    pltpu.bitcast(half, jnp.uint32) | _zero_bits(y[0:1, :]), f32)
                ys.append(y)
            return ys

        def store(ys):
            for j, y in enumerate(ys):
                buf_ref[halo + j * rb:halo + (j + 1) * rb, l0:l0 + n] = y
        return compute, store

    def conv_item(units):
        def compute():
            ys, dep = [], None
            for t0, l0 in units:
                x = buf_ref[t0:t0 + halo + rb, l0:l0 + LANES]
                y, dep = _conv_rows(x, cw_ref, slice(l0, l0 + LANES), rb, dep)
                ys.append(y + cb_ref[:, l0:l0 + LANES])
            return ys

        def store(ys):
            for y, (t0, l0) in zip(ys, units):
                acc_ref[t0:t0 + rb, l0:l0 + LANES] = y
        return compute, store

    def ln_item(r0, n):
        def compute():
            y = acc_ref[r0:r0 + n, :]
            mu = jnp.mean(y, axis=-1, keepdims=True)
            yc = y - mu
            var = jnp.mean(yc * yc, axis=-1, keepdims=True)
            yn = yc * lax.rsqrt(var + EPS) * lng_ref[...] + lnb_ref[...]
            return (_silu(yn) * _silu(z_ref[rd, r0:r0 + n, :].astype(f32))).astype(c_ref.dtype)

        def store(out):
            c_ref[r0:r0 + n, :] = out
        return compute, store

    def to_proj(col, val):
        proj_ref[:, col:col + tn] = val

    def to_ag(col, val):
        ag_ref[:, col - a0:col - a0 + tn] = val

    def to_z(col, val):
        z_ref[wr, :, col - zc0:col - zc0 + tn] = val

    q0, k0, v0, o0, zm0, a0, g0, zc0 = (j * d for j in range(8))
    n_grp = d // tn
    n_chunks = 8 * n_grp
    mm_order = ([(to_proj, q0 + j * tn) for j in range(2 * n_grp)]
                + [(to_ag, a0 + j * tn) for j in range(2 * n_grp)]
                + [(to_proj, v0 + j * tn) for j in range(2 * n_grp)]
                + [(to_proj, zm0 + j * tn) for j in range(n_grp)]
                + [(to_z, zc0 + j * tn) for j in range(n_grp)])
    n_glu = max(n_grp // 2, 1)
    n_ln = 2 * n_grp
    n_conv = n_chunks - n_glu - n_ln
    assert n_glu <= 2 * n_grp
    units = [(t0, l0) for l0 in range(0, dc, LANES) for t0 in range(0, tm, rb)]
    cuts = [len(units) * j // n_conv for j in range(n_conv + 1)]
    work = ([glu_item(j * (dc // n_glu), dc // n_glu) for j in range(n_glu)]
            + [conv_item(units[cuts[j]:cuts[j + 1]]) for j in range(n_conv)]
            + [ln_item(j * (tm // n_ln), tm // n_ln) for j in range(n_ln)])

    once = jnp.minimum(i, 0) + 1

    for (mm_store, col), (compute, store) in zip(mm_order, work):
        def region(_, carry, mm_store=mm_store, col=col, compute=compute, store=store):
            out = compute()
            res = matmul(col)
            mm_store(col, res)
            store(out)
            return carry
        lax.fori_loop(0, once, region, 0)


def _proj_conv(x2, g, wt_mlstm, wt_cv, wg_t, bg, conv_w, conv_b, ln_g, ln_b, *, seq, d_conv,
               tm=512, tn=512, rb=64):
    t, d = x2.shape
    dm = wt_mlstm.shape[0]
    assert (d == d_conv and dm == 5 * d and wt_cv.shape[0] == 3 * d_conv and d % tn == 0
            and seq % tm == 0 and tm % rb == 0)
    n = t // tm
    last = n - 1
    const = lambda i: (0, 0)
    vec = pl.BlockSpec((1, d_conv), const)
    return pl.pallas_call(
        functools.partial(_proj_conv_kernel, tiles_per_seq=seq // tm, tn=tn, rb=rb),
        grid=(n + 1,),
        in_specs=[
            pl.BlockSpec((tm, d), lambda i: (jnp.minimum(i, last), 0)),
            pl.BlockSpec((1, d), const),
            pl.BlockSpec(wt_mlstm.shape, const, pipeline_mode=pl.Buffered(1)),
            pl.BlockSpec(wt_cv.shape, const, pipeline_mode=pl.Buffered(1)),
            pl.BlockSpec((2 * N_HEADS, d), const),
            pl.BlockSpec((2 * N_HEADS, 1), const),
            pl.BlockSpec((CONV_WIDTH, d_conv), const),
            vec, vec, vec,
        ],
        out_specs=[
            pl.BlockSpec((tm, dm), lambda i: (jnp.minimum(i, last), 0)),
            pl.BlockSpec((2 * N_HEADS, tm), lambda i: (0, jnp.minimum(i, last))),
            pl.BlockSpec((tm, d_conv), lambda i: (jnp.maximum(i - 1, 0), 0)),
        ],
        out_shape=[
            jax.ShapeDtypeStruct((t, dm), bf16),
            jax.ShapeDtypeStruct((2 * N_HEADS, t), f32),
            jax.ShapeDtypeStruct((t, d_conv), bf16),
        ],
        scratch_shapes=[
            pltpu.VMEM((tm, d), bf16),
            pltpu.VMEM((tm, 2 * d_conv), bf16),
            pltpu.VMEM((2, tm, d_conv), bf16),
            pltpu.VMEM((tm + CONV_HALO, d_conv), f32),
            pltpu.VMEM((tm, d_conv), f32),
        ],
        compiler_params=pltpu.CompilerParams(
            dimension_semantics=("arbitrary",), vmem_limit_bytes=VMEM_LIMIT),
        name="proj_conv",
    )(x2, g, wt_mlstm, wt_cv, wg_t, bg, conv_w, conv_b, ln_g, ln_b)


def _cumsum_lanes(x):
    rows, n = x.shape
    tri = (lax.broadcasted_iota(jnp.int32, (n, n), 0)
           <= lax.broadcasted_iota(jnp.int32, (n, n), 1)).astype(bf16)
    hi = x.astype(bf16).astype(f32)
    mid = (x - hi).astype(bf16).astype(f32)
    lo = x - hi - mid
    parts = jnp.dot(jnp.concatenate([hi, mid, lo], axis=0).astype(bf16), tri, preferred_element_type=f32)
    return parts[0:rows] + parts[rows:2 * rows] + parts[2 * rows:3 * rows]


def _mlstm_out_kernel(q_ref, k_ref, v_ref, o_ref, z_ref, gates_ref, mhg_ref,
                      x_ref, cc_ref, wa_ref, wc_ref, fg_ref, out_ref,
                      c_ref, n_ref, m_ref, hm_ref, *, head_dim, chunks_per_seq, final_norm):
    L = q_ref.shape[0]
    scale = head_dim ** -0.5
    step = pl.program_id(0)
    wr = step % 2
    rd = 1 - wr

    @pl.when(step == 0)
    def _():
        hm_ref[...] = jnp.zeros_like(hm_ref)

    @pl.when(step % chunks_per_seq == 0)
    def _():
        c_ref[...] = jnp.zeros_like(c_ref)
        n_ref[...] = jnp.zeros_like(n_ref)
        m_ref[...] = jnp.full_like(m_ref, M_INIT)

    gr = gates_ref[...]
    lf = jnp.minimum(gr, 0.0) - jnp.log1p(jnp.exp(-jnp.abs(gr)))
    bfull = _cumsum_lanes(lf)
    ig = gr[0:N_HEADS]
    b = bfull[N_HEADS:2 * N_HEADS]
    b_last = b[:, L - 1:L]
    m_prev = m_ref[0:N_HEADS, 0:1]
    a = b_last - b + ig
    a_max = jnp.max(a, axis=1, keepdims=True)
    m_new = jnp.maximum(b_last + m_prev, a_max)
    s_old = jnp.exp(b_last + m_prev - m_new)
    w_a = jnp.exp(a - m_new)
    c_row = ig - b

    cols = jnp.transpose(jnp.concatenate([b, w_a], axis=0))

    row_id = lax.broadcasted_iota(jnp.int32, (L, L), 0)
    col_id = lax.broadcasted_iota(jnp.int32, (L, L), 1)
    causal = col_id <= row_id

    d_out = out_ref.shape[1]
    tn = d_out // N_HEADS

    heads = range(N_HEADS)
    hs = [slice(h * head_dim, (h + 1) * head_dim) for h in heads]
    ys = []

    def out_proj_part(j):
        ol = slice(j * tn, (j + 1) * tn)
        ys.append(jnp.dot(hm_ref[rd], wa_ref[:, ol], preferred_element_type=f32)
                  + jnp.dot(cc_ref[...], wc_ref[:, ol], preferred_element_type=f32))

    s_ = [lax.dot_general(q_ref[:, hs[h]], k_ref[:, hs[h]], _NT, preferred_element_type=f32) * scale
          for h in heads]
    c_prev = [c_ref[h] for h in heads]
    qc = [jnp.dot(q_ref[:, hs[h]], c_prev[h].astype(bf16), preferred_element_type=f32) for h in heads]

    p_, m_t_, s_inter_ = [], [], []
    for h in heads:
        b_col = cols[:, h:h + 1]
        g = jnp.where(causal, b_col + c_row[h:h + 1, :], -jnp.inf)
        li = b_col + m_prev[h:h + 1, :]
        m_t = jnp.maximum(li, jnp.max(g, axis=1, keepdims=True))
        p_.append(jnp.exp(g - m_t) * s_[h])
        m_t_.append(m_t)
        s_inter_.append(jnp.exp(li - m_t))

    pv = [jnp.dot(p_[h].astype(bf16), v_ref[:, hs[h]], preferred_element_type=f32) for h in heads]
    out_proj_part(0)
    out_proj_part(1)

    hm_new, vw = [], []
    for h in heads:
        n_prev = n_ref[h:h + 1, :]
        num = pv[h] + s_inter_[h] * qc[h]
        den = (jnp.sum(p_[h], axis=1, keepdims=True)
               + s_inter_[h] * jnp.sum(q_ref[:, hs[h]].astype(f32) * n_prev, axis=1, keepdims=True))
        hh = num * (1.0 / jnp.maximum(jnp.abs(den), jnp.exp(-m_t_[h])))
        hh = _sigmoid(o_ref[:, hs[h]].astype(f32)) * hh
        hh = hh * lax.rsqrt(jnp.mean(hh * hh, axis=-1, keepdims=True) + EPS)
        hh = hh * mhg_ref[:, hs[h]]
        hm_new.append((hh * _silu(z_ref[:, hs[h]].astype(f32))).astype(bf16))
        w_col = cols[:, N_HEADS + h:N_HEADS + h + 1]
        vw.append((v_ref[:, hs[h]].astype(f32) * w_col).astype(bf16))

    out_proj_part(2)
    kv = [lax.dot_general(k_ref[:, hs[h]], vw[h], _TN, preferred_element_type=f32) * scale
          for h in heads]
    out_proj_part(3)
    n_all = jnp.dot(jnp.concatenate([w_a, w_a], axis=0).astype(bf16), k_ref[...],
                    preferred_element_type=f32) * scale
    c_new =[s_old[h:h + 1, :] * c_prev[h] + kv[h] for h in heads]
    n_new = [s_old[h:h + 1, :] * n_ref[h:h + 1, :] + n_all[h:h + 1, hs[h]] for h in heads]

    y = x_ref[...] + jnp.concatenate(ys, axis=1)
    if final_norm:
        y = y * lax.rsqrt(jnp.mean(y * y, axis=-1, keepdims=True) + EPS) * fg_ref[...]
    out_ref[...] = y
    for h in heads:
        hm_ref[wr, :, hs[h]] = hm_new[h]
        c_ref[h] = c_new[h]
        n_ref[h:h + 1, :] = n_new[h]
    m_ref[0:N_HEADS, :] = jnp.broadcast_to(m_new, (N_HEADS, m_ref.shape[1]))


def _mlstm_out(proj, gates, mhg, x2, cc, w_a, w_c, fg, *, seq, d_mlstm, final_norm):
    t, d = x2.shape
    L = MLSTM_CHUNK
    nc = seq // L
    n = t // L
    last = n - 1
    hd = d_mlstm // N_HEADS
    assert d % N_HEADS == 0 and seq % L == 0

    def col(j):
        return pl.BlockSpec((L, d_mlstm), lambda s, j=j: (jnp.minimum(s, last), j))

    prev = lambda s: (jnp.maximum(s - 1, 0), 0)
    const = lambda s: (0, 0)
    return pl.pallas_call(
        functools.partial(_mlstm_out_kernel, head_dim=hd, chunks_per_seq=nc, final_norm=final_norm),
        grid=(n + 1,),
        in_specs=[col(0), col(1), col(2), col(3), col(4),
                  pl.BlockSpec((2 * N_HEADS, L), lambda s: (0, jnp.minimum(s, last))),
                  pl.BlockSpec((1, d_mlstm), const),
                  pl.BlockSpec((L, d), prev),
                  pl.BlockSpec((L, cc.shape[1]), prev),
                  pl.BlockSpec(w_a.shape, const, pipeline_mode=pl.Buffered(1)),
                  pl.BlockSpec(w_c.shape, const, pipeline_mode=pl.Buffered(1)),
                  pl.BlockSpec((1, d), const)],
        out_specs=pl.BlockSpec((L, d), prev),
        out_shape=jax.ShapeDtypeStruct((t, d), f32),
        scratch_shapes=[pltpu.VMEM((N_HEADS, hd, hd), f32),
                        pltpu.VMEM((SUBLANES, hd), f32),
                        pltpu.VMEM((SUBLANES, LANES), f32),
                        pltpu.VMEM((2, L, d_mlstm), bf16)],
        compiler_params=pltpu.CompilerParams(
            dimension_semantics=("arbitrary",), vmem_limit_bytes=VMEM_LIMIT),
        name="mlstm_out",
    )(proj, proj, proj, proj, proj, gates, mhg, x2, cc, w_a, w_c, fg)


def kernel(x, norm_g, w_in, b_gates, mh_norm_g, conv_w, conv_b, conv_ln_g, conv_ln_b, w_out, final_norm_g):
    batch, seq, d_model = x.shape
    depth = norm_g.shape[0]
    d_mlstm = mh_norm_g.shape[1]
    d_conv = conv_b.shape[1]
    n_gate = 2 * N_HEADS
    g0 = 5 * d_mlstm

    h = x.reshape(batch * seq, d_model)
    for l in range(depth):
        w_t = jnp.swapaxes(w_in[l], 0, 1)
        wt_mlstm = w_t[:g0].astype(bf16)
        wt_cv = w_t[g0 + n_gate:].astype(bf16)
        wg_t = w_t[g0:g0 + n_gate].astype(bf16)
        proj, gates, c = _proj_conv(h, norm_g[l][None, :], wt_mlstm, wt_cv, wg_t, b_gates[l][:, None],
                                    conv_w[l], conv_b[l][None, :], conv_ln_g[l][None, :],
                                    conv_ln_b[l][None, :], seq=seq, d_conv=d_conv)
        wo = w_out[l].astype(bf16)
        h = _mlstm_out(proj, gates, mh_norm_g[l][None, :], h, c, wo[:d_mlstm], wo[d_mlstm:],
                       final_norm_g[None, :], seq=seq, d_mlstm=d_mlstm, final_norm=(l == depth - 1))
    return h.reshape(batch, seq, d_model)
```

```python
import functools

import jax
import jax.numpy as jnp
from jax import lax
from jax.experimental import pallas as pl
from jax.experimental.pallas import tpu as pltpu

N_HEADS = 4
CONV_WIDTH = 31
EPS = 1e-6
M_INIT = -1e30

MLSTM_CHUNK = 256
LANES = 128
SUBLANES = 8
CONV_HALO = 32
VMEM_LIMIT = 56 * 1024 * 1024

f32 = jnp.float32
bf16 = jnp.bfloat16

_NT = (((1,), (1,)), ((), ()))
_TN = (((0,), (0,)), ((), ()))


def _zero_bits(x):
    b = pltpu.bitcast(x, jnp.uint32)
    return lax.shift_right_logical(lax.shift_right_logical(b, jnp.uint32(16)), jnp.uint32(16))


def _sigmoid(x):
    return 0.5 * jnp.tanh(0.5 * x) + 0.5


def _silu(x):
    hx = 0.5 * x
    return hx * jnp.tanh(hx) + hx


def _conv_rows(x, w_ref, lanes, rb, dep):
    acc = None
    for r in range(SUBLANES):
        nq = (CONV_WIDTH - 1 - r) // SUBLANES + 1
        lo = 0 if r == 0 else SUBLANES
        p = None
        for q in range(nq):
            j = CONV_WIDTH - 1 - (SUBLANES * q + r)
            start = CONV_HALO - lo - SUBLANES * q
            wj = w_ref[j:j + 1, lanes]
            if dep is not None:
                wj = pltpu.bitcast(pltpu.bitcast(wj, jnp.uint32) | dep, f32)
            term = wj * x[start:start + rb + lo, :]
            p = term if p is None else p + term
        acc = p if r == 0 else acc + p[SUBLANES - r:SUBLANES - r + rb, :]
        dep = _zero_bits(acc[0:1, :])
    return acc, dep


def _proj_conv_kernel(x_ref, g_ref, wm_ref, wc_ref, wg_ref, bg_ref, cw_ref, cb_ref, lng_ref, lnb_ref,
                      proj_ref, gates_ref, c_ref,
                      u_ref, ag_ref, z_ref, buf_ref, acc_ref, *, tiles_per_seq, tn, rb):
    i = pl.program_id(0)
    tm, d = x_ref.shape
    dc = c_ref.shape[1]
    halo = CONV_HALO

    @pl.when(i == 0)
    def _():
        z_ref[...] = jnp.zeros_like(z_ref)
        buf_ref[...] = jnp.zeros_like(buf_ref)

    x = x_ref[...]
    u = x * lax.rsqrt(jnp.mean(x * x, axis=-1, keepdims=True) + EPS) * g_ref[...]
    u_ref[...] = u.astype(bf16)
    gates_ref[...] = lax.dot_general(wg_ref[...], u_ref[...], _NT, preferred_element_type=f32) + bg_ref[...]

    dm = wm_ref.shape[0]

    def matmul(col):
        w = wm_ref[col:col + tn, :] if col < dm else wc_ref[col - dm:col - dm + tn, :]
        return lax.dot_general(u_ref[...], w, _NT, preferred_element_type=f32).astype(bf16)

    def glu_item(r0, n, with_history):
        def compute():
            half = jnp.full((1, dc), 0.5, f32)
            ys = []
            for p0 in range(r0, r0 + n, rb):
                a = ag_ref[p0:p0 + rb, 0:dc].astype(f32)
                g = ag_ref[p0:p0 + rb, dc:2 * dc].astype(f32)
                y = a * (half * jnp.tanh(half * g) + half)
                half = pltpu.bitcast(pltpu.bitcast(half, jnp.uint32) | _zero_bits(y[0:1, :]), f32)
                ys.append(y)
            if with_history:
                tail = buf_ref[tm:tm + halo, :]
                ys.append(jnp.where(i % tiles_per_seq == 0, jnp.zeros_like(tail), tail))
            return ys

        def store(ys):
            for j in range(n // rb):
                buf_ref[halo + r0 + j * rb:halo + r0 + (j + 1) * rb, :] = ys[j]
            if with_history:
                buf_ref[0:halo, :] = ys[-1]
        return compute, store

    def conv_item(units):
        def compute():
            ys, dep = [], None
            for t0, l0 in units:
                x = buf_ref[t0:t0 + halo + rb, l0:l0 + LANES]
                y, dep = _conv_rows(x, cw_ref, slice(l0, l0 + LANES), rb, dep)
                ys.append(y + cb_ref[:, l0:l0 + LANES])
            return ys

        def store(ys):
            for y, (t0, l0) in zip(ys, units):
                acc_ref[t0:t0 + rb, l0:l0 + LANES] = y
        return compute, store

    def ln_item(r0, n):
        def compute():
            outs = []
            gain = lng_ref[...]
            for p0 in range(r0, r0 + n, rb):
                y = acc_ref[p0:p0 + rb, :]
                mu = jnp.mean(y, axis=-1, keepdims=True)
                yc = y - mu
                var = jnp.mean(yc * yc, axis=-1, keepdims=True)
                yn = yc * lax.rsqrt(var + EPS) * gain + lnb_ref[...]
                out = _silu(yn) * _silu(z_ref[p0:p0 + rb, :].astype(f32))
                gain = pltpu.bitcast(pltpu.bitcast(gain, jnp.uint32) | _zero_bits(out[0:1, :]), f32)
                outs.append(out.astype(c_ref.dtype))
            return outs

        def store(outs):
            for j, out in enumerate(outs):
                c_ref[r0 + j * rb:r0 + (j + 1) * rb, :] = out
        return compute, store

    def to_proj(col, val):
        proj_ref[:, col:col + tn] = val

    def to_ag(col, val):
        ag_ref[:, col - a0:col - a0 + tn] = val

    def to_z(col, val):
        z_ref[:, col - zc0:col - zc0 + tn] = val

    q0, k0, v0, o0, zm0, a0, g0, zc0 = (j * d for j in range(8))
    assert tn == d and tm == 8 * rb
    mm_order = [(to_proj, q0), (to_proj, k0), (to_ag, a0), (to_ag, g0),
                (to_proj, v0), (to_proj, o0), (to_proj, zm0), (to_z, zc0)]
    units = [(t0, l0) for t0 in range(0, tm, rb) for l0 in range(0, dc, LANES)]
    n_units = (11, 11, 11, 11, 10, 10)
    cuts = [sum(n_units[:j]) for j in range(len(n_units) + 1)]
    assert cuts[-1] == len(units)
    glu_rows = tm // 4
    work = [[conv_item(units[cuts[j]:cuts[j + 1]])] for j in range(4)]
    work += [[conv_item(units[cuts[4]:cuts[5]]), glu_item(0, glu_rows, True)],
             [conv_item(units[cuts[5]:cuts[6]]), glu_item(glu_rows, glu_rows, False)],
             [ln_item(0, tm // 2), glu_item(2 * glu_rows, glu_rows, False)],
             [ln_item(tm // 2, tm // 2), glu_item(3 * glu_rows, glu_rows, False)]]

    once = jnp.minimum(i, 0) + 1

    for (mm_store, col), items in zip(mm_order, work):
        def region(_, carry, mm_store=mm_store, col=col, items=items):
            outs = [compute() for compute, _ in items]
            res = matmul(col)
            mm_store(col, res)
            for (_, store), out in zip(items, outs):
                store(out)
            return carry
        lax.fori_loop(0, once, region, 0)


def _proj_conv(x2, g, wt_mlstm, wt_cv, wg_t, bg, conv_w, conv_b, ln_g, ln_b, *, seq, d_conv,
               tm=512, tn=1024, rb=64):
    t, d = x2.shape
    dm = wt_mlstm.shape[0]
    assert (d == d_conv and dm == 5 * d and wt_cv.shape[0] == 3 * d_conv and d % tn == 0
            and seq % tm == 0 and tm % rb == 0)
    n = t // tm
    last = n - 1
    const = lambda i: (0, 0)
    vec = pl.BlockSpec((1, d_conv), const)
    return pl.pallas_call(
        functools.partial(_proj_conv_kernel, tiles_per_seq=seq // tm, tn=tn, rb=rb),
        grid=(n + 1,),
        in_specs=[
            pl.BlockSpec((tm, d), lambda i: (jnp.minimum(i, last), 0)),
            pl.BlockSpec((1, d), const),
            pl.BlockSpec(wt_mlstm.shape, const, pipeline_mode=pl.Buffered(1)),
            pl.BlockSpec(wt_cv.shape, const, pipeline_mode=pl.Buffered(1)),
            pl.BlockSpec((2 * N_HEADS, d), const),
            pl.BlockSpec((2 * N_HEADS, 1), const),
            pl.BlockSpec((CONV_WIDTH, d_conv), const),
            vec, vec, vec,
        ],
        out_specs=[
            pl.BlockSpec((tm, dm), lambda i: (jnp.minimum(i, last), 0)),
            pl.BlockSpec((2 * N_HEADS, tm), lambda i: (0, jnp.minimum(i, last))),
            pl.BlockSpec((tm, d_conv), lambda i: (jnp.maximum(i - 1, 0), 0)),
        ],
        out_shape=[
            jax.ShapeDtypeStruct((t, dm), bf16),
            jax.ShapeDtypeStruct((2 * N_HEADS, t), f32),
            jax.ShapeDtypeStruct((t, d_conv), bf16),
        ],
        scratch_shapes=[
            pltpu.VMEM((tm, d), bf16),
            pltpu.VMEM((tm, 2 * d_conv), bf16),
            pltpu.VMEM((tm, d_conv), bf16),
            pltpu.VMEM((tm + CONV_HALO, d_conv), f32),
            pltpu.VMEM((tm, d_conv), f32),
        ],
        compiler_params=pltpu.CompilerParams(
            dimension_semantics=("arbitrary",), vmem_limit_bytes=VMEM_LIMIT),
        name="proj_conv",
    )(x2, g, wt_mlstm, wt_cv, wg_t, bg, conv_w, conv_b, ln_g, ln_b)


def _cumsum_lanes(x):
    rows, n = x.shape
    tri = (lax.broadcasted_iota(jnp.int32, (n, n), 0)
           <= lax.broadcasted_iota(jnp.int32, (n, n), 1)).astype(bf16)
    hi = x.astype(bf16).astype(f32)
    mid = (x - hi).astype(bf16).astype(f32)
    lo = x - hi - mid
    parts = jnp.dot(jnp.concatenate([hi, mid, lo], axis=0).astype(bf16), tri, preferred_element_type=f32)
    return parts[0:rows] + parts[rows:2 * rows] + parts[2 * rows:3 * rows]


def _mlstm_out_kernel(q_ref, k_ref, v_ref, o_ref, z_ref, gates_ref, mhg_ref,
                      x_ref, cc_ref, wa_ref, wc_ref, fg_ref, out_ref,
                      c_ref, n_ref, m_ref, hm_ref, *, head_dim, chunks_per_seq, final_norm):
    L = q_ref.shape[0]
    scale = head_dim ** -0.5
    step = pl.program_id(0)
    wr = step % 2
    rd = 1 - wr

    @pl.when(step == 0)
    def _():
        hm_ref[...] = jnp.zeros_like(hm_ref)

    @pl.when(step % chunks_per_seq == 0)
    def _():
        c_ref[...] = jnp.zeros_like(c_ref)
        n_ref[...] = jnp.zeros_like(n_ref)
        m_ref[...] = jnp.full_like(m_ref, M_INIT)

    gr = gates_ref[...]
    lf = jnp.minimum(gr, 0.0) - jnp.log1p(jnp.exp(-jnp.abs(gr)))
    bfull = _cumsum_lanes(lf)
    ig = gr[0:N_HEADS]
    b = bfull[N_HEADS:2 * N_HEADS]
    b_last = b[:, L - 1:L]
    m_prev = m_ref[0:N_HEADS, 0:1]
    a = b_last - b + ig
    a_max = jnp.max(a, axis=1, keepdims=True)
    m_new = jnp.maximum(b_last + m_prev, a_max)
    s_old = jnp.exp(b_last + m_prev - m_new)
    w_a = jnp.exp(a - m_new)
    c_row = ig - b

    cols = jnp.transpose(jnp.concatenate([b, w_a], axis=0))

    row_id = lax.broadcasted_iota(jnp.int32, (L, L), 0)
    col_id = lax.broadcasted_iota(jnp.int32, (L, L), 1)
    causal = col_id <= row_id

    d_out = out_ref.shape[1]
    tn = d_out // N_HEADS

    heads = range(N_HEADS)
    hs = [slice(h * head_dim, (h + 1) * head_dim) for h in heads]
    ys = []

    def out_proj_part(j):
        ol = slice(j * tn, (j + 1) * tn)
        ys.append(jnp.dot(hm_ref[rd], wa_ref[:, ol], preferred_element_type=f32)
                  + jnp.dot(cc_ref[...], wc_ref[:, ol], preferred_element_type=f32))

    s_ = [lax.dot_general(q_ref[:, hs[h]], k_ref[:, hs[h]], _NT, preferred_element_type=f32) * scale
          for h in heads]
    c_prev = [c_ref[h] for h in heads]
    qc = [jnp.dot(q_ref[:, hs[h]], c_prev[h].astype(bf16), preferred_element_type=f32) for h in heads]

    p_, m_t_, s_inter_ = [], [], []
    for h in heads:
        b_col = cols[:, h:h + 1]
        g = jnp.where(causal, b_col + c_row[h:h + 1, :], -jnp.inf)
        li = b_col + m_prev[h:h + 1, :]
        m_t = jnp.maximum(li, jnp.max(g, axis=1, keepdims=True))
        p_.append(jnp.exp(g - m_t) * s_[h])
        m_t_.append(m_t)
        s_inter_.append(jnp.exp(li - m_t))

    pv = [jnp.dot(p_[h].astype(bf16), v_ref[:, hs[h]], preferred_element_type=f32) for h in heads]
    out_proj_part(0)
    out_proj_part(1)

    hm_new, vw = [], []
    for h in heads:
        n_prev = n_ref[h:h + 1, :]
        num = pv[h] + s_inter_[h] * qc[h]
        den = (jnp.sum(p_[h], axis=1, keepdims=True)
               + s_inter_[h] * jnp.sum(q_ref[:, hs[h]].astype(f32) * n_prev, axis=1, keepdims=True))
        hh = num * (1.0 / jnp.maximum(jnp.abs(den), jnp.exp(-m_t_[h])))
        hh = _sigmoid(o_ref[:, hs[h]].astype(f32)) * hh
        hh = hh * lax.rsqrt(jnp.mean(hh * hh, axis=-1, keepdims=True) + EPS)
        hh = hh * mhg_ref[:, hs[h]]
        hm_new.append((hh * _silu(z_ref[:, hs[h]].astype(f32))).astype(bf16))
        w_col = cols[:, N_HEADS + h:N_HEADS + h + 1]
        vw.append((v_ref[:, hs[h]].astype(f32) * w_col).astype(bf16))

    out_proj_part(2)
    kv = [lax.dot_general(k_ref[:, hs[h]], vw[h], _TN, preferred_element_type=f32) * scale
          for h in heads]
    out_proj_part(3)
    n_all = jnp.dot(jnp.concatenate([w_a, w_a], axis=0).astype(bf16), k_ref[...],
                    preferred_element_type=f32) * scale
    c_new =[s_old[h:h + 1, :] * c_prev[h] + kv[h] for h in heads]
    n_new = [s_old[h:h + 1, :] * n_ref[h:h + 1, :] + n_all[h:h + 1, hs[h]] for h in heads]

    y = x_ref[...] + jnp.concatenate(ys, axis=1)
    if final_norm:
        y = y * lax.rsqrt(jnp.mean(y * y, axis=-1, keepdims=True) + EPS) * fg_ref[...]
    out_ref[...] = y
    for h in heads:
        hm_ref[wr, :, hs[h]] = hm_new[h]
        c_ref[h] = c_new[h]
        n_ref[h:h + 1, :] = n_new[h]
    m_ref[0:N_HEADS, :] = jnp.broadcast_to(m_new, (N_HEADS, m_ref.shape[1]))


def _mlstm_out(proj, gates, mhg, x2, cc, w_a, w_c, fg, *, seq, d_mlstm, final_norm):
    t, d = x2.shape
    L = MLSTM_CHUNK
    nc = seq // L
    n = t // L
    last = n - 1
    hd = d_mlstm // N_HEADS
    assert d % N_HEADS == 0 and seq % L == 0

    def col(j):
        return pl.BlockSpec((L, d_mlstm), lambda s, j=j: (jnp.minimum(s, last), j))

    prev = lambda s: (jnp.maximum(s - 1, 0), 0)
    const = lambda s: (0, 0)
    return pl.pallas_call(
        functools.partial(_mlstm_out_kernel, head_dim=hd, chunks_per_seq=nc, final_norm=final_norm),
        grid=(n + 1,),
        in_specs=[col(0), col(1), col(2), col(3), col(4),
                  pl.BlockSpec((2 * N_HEADS, L), lambda s: (0, jnp.minimum(s, last))),
                  pl.BlockSpec((1, d_mlstm), const),
                  pl.BlockSpec((L, d), prev),
                  pl.BlockSpec((L, cc.shape[1]), prev),
                  pl.BlockSpec(w_a.shape, const, pipeline_mode=pl.Buffered(1)),
                  pl.BlockSpec(w_c.shape, const, pipeline_mode=pl.Buffered(1)),
                  pl.BlockSpec((1, d), const)],
        out_specs=pl.BlockSpec((L, d), prev),
        out_shape=jax.ShapeDtypeStruct((t, d), f32),
        scratch_shapes=[pltpu.VMEM((N_HEADS, hd, hd), f32),
                        pltpu.VMEM((SUBLANES, hd), f32),
                        pltpu.VMEM((SUBLANES, LANES), f32),
                        pltpu.VMEM((2, L, d_mlstm), bf16)],
        compiler_params=pltpu.CompilerParams(
            dimension_semantics=("arbitrary",), vmem_limit_bytes=VMEM_LIMIT),
        name="mlstm_out",
    )(proj, proj, proj, proj, proj, gates, mhg, x2, cc, w_a, w_c, fg)


def kernel(x, norm_g, w_in, b_gates, mh_norm_g, conv_w, conv_b, conv_ln_g, conv_ln_b, w_out, final_norm_g):
    batch, seq, d_model = x.shape
    depth = norm_g.shape[0]
    d_mlstm = mh_norm_g.shape[1]
    d_conv = conv_b.shape[1]
    n_gate = 2 * N_HEADS
    g0 = 5 * d_mlstm

    h = x.reshape(batch * seq, d_model)
    for l in range(depth):
        w_t = jnp.swapaxes(w_in[l], 0, 1)
        wt_mlstm = w_t[:g0].astype(bf16)
        wt_cv = w_t[g0 + n_gate:].astype(bf16)
        wg_t = w_t[g0:g0 + n_gate].astype(bf16)
        proj, gates, c = _proj_conv(h, norm_g[l][None, :], wt_mlstm, wt_cv, wg_t, b_gates[l][:, None],
                                    conv_w[l], conv_b[l][None, :], conv_ln_g[l][None, :],
                                    conv_ln_b[l][None, :], seq=seq, d_conv=d_conv)
        wo = w_out[l].astype(bf16)
        h = _mlstm_out(proj, gates, mh_norm_g[l][None, :], h, c, wo[:d_mlstm], wo[d_mlstm:],
                       final_norm_g[None, :], seq=seq, d_mlstm=d_mlstm, final_norm=(l == depth - 1))
    return h.reshape(batch, seq, d_model)
```

```python
import functools

import jax
import jax.numpy as jnp
from jax import lax
from jax.experimental import pallas as pl
from jax.experimental.pallas import tpu as pltpu

N_HEADS = 4
CONV_WIDTH = 31
EPS = 1e-6
M_INIT = -1e30

MLSTM_CHUNK = 256
LANES = 128
SUBLANES = 8
CONV_HALO = 32
VMEM_LIMIT = 56 * 1024 * 1024

f32 = jnp.float32
bf16 = jnp.bfloat16

_NT = (((1,), (1,)), ((), ()))
_TN = (((0,), (0,)), ((), ()))


def _zero_bits(x):
    b = pltpu.bitcast(x, jnp.uint32)
    return lax.shift_right_logical(lax.shift_right_logical(b, jnp.uint32(16)), jnp.uint32(16))


def _sigmoid(x):
    return 0.5 * jnp.tanh(0.5 * x) + 0.5


def _silu(x):
    hx = 0.5 * x
    return hx * jnp.tanh(hx) + hx


def _conv_rows(x, w_ref, lanes, rb, dep):
    acc = None
    for r in range(SUBLANES):
        nq = (CONV_WIDTH - 1 - r) // SUBLANES + 1
        lo = 0 if r == 0 else SUBLANES
        p = None
        for q in range(nq):
            j = CONV_WIDTH - 1 - (SUBLANES * q + r)
            start = CONV_HALO - lo - SUBLANES * q
            wj = w_ref[j:j + 1, lanes]
            if dep is not None:
                wj = pltpu.bitcast(pltpu.bitcast(wj, jnp.uint32) | dep, f32)
            term = wj * x[start:start + rb + lo, :]
            p = term if p is None else p + term
        acc = p if r == 0 else acc + p[SUBLANES - r:SUBLANES - r + rb, :]
        dep = _zero_bits(acc[0:1, :])
    return acc, dep


def _proj_conv_kernel(x_ref, g_ref, wm_ref, wc_ref, wg_ref, bg_ref, cw_ref, cb_ref, lng_ref, lnb_ref,
                      proj_ref, gates_ref, c_ref,
                      u_ref, ag_ref, z_ref, buf_ref, acc_ref, *, tiles_per_seq, tn, rb):
    i = pl.program_id(0)
    tm, d = x_ref.shape
    dc = c_ref.shape[1]
    halo = CONV_HALO

    @pl.when(i == 0)
    def _():
        z_ref[...] = jnp.zeros_like(z_ref)
        buf_ref[...] = jnp.zeros_like(buf_ref)

    x = x_ref[...]
    u = x * lax.rsqrt(jnp.mean(x * x, axis=-1, keepdims=True) + EPS) * g_ref[...]
    u_ref[...] = u.astype(bf16)
    gates_ref[...] = lax.dot_general(wg_ref[...], u_ref[...], _NT, preferred_element_type=f32) + bg_ref[...]

    dm = wm_ref.shape[0]

    def matmul(col):
        w = wm_ref[col:col + tn, :] if col < dm else wc_ref[col - dm:col - dm + tn, :]
        return lax.dot_general(u_ref[...], w, _NT, preferred_element_type=f32).astype(bf16)

    def glu_item(r0, n, with_history):
        def compute():
            half = jnp.full((1, dc), 0.5, f32)
            ys = []
            for p0 in range(r0, r0 + n, rb):
                a = ag_ref[p0:p0 + rb, 0:dc].astype(f32)
                g = ag_ref[p0:p0 + rb, dc:2 * dc].astype(f32)
                y = a * (half * jnp.tanh(half * g) + half)
                half = pltpu.bitcast(pltpu.bitcast(half, jnp.uint32) | _zero_bits(y[0:1, :]), f32)
                ys.append(y)
            if with_history:
                tail = buf_ref[tm:tm + halo, :]
                ys.append(jnp.where(i % tiles_per_seq == 0, jnp.zeros_like(tail), tail))
            return ys

        def store(ys):
            for j in range(n // rb):
                buf_ref[halo + r0 + j * rb:halo + r0 + (j + 1) * rb, :] = ys[j]
            if with_history:
                buf_ref[0:halo, :] = ys[-1]
        return compute, store

    def conv_item(units):
        def compute():
            ys, dep = [], None
            for t0, l0 in units:
                x = buf_ref[t0:t0 + halo + rb, l0:l0 + LANES]
                y, dep = _conv_rows(x, cw_ref, slice(l0, l0 + LANES), rb, dep)
                ys.append(y + cb_ref[:, l0:l0 + LANES])
            return ys

        def store(ys):
            for y, (t0, l0) in zip(ys, units):
                acc_ref[t0:t0 + rb, l0:l0 + LANES] = y
        return compute, store

    def ln_item(r0, n):
        def compute():
            outs = []
            gain = lng_ref[...]
            for p0 in range(r0, r0 + n, rb):
                y = acc_ref[p0:p0 + rb, :]
                mu = jnp.mean(y, axis=-1, keepdims=True)
                yc = y - mu
                var = jnp.mean(yc * yc, axis=-1, keepdims=True)
                yn = yc * lax.rsqrt(var + EPS) * gain + lnb_ref[...]
                out = _silu(yn) * _silu(z_ref[p0:p0 + rb, :].astype(f32))
                gain = pltpu.bitcast(pltpu.bitcast(gain, jnp.uint32) | _zero_bits(out[0:1, :]), f32)
                outs.append(out.astype(c_ref.dtype))
            return outs

        def store(outs):
            for j, out in enumerate(outs):
                c_ref[r0 + j * rb:r0 + (j + 1) * rb, :] = out
        return compute, store

    def to_proj(col, val):
        proj_ref[:, col:col + tn] = val

    def to_ag(col, val):
        ag_ref[:, col - a0:col - a0 + tn] = val

    def to_z(col, val):
        z_ref[:, col - zc0:col - zc0 + tn] = val

    q0, k0, v0, o0, zm0, a0, g0, zc0 = (j * d for j in range(8))
    assert tn == d and tm == 8 * rb
    mm_order = [(to_proj, q0), (to_proj, k0), (to_ag, a0), (to_ag, g0),
                (to_proj, v0), (to_proj, o0), (to_proj, zm0), (to_z, zc0)]
    units = [(t0, l0) for t0 in range(0, tm, rb) for l0 in range(0, dc, LANES)]
    n_units = (11, 11, 11, 11, 10, 10)
    cuts = [sum(n_units[:j]) for j in range(len(n_units) + 1)]
    assert cuts[-1] == len(units)
    glu_rows = tm // 4
    work = [[conv_item(units[cuts[j]:cuts[j + 1]])] for j in range(4)]
    work += [[conv_item(units[cuts[4]:cuts[5]]), glu_item(0, glu_rows, True)],
             [conv_item(units[cuts[5]:cuts[6]]), glu_item(glu_rows, glu_rows, False)],
             [ln_item(0, tm // 2), glu_item(2 * glu_rows, glu_rows, False)],
             [ln_item(tm // 2, tm // 2), glu_item(3 * glu_rows, glu_rows, False)]]

    once = jnp.minimum(i, 0) + 1

    for (mm_store, col), items in zip(mm_order, work):
        def region(_, carry, mm_store=mm_store, col=col, items=items):
            outs = [compute() for compute, _ in items]
            res = matmul(col)
            mm_store(col, res)
            for (_, store), out in zip(items, outs):
                store(out)
            return carry
        lax.fori_loop(0, once, region, 0)


def _proj_conv(x2, g, wt_mlstm, wt_cv, wg_t, bg, conv_w, conv_b, ln_g, ln_b, *, seq, d_conv,
               tm=512, tn=1024, rb=64):
    t, d = x2.shape
    dm = wt_mlstm.shape[0]
    assert (d == d_conv and dm == 5 * d and wt_cv.shape[0] == 3 * d_conv and d % tn == 0
            and seq % tm == 0 and tm % rb == 0)
    n = t // tm
    last = n - 1
    const = lambda i: (0, 0)
    vec = pl.BlockSpec((1, d_conv), const)
    return pl.pallas_call(
        functools.partial(_proj_conv_kernel, tiles_per_seq=seq // tm, tn=tn, rb=rb),
        grid=(n + 1,),
        in_specs=[
            pl.BlockSpec((tm, d), lambda i: (jnp.minimum(i, last), 0)),
            pl.BlockSpec((1, d), const),
            pl.BlockSpec(wt_mlstm.shape, const, pipeline_mode=pl.Buffered(1)),
            pl.BlockSpec(wt_cv.shape, const, pipeline_mode=pl.Buffered(1)),
            pl.BlockSpec((2 * N_HEADS, d), const),
            pl.BlockSpec((2 * N_HEADS, 1), const),
            pl.BlockSpec((CONV_WIDTH, d_conv), const),
            vec, vec, vec,
        ],
        out_specs=[
            pl.BlockSpec((tm, dm), lambda i: (jnp.minimum(i, last), 0)),
            pl.BlockSpec((2 * N_HEADS, tm), lambda i: (0, jnp.minimum(i, last))),
            pl.BlockSpec((tm, d_conv), lambda i: (jnp.maximum(i - 1, 0), 0)),
        ],
        out_shape=[
            jax.ShapeDtypeStruct((t, dm), bf16),
            jax.ShapeDtypeStruct((2 * N_HEADS, t), f32),
            jax.ShapeDtypeStruct((t, d_conv), bf16),
        ],
        scratch_shapes=[
            pltpu.VMEM((tm, d), bf16),
            pltpu.VMEM((tm, 2 * d_conv), bf16),
            pltpu.VMEM((tm, d_conv), bf16),
            pltpu.VMEM((tm + CONV_HALO, d_conv), f32),
            pltpu.VMEM((tm, d_conv), f32),
        ],
        compiler_params=pltpu.CompilerParams(
            dimension_semantics=("arbitrary",), vmem_limit_bytes=VMEM_LIMIT),
        name="proj_conv",
    )(x2, g, wt_mlstm, wt_cv, wg_t, bg, conv_w, conv_b, ln_g, ln_b)


def _cumsum_lanes(x):
    rows, n = x.shape
    tri = (lax.broadcasted_iota(jnp.int32, (n, n), 0)
           <= lax.broadcasted_iota(jnp.int32, (n, n), 1)).astype(bf16)
    hi = x.astype(bf16).astype(f32)
    mid = (x - hi).astype(bf16).astype(f32)
    lo = x - hi - mid
    parts = jnp.dot(jnp.concatenate([hi, mid, lo], axis=0).astype(bf16), tri, preferred_element_type=f32)
    return parts[0:rows] + parts[rows:2 * rows] + parts[2 * rows:3 * rows]


def _mlstm_out_kernel(q_ref, k_ref, v_ref, o_ref, z_ref, gates_ref, mhg_ref,
                      x_ref, cc_ref, wa_ref, wc_ref, fg_ref, out_ref,
                      c_ref, n_ref, m_ref, hm_ref, *, head_dim, chunks_per_seq, final_norm):
    L = q_ref.shape[0]
    scale = head_dim ** -0.5
    step = pl.program_id(0)
    wr = step % 2
    rd = 1 - wr

    @pl.when(step == 0)
    def _():
        hm_ref[...] = jnp.zeros_like(hm_ref)

    @pl.when(step % chunks_per_seq == 0)
    def _():
        c_ref[...] = jnp.zeros_like(c_ref)
        n_ref[...] = jnp.zeros_like(n_ref)
        m_ref[...] = jnp.full_like(m_ref, M_INIT)

    gr = gates_ref[...]
    lf = jnp.minimum(gr, 0.0) - jnp.log1p(jnp.exp(-jnp.abs(gr)))
    bfull = _cumsum_lanes(lf)
    ig = gr[0:N_HEADS]
    b = bfull[N_HEADS:2 * N_HEADS]
    b_last = b[:, L - 1:L]
    m_prev = m_ref[0:N_HEADS, 0:1]
    a = b_last - b + ig
    a_max = jnp.max(a, axis=1, keepdims=True)
    m_new = jnp.maximum(b_last + m_prev, a_max)
    s_old = jnp.exp(b_last + m_prev - m_new)
    w_a = jnp.exp(a - m_new)
    c_row = ig - b

    cols = jnp.transpose(jnp.concatenate([b, w_a], axis=0))

    row_id = lax.broadcasted_iota(jnp.int32, (L, L), 0)
    col_id = lax.broadcasted_iota(jnp.int32, (L, L), 1)
    causal = col_id <= row_id

    d_out = out_ref.shape[1]
    tn = d_out // N_HEADS

    heads = range(N_HEADS)
    hs = [slice(h * head_dim, (h + 1) * head_dim) for h in heads]
    ys = []

    def out_proj_part(j):
        ol = slice(j * tn, (j + 1) * tn)
        ys.append(jnp.dot(hm_ref[rd], wa_ref[:, ol], preferred_element_type=f32)
                  + jnp.dot(cc_ref[...], wc_ref[:, ol], preferred_element_type=f32))

    o_gate = [_sigmoid(o_ref[:, hs[h]].astype(f32)) for h in heads]
    z_gate = [_silu(z_ref[:, hs[h]].astype(f32)) for h in heads]
    qn = [jnp.sum(q_ref[:, hs[h]].astype(f32) * n_ref[h:h + 1, :], axis=1, keepdims=True) for h in heads]

    s_ = [lax.dot_general(q_ref[:, hs[h]], k_ref[:, hs[h]], _NT, preferred_element_type=f32) * scale
          for h in heads]
    c_prev = [c_ref[h] for h in heads]
    qc = [jnp.dot(q_ref[:, hs[h]], c_prev[h].astype(bf16), preferred_element_type=f32) for h in heads]

    p_, m_t_, s_inter_ = [], [], []
    for h in heads:
        b_col = cols[:, h:h + 1]
        g = jnp.where(causal, b_col + c_row[h:h + 1, :], -jnp.inf)
        li = b_col + m_prev[h:h + 1, :]
        m_t = jnp.maximum(li, jnp.max(g, axis=1, keepdims=True))
        p_.append(jnp.exp(g - m_t) * s_[h])
        m_t_.append(m_t)
        s_inter_.append(jnp.exp(li - m_t))

    pv = [jnp.dot(p_[h].astype(bf16), v_ref[:, hs[h]], preferred_element_type=f32) for h in heads]
    out_proj_part(0)
    out_proj_part(1)

    hm_new, vw = [], []
    for h in heads:
        num = pv[h] + s_inter_[h] * qc[h]
        den = jnp.sum(p_[h], axis=1, keepdims=True) + s_inter_[h] * qn[h]
        hh = num * (1.0 / jnp.maximum(jnp.abs(den), jnp.exp(-m_t_[h])))
        hh = o_gate[h] * hh
        hh = hh * lax.rsqrt(jnp.mean(hh * hh, axis=-1, keepdims=True) + EPS)
        hh = hh * mhg_ref[:, hs[h]]
        hm_new.append((hh * z_gate[h]).astype(bf16))
        w_col = cols[:, N_HEADS + h:N_HEADS + h + 1]
        vw.append((v_ref[:, hs[h]].astype(f32) * w_col).astype(bf16))

    out_proj_part(2)
    kv = [lax.dot_general(k_ref[:, hs[h]], vw[h], _TN, preferred_element_type=f32) * scale
          for h in heads]
    out_proj_part(3)
    n_all = jnp.dot(jnp.concatenate([w_a, w_a], axis=0).astype(bf16), k_ref[...],
                    preferred_element_type=f32) * scale
    c_new =[s_old[h:h + 1, :] * c_prev[h] + kv[h] for h in heads]
    n_new = [s_old[h:h + 1, :] * n_ref[h:h + 1, :] + n_all[h:h + 1, hs[h]] for h in heads]

    y = x_ref[...] + jnp.concatenate(ys, axis=1)
    if final_norm:
        y = y * lax.rsqrt(jnp.mean(y * y, axis=-1, keepdims=True) + EPS) * fg_ref[...]
    out_ref[...] = y
    for h in heads:
        hm_ref[wr, :, hs[h]] = hm_new[h]
        c_ref[h] = c_new[h]
        n_ref[h:h + 1, :] = n_new[h]
    m_ref[0:N_HEADS, :] = jnp.broadcast_to(m_new, (N_HEADS, m_ref.shape[1]))


def _mlstm_out(proj, gates, mhg, x2, cc, w_a, w_c, fg, *, seq, d_mlstm, final_norm):
    t, d = x2.shape
    L = MLSTM_CHUNK
    nc = seq // L
    n = t // L
    last = n - 1
    hd = d_mlstm // N_HEADS
    assert d % N_HEADS == 0 and seq % L == 0

    def col(j):
        return pl.BlockSpec((L, d_mlstm), lambda s, j=j: (jnp.minimum(s, last), j))

    prev = lambda s: (jnp.maximum(s - 1, 0), 0)
    const = lambda s: (0, 0)
    return pl.pallas_call(
        functools.partial(_mlstm_out_kernel, head_dim=hd, chunks_per_seq=nc, final_norm=final_norm),
        grid=(n + 1,),
        in_specs=[col(0), col(1), col(2), col(3), col(4),
                  pl.BlockSpec((2 * N_HEADS, L), lambda s: (0, jnp.minimum(s, last))),
                  pl.BlockSpec((1, d_mlstm), const),
                  pl.BlockSpec((L, d), prev),
                  pl.BlockSpec((L, cc.shape[1]), prev),
                  pl.BlockSpec(w_a.shape, const, pipeline_mode=pl.Buffered(1)),
                  pl.BlockSpec(w_c.shape, const, pipeline_mode=pl.Buffered(1)),
                  pl.BlockSpec((1, d), const)],
        out_specs=pl.BlockSpec((L, d), prev),
        out_shape=jax.ShapeDtypeStruct((t, d), f32),
        scratch_shapes=[pltpu.VMEM((N_HEADS, hd, hd), f32),
                        pltpu.VMEM((SUBLANES, hd), f32),
                        pltpu.VMEM((SUBLANES, LANES), f32),
                        pltpu.VMEM((2, L, d_mlstm), bf16)],
        compiler_params=pltpu.CompilerParams(
            dimension_semantics=("arbitrary",), vmem_limit_bytes=VMEM_LIMIT),
        name="mlstm_out",
    )(proj, proj, proj, proj, proj, gates, mhg, x2, cc, w_a, w_c, fg)


def kernel(x, norm_g, w_in, b_gates, mh_norm_g, conv_w, conv_b, conv_ln_g, conv_ln_b, w_out, final_norm_g):
    batch, seq, d_model = x.shape
    depth = norm_g.shape[0]
    d_mlstm = mh_norm_g.shape[1]
    d_conv = conv_b.shape[1]
    n_gate = 2 * N_HEADS
    g0 = 5 * d_mlstm

    h = x.reshape(batch * seq, d_model)
    for l in range(depth):
        w_t = jnp.swapaxes(w_in[l], 0, 1)
        wt_mlstm = w_t[:g0].astype(bf16)
        wt_cv = w_t[g0 + n_gate:].astype(bf16)
        wg_t = w_t[g0:g0 + n_gate].astype(bf16)
        proj, gates, c = _proj_conv(h, norm_g[l][None, :], wt_mlstm, wt_cv, wg_t, b_gates[l][:, None],
                                    conv_w[l], conv_b[l][None, :], conv_ln_g[l][None, :],
                                    conv_ln_b[l][None, :], seq=seq, d_conv=d_conv)
        wo = w_out[l].astype(bf16)
        h = _mlstm_out(proj, gates, mh_norm_g[l][None, :], h, c, wo[:d_mlstm], wo[d_mlstm:],
                       final_norm_g[None, :], seq=seq, d_mlstm=d_mlstm, final_norm=(l == depth - 1))
    return h.reshape(batch, seq, d_model)
```

```python
import functools

import jax
import jax.numpy as jnp
from jax import lax
from jax.experimental import pallas as pl
from jax.experimental.pallas import tpu as pltpu

N_HEADS = 4
CONV_WIDTH = 31
EPS = 1e-6
M_INIT = -1e30

MLSTM_CHUNK = 256
LANES = 128
SUBLANES = 8
CONV_HALO = 32
VMEM_LIMIT = 56 * 1024 * 1024

f32 = jnp.float32
bf16 = jnp.bfloat16

_NT = (((1,), (1,)), ((), ()))
_TN = (((0,), (0,)), ((), ()))


def _zero_bits(x):
    b = pltpu.bitcast(x, jnp.uint32)
    return lax.shift_right_logical(lax.shift_right_logical(b, jnp.uint32(16)), jnp.uint32(16))


def _sigmoid(x):
    return 0.5 * jnp.tanh(0.5 * x) + 0.5


def _silu(x):
    hx = 0.5 * x
    return hx * jnp.tanh(hx) + hx


def _conv_rows(x, w_ref, lanes, rb, dep):
    acc = None
    for r in range(SUBLANES):
        nq = (CONV_WIDTH - 1 - r) // SUBLANES + 1
        lo = 0 if r == 0 else SUBLANES
        p = None
        for q in range(nq):
            j = CONV_WIDTH - 1 - (SUBLANES * q + r)
            start = CONV_HALO - lo - SUBLANES * q
            wj = w_ref[j:j + 1, lanes]
            if dep is not None:
                wj = pltpu.bitcast(pltpu.bitcast(wj, jnp.uint32) | dep, f32)
            term = wj * x[start:start + rb + lo, :]
            p = term if p is None else p + term
        acc = p if r == 0 else acc + p[SUBLANES - r:SUBLANES - r + rb, :]
        dep = _zero_bits(acc[0:1, :])
    return acc, dep


def _proj_conv_kernel(x_ref, g_ref, wm_ref, wc_ref, wg_ref, bg_ref, cw_ref, cb_ref, lng_ref, lnb_ref,
                      proj_ref, gates_ref, c_ref,
                      u_ref, ag_ref, z_ref, buf_ref, acc_ref, *, tiles_per_seq, tn, rb):
    i = pl.program_id(0)
    tm, d = x_ref.shape
    dc = c_ref.shape[1]
    halo = CONV_HALO

    @pl.when(i == 0)
    def _():
        z_ref[...] = jnp.zeros_like(z_ref)
        buf_ref[...] = jnp.zeros_like(buf_ref)

    x = x_ref[...]
    u = x * lax.rsqrt(jnp.mean(x * x, axis=-1, keepdims=True) + EPS) * g_ref[...]
    u_ref[...] = u.astype(bf16)
    gates_ref[...] = lax.dot_general(wg_ref[...], u_ref[...], _NT, preferred_element_type=f32) + bg_ref[...]

    dm = wm_ref.shape[0]

    def matmul(col):
        w = wm_ref[col:col + tn, :] if col < dm else wc_ref[col - dm:col - dm + tn, :]
        return lax.dot_general(u_ref[...], w, _NT, preferred_element_type=f32).astype(bf16)

    def glu_item(r0, n, with_history):
        def compute():
            half = jnp.full((1, dc), 0.5, f32)
            ys = []
            for p0 in range(r0, r0 + n, rb):
                a = ag_ref[p0:p0 + rb, 0:dc].astype(f32)
                g = ag_ref[p0:p0 + rb, dc:2 * dc].astype(f32)
                y = a * (half * jnp.tanh(half * g) + half)
                half = pltpu.bitcast(pltpu.bitcast(half, jnp.uint32) | _zero_bits(y[0:1, :]), f32)
                ys.append(y)
            if with_history:
                tail = buf_ref[tm:tm + halo, :]
                ys.append(jnp.where(i % tiles_per_seq == 0, jnp.zeros_like(tail), tail))
            return ys

        def store(ys):
            for j in range(n // rb):
                buf_ref[halo + r0 + j * rb:halo + r0 + (j + 1) * rb, :] = ys[j]
            if with_history:
                buf_ref[0:halo, :] = ys[-1]
        return compute, store

    def conv_item(units):
        def compute():
            ys, dep = [], None
            for t0, l0 in units:
                x = buf_ref[t0:t0 + halo + rb, l0:l0 + LANES]
                y, dep = _conv_rows(x, cw_ref, slice(l0, l0 + LANES), rb, dep)
                ys.append(y + cb_ref[:, l0:l0 + LANES])
            return ys

        def store(ys):
            for y, (t0, l0) in zip(ys, units):
                acc_ref[t0:t0 + rb, l0:l0 + LANES] = y
        return compute, store

    def ln_item(r0, n):
        def compute():
            outs = []
            gain = lng_ref[...]
            for p0 in range(r0, r0 + n, rb):
                y = acc_ref[p0:p0 + rb, :]
                mu = jnp.mean(y, axis=-1, keepdims=True)
                yc = y - mu
                var = jnp.mean(yc * yc, axis=-1, keepdims=True)
                yn = yc * lax.rsqrt(var + EPS) * gain + lnb_ref[...]
                out = _silu(yn) * _silu(z_ref[p0:p0 + rb, :].astype(f32))
                gain = pltpu.bitcast(pltpu.bitcast(gain, jnp.uint32) | _zero_bits(out[0:1, :]), f32)
                outs.append(out.astype(c_ref.dtype))
            return outs

        def store(outs):
            for j, out in enumerate(outs):
                c_ref[r0 + j * rb:r0 + (j + 1) * rb, :] = out
        return compute, store

    def to_proj(col, val):
        proj_ref[:, col:col + tn] = val

    def to_ag(col, val):
        ag_ref[:, col - a0:col - a0 + tn] = val

    def to_z(col, val):
        z_ref[:, col - zc0:col - zc0 + tn] = val

    q0, k0, v0, o0, zm0, a0, g0, zc0 = (j * d for j in range(8))
    assert tn == d and tm == 8 * rb
    mm_order = [(to_proj, q0), (to_proj, k0), (to_ag, a0), (to_ag, g0),
                (to_proj, v0), (to_proj, o0), (to_proj, zm0), (to_z, zc0)]
    units = [(t0, l0) for t0 in range(0, tm, rb) for l0 in range(0, dc, LANES)]
    n_units = (11, 9, 9, 9, 9, 9, 8)
    cuts = [sum(n_units[:j]) for j in range(len(n_units) + 1)]
    assert cuts[-1] == len(units)
    per_row_block = dc // LANES
    work = []
    for j in range(len(n_units)):
        items = [conv_item(units[cuts[j]:cuts[j + 1]])]
        if j >= 1:
            assert cuts[j] >= j * per_row_block
            items.append(ln_item((j - 1) * rb, rb))
        work.append(items)
    done = len(n_units) - 1
    assert cuts[done + 1] >= min(((tm // 2) // rb + 1) * per_row_block, len(units))
    work[done].append(glu_item(0, tm // 2, True))
    work.append([ln_item(done * rb, tm - done * rb), glu_item(tm // 2, tm // 2, False)])

    once = jnp.minimum(i, 0) + 1

    for (mm_store, col), items in zip(mm_order, work):
        def region(_, carry, mm_store=mm_store, col=col, items=items):
            outs = [compute() for compute, _ in items]
            res = matmul(col)
            mm_store(col, res)
            for (_, store), out in zip(items, outs):
                store(out)
            return carry
        lax.fori_loop(0, once, region, 0)


def _proj_conv(x2, g, wt_mlstm, wt_cv, wg_t, bg, conv_w, conv_b, ln_g, ln_b, *, seq, d_conv,
               tm=512, tn=1024, rb=64):
    t, d = x2.shape
    dm = wt_mlstm.shape[0]
    assert (d == d_conv and dm == 5 * d and wt_cv.shape[0] == 3 * d_conv and d % tn == 0
            and seq % tm == 0 and tm % rb == 0)
    n = t // tm
    last = n - 1
    const = lambda i: (0, 0)
    vec = pl.BlockSpec((1, d_conv), const)
    return pl.pallas_call(
        functools.partial(_proj_conv_kernel, tiles_per_seq=seq // tm, tn=tn, rb=rb),
        grid=(n + 1,),
        in_specs=[
            pl.BlockSpec((tm, d), lambda i: (jnp.minimum(i, last), 0)),
            pl.BlockSpec((1, d), const),
            pl.BlockSpec(wt_mlstm.shape, const, pipeline_mode=pl.Buffered(1)),
            pl.BlockSpec(wt_cv.shape, const, pipeline_mode=pl.Buffered(1)),
            pl.BlockSpec((2 * N_HEADS, d), const),
            pl.BlockSpec((2 * N_HEADS, 1), const),
            pl.BlockSpec((CONV_WIDTH, d_conv), const),
            vec, vec, vec,
        ],
        out_specs=[
            pl.BlockSpec((tm, dm), lambda i: (jnp.minimum(i, last), 0)),
            pl.BlockSpec((2 * N_HEADS, tm), lambda i: (0, jnp.minimum(i, last))),
            pl.BlockSpec((tm, d_conv), lambda i: (jnp.maximum(i - 1, 0), 0)),
        ],
        out_shape=[
            jax.ShapeDtypeStruct((t, dm), bf16),
            jax.ShapeDtypeStruct((2 * N_HEADS, t), f32),
            jax.ShapeDtypeStruct((t, d_conv), bf16),
        ],
        scratch_shapes=[
            pltpu.VMEM((tm, d), bf16),
            pltpu.VMEM((tm, 2 * d_conv), bf16),
            pltpu.VMEM((tm, d_conv), bf16),
            pltpu.VMEM((tm + CONV_HALO, d_conv), f32),
            pltpu.VMEM((tm, d_conv), f32),
        ],
        compiler_params=pltpu.CompilerParams(
            dimension_semantics=("arbitrary",), vmem_limit_bytes=VMEM_LIMIT),
        name="proj_conv",
    )(x2, g, wt_mlstm, wt_cv, wg_t, bg, conv_w, conv_b, ln_g, ln_b)


def _cumsum_lanes(x):
    rows, n = x.shape
    tri = (lax.broadcasted_iota(jnp.int32, (n, n), 0)
           <= lax.broadcasted_iota(jnp.int32, (n, n), 1)).astype(bf16)
    hi = x.astype(bf16).astype(f32)
    mid = (x - hi).astype(bf16).astype(f32)
    lo = x - hi - mid
    parts = jnp.dot(jnp.concatenate([hi, mid, lo], axis=0).astype(bf16), tri, preferred_element_type=f32)
    return parts[0:rows] + parts[rows:2 * rows] + parts[2 * rows:3 * rows]


def _mlstm_out_kernel(q_ref, k_ref, v_ref, o_ref, z_ref, gates_ref, mhg_ref,
                      x_ref, cc_ref, wa_ref, wc_ref, fg_ref, out_ref,
                      c_ref, n_ref, m_ref, hm_ref, *, head_dim, chunks_per_seq, final_norm):
    L = q_ref.shape[0]
    scale = head_dim ** -0.5
    step = pl.program_id(0)
    wr = step % 2
    rd = 1 - wr

    @pl.when(step == 0)
    def _():
        hm_ref[...] = jnp.zeros_like(hm_ref)

    @pl.when(step % chunks_per_seq == 0)
    def _():
        c_ref[...] = jnp.zeros_like(c_ref)
        n_ref[...] = jnp.zeros_like(n_ref)
        m_ref[...] = jnp.full_like(m_ref, M_INIT)

    gr = gates_ref[...]
    lf = jnp.minimum(gr, 0.0) - jnp.log1p(jnp.exp(-jnp.abs(gr)))
    bfull = _cumsum_lanes(lf)
    ig = gr[0:N_HEADS]
    b = bfull[N_HEADS:2 * N_HEADS]
    b_last = b[:, L - 1:L]
    m_prev = m_ref[0:N_HEADS, 0:1]
    a = b_last - b + ig
    a_max = jnp.max(a, axis=1, keepdims=True)
    m_new = jnp.maximum(b_last + m_prev, a_max)
    s_old = jnp.exp(b_last + m_prev - m_new)
    w_a = jnp.exp(a - m_new)
    c_row = ig - b

    cols = jnp.transpose(jnp.concatenate([b, w_a], axis=0))

    row_id = lax.broadcasted_iota(jnp.int32, (L, L), 0)
    col_id = lax.broadcasted_iota(jnp.int32, (L, L), 1)
    causal = col_id <= row_id

    d_out = out_ref.shape[1]
    tn = d_out // N_HEADS

    heads = range(N_HEADS)
    hs = [slice(h * head_dim, (h + 1) * head_dim) for h in heads]
    ys = []

    def out_proj_part(j):
        ol = slice(j * tn, (j + 1) * tn)
        ys.append(jnp.dot(hm_ref[rd], wa_ref[:, ol], preferred_element_type=f32)
                  + jnp.dot(cc_ref[...], wc_ref[:, ol], preferred_element_type=f32))

    o_gate = [_sigmoid(o_ref[:, hs[h]].astype(f32)) for h in heads]
    z_gate = [_silu(z_ref[:, hs[h]].astype(f32)) for h in heads]
    qn = [jnp.sum(q_ref[:, hs[h]].astype(f32) * n_ref[h:h + 1, :], axis=1, keepdims=True) for h in heads]

    s_ = [lax.dot_general(q_ref[:, hs[h]], k_ref[:, hs[h]], _NT, preferred_element_type=f32) * scale
          for h in heads]
    c_prev = [c_ref[h] for h in heads]
    qc = [jnp.dot(q_ref[:, hs[h]], c_prev[h].astype(bf16), preferred_element_type=f32) for h in heads]

    p_, m_t_, s_inter_ = [], [], []
    for h in heads:
        b_col = cols[:, h:h + 1]
        g = jnp.where(causal, b_col + c_row[h:h + 1, :], -jnp.inf)
        li = b_col + m_prev[h:h + 1, :]
        m_t = jnp.maximum(li, jnp.max(g, axis=1, keepdims=True))
        p_.append(jnp.exp(g - m_t) * s_[h])
        m_t_.append(m_t)
        s_inter_.append(jnp.exp(li - m_t))

    pv = [jnp.dot(p_[h].astype(bf16), v_ref[:, hs[h]], preferred_element_type=f32) for h in heads]
    out_proj_part(0)
    out_proj_part(1)

    hm_new, vw = [], []
    for h in heads:
        num = pv[h] + s_inter_[h] * qc[h]
        den = jnp.sum(p_[h], axis=1, keepdims=True) + s_inter_[h] * qn[h]
        hh = num * (1.0 / jnp.maximum(jnp.abs(den), jnp.exp(-m_t_[h])))
        hh = o_gate[h] * hh
        hh = hh * lax.rsqrt(jnp.mean(hh * hh, axis=-1, keepdims=True) + EPS)
        hh = hh * mhg_ref[:, hs[h]]
        hm_new.append((hh * z_gate[h]).astype(bf16))
        w_col = cols[:, N_HEADS + h:N_HEADS + h + 1]
        vw.append((v_ref[:, hs[h]].astype(f32) * w_col).astype(bf16))

    out_proj_part(2)
    kv = [lax.dot_general(k_ref[:, hs[h]], vw[h], _TN, preferred_element_type=f32) * scale
          for h in heads]
    out_proj_part(3)
    n_all = jnp.dot(jnp.concatenate([w_a, w_a], axis=0).astype(bf16), k_ref[...],
                    preferred_element_type=f32) * scale
    c_new =[s_old[h:h + 1, :] * c_prev[h] + kv[h] for h in heads]
    n_new = [s_old[h:h + 1, :] * n_ref[h:h + 1, :] + n_all[h:h + 1, hs[h]] for h in heads]

    y = x_ref[...] + jnp.concatenate(ys, axis=1)
    if final_norm:
        y = y * lax.rsqrt(jnp.mean(y * y, axis=-1, keepdims=True) + EPS) * fg_ref[...]
    out_ref[...] = y
    for h in heads:
        hm_ref[wr, :, hs[h]] = hm_new[h]
        c_ref[h] = c_new[h]
        n_ref[h:h + 1, :] = n_new[h]
    m_ref[0:N_HEADS, :] = jnp.broadcast_to(m_new, (N_HEADS, m_ref.shape[1]))


def _mlstm_out(proj, gates, mhg, x2, cc, w_a, w_c, fg, *, seq, d_mlstm, final_norm):
    t, d = x2.shape
    L = MLSTM_CHUNK
    nc = seq // L
    n = t // L
    last = n - 1
    hd = d_mlstm // N_HEADS
    assert d % N_HEADS == 0 and seq % L == 0

    def col(j):
        return pl.BlockSpec((L, d_mlstm), lambda s, j=j: (jnp.minimum(s, last), j))

    prev = lambda s: (jnp.maximum(s - 1, 0), 0)
    const = lambda s: (0, 0)
    return pl.pallas_call(
        functools.partial(_mlstm_out_kernel, head_dim=hd, chunks_per_seq=nc, final_norm=final_norm),
        grid=(n + 1,),
        in_specs=[col(0), col(1), col(2), col(3), col(4),
                  pl.BlockSpec((2 * N_HEADS, L), lambda s: (0, jnp.minimum(s, last))),
                  pl.BlockSpec((1, d_mlstm), const),
                  pl.BlockSpec((L, d), prev),
                  pl.BlockSpec((L, cc.shape[1]), prev),
                  pl.BlockSpec(w_a.shape, const, pipeline_mode=pl.Buffered(1)),
                  pl.BlockSpec(w_c.shape, const, pipeline_mode=pl.Buffered(1)),
                  pl.BlockSpec((1, d), const)],
        out_specs=pl.BlockSpec((L, d), prev),
        out_shape=jax.ShapeDtypeStruct((t, d), f32),
        scratch_shapes=[pltpu.VMEM((N_HEADS, hd, hd), f32),
                        pltpu.VMEM((SUBLANES, hd), f32),
                        pltpu.VMEM((SUBLANES, LANES), f32),
                        pltpu.VMEM((2, L, d_mlstm), bf16)],
        compiler_params=pltpu.CompilerParams(
            dimension_semantics=("arbitrary",), vmem_limit_bytes=VMEM_LIMIT),
        name="mlstm_out",
    )(proj, proj, proj, proj, proj, gates, mhg, x2, cc, w_a, w_c, fg)


def kernel(x, norm_g, w_in, b_gates, mh_norm_g, conv_w, conv_b, conv_ln_g, conv_ln_b, w_out, final_norm_g):
    batch, seq, d_model = x.shape
    depth = norm_g.shape[0]
    d_mlstm = mh_norm_g.shape[1]
    d_conv = conv_b.shape[1]
    n_gate = 2 * N_HEADS
    g0 = 5 * d_mlstm

    h = x.reshape(batch * seq, d_model)
    for l in range(depth):
        w_t = jnp.swapaxes(w_in[l], 0, 1)
        wt_mlstm = w_t[:g0].astype(bf16)
        wt_cv = w_t[g0 + n_gate:].astype(bf16)
        wg_t = w_t[g0:g0 + n_gate].astype(bf16)
        proj, gates, c = _proj_conv(h, norm_g[l][None, :], wt_mlstm, wt_cv, wg_t, b_gates[l][:, None],
                                    conv_w[l], conv_b[l][None, :], conv_ln_g[l][None, :],
                                    conv_ln_b[l][None, :], seq=seq, d_conv=d_conv)
        wo = w_out[l].astype(bf16)
        h = _mlstm_out(proj, gates, mh_norm_g[l][None, :], h, c, wo[:d_mlstm], wo[d_mlstm:],
                       final_norm_g[None, :], seq=seq, d_mlstm=d_mlstm, final_norm=(l == depth - 1))
    return h.reshape(batch, seq, d_model)
```

```python
import functools

import jax
import jax.numpy as jnp
from jax import lax
from jax.experimental import pallas as pl
from jax.experimental.pallas import tpu as pltpu

N_HEADS = 4
CONV_WIDTH = 31
EPS = 1e-6
M_INIT = -1e30

MLSTM_CHUNK = 256
LANES = 128
SUBLANES = 8
CONV_HALO = 32
VMEM_LIMIT = 56 * 1024 * 1024

f32 = jnp.float32
bf16 = jnp.bfloat16

_NT = (((1,), (1,)), ((), ()))
_TN = (((0,), (0,)), ((), ()))


def _zero_bits(x):
    b = pltpu.bitcast(x, jnp.uint32)
    return lax.shift_right_logical(lax.shift_right_logical(b, jnp.uint32(16)), jnp.uint32(16))


def _sigmoid(x):
    return 0.5 * jnp.tanh(0.5 * x) + 0.5


def _silu(x):
    hx = 0.5 * x
    return hx * jnp.tanh(hx) + hx


def _conv_rows(x, w_ref, lanes, rb, dep):
    acc = None
    for r in range(SUBLANES):
        nq = (CONV_WIDTH - 1 - r) // SUBLANES + 1
        lo = 0 if r == 0 else SUBLANES
        p = None
        for q in range(nq):
            j = CONV_WIDTH - 1 - (SUBLANES * q + r)
            start = CONV_HALO - lo - SUBLANES * q
            wj = w_ref[j:j + 1, lanes]
            if dep is not None:
                wj = pltpu.bitcast(pltpu.bitcast(wj, jnp.uint32) | dep, f32)
            term = wj * x[start:start + rb + lo, :]
            p = term if p is None else p + term
        acc = p if r == 0 else acc + p[SUBLANES - r:SUBLANES - r + rb, :]
        dep = _zero_bits(acc[0:1, :])
    return acc, dep


def _proj_conv_kernel(x_ref, g_ref, wm_ref, wc_ref, wg_ref, bg_ref, cw_ref, cb_ref, lng_ref, lnb_ref,
                      proj_ref, gates_ref, c_ref,
                      u_ref, ag_ref, z_ref, buf_ref, acc_ref, *, tiles_per_seq, tn, rb):
    i = pl.program_id(0)
    tm, d = x_ref.shape
    dc = c_ref.shape[1]
    halo = CONV_HALO

    @pl.when(i == 0)
    def _():
        z_ref[...] = jnp.zeros_like(z_ref)
        buf_ref[...] = jnp.zeros_like(buf_ref)

    x = x_ref[...]
    u = x * lax.rsqrt(jnp.mean(x * x, axis=-1, keepdims=True) + EPS) * g_ref[...]
    u_ref[...] = u.astype(bf16)
    gates_ref[...] = lax.dot_general(wg_ref[...], u_ref[...], _NT, preferred_element_type=f32) + bg_ref[...]

    dm = wm_ref.shape[0]

    def matmul(col):
        w = wm_ref[col:col + tn, :] if col < dm else wc_ref[col - dm:col - dm + tn, :]
        return lax.dot_general(u_ref[...], w, _NT, preferred_element_type=f32).astype(bf16)

    def glu_item(r0, n, with_history):
        def compute():
            half = jnp.full((1, dc), 0.5, f32)
            ys = []
            for p0 in range(r0, r0 + n, rb):
                a = ag_ref[p0:p0 + rb, 0:dc].astype(f32)
                g = ag_ref[p0:p0 + rb, dc:2 * dc].astype(f32)
                y = a * (half * jnp.tanh(half * g) + half)
                half = pltpu.bitcast(pltpu.bitcast(half, jnp.uint32) | _zero_bits(y[0:1, :]), f32)
                ys.append(y)
            if with_history:
                tail = buf_ref[tm:tm + halo, :]
                ys.append(jnp.where(i % tiles_per_seq == 0, jnp.zeros_like(tail), tail))
            return ys

        def store(ys):
            for j in range(n // rb):
                buf_ref[halo + r0 + j * rb:halo + r0 + (j + 1) * rb, :] = ys[j]
            if with_history:
                buf_ref[0:halo, :] = ys[-1]
        return compute, store

    def conv_item(units):
        def compute():
            ys, dep = [], None
            for t0, l0 in units:
                x = buf_ref[t0:t0 + halo + rb, l0:l0 + LANES]
                y, dep = _conv_rows(x, cw_ref, slice(l0, l0 + LANES), rb, dep)
                ys.append(y + cb_ref[:, l0:l0 + LANES])
            return ys

        def store(ys):
            for y, (t0, l0) in zip(ys, units):
                acc_ref[t0:t0 + rb, l0:l0 + LANES] = y
        return compute, store

    def ln_item(r0, n):
        def compute():
            outs = []
            gain = lng_ref[...]
            for p0 in range(r0, r0 + n, rb):
                y = acc_ref[p0:p0 + rb, :]
                mu = jnp.mean(y, axis=-1, keepdims=True)
                yc = y - mu
                var = jnp.mean(yc * yc, axis=-1, keepdims=True)
                yn = yc * lax.rsqrt(var + EPS) * gain + lnb_ref[...]
                out = _silu(yn) * _silu(z_ref[p0:p0 + rb, :].astype(f32))
                gain = pltpu.bitcast(pltpu.bitcast(gain, jnp.uint32) | _zero_bits(out[0:1, :]), f32)
                outs.append(out.astype(c_ref.dtype))
            return outs

        def store(outs):
            for j, out in enumerate(outs):
                c_ref[r0 + j * rb:r0 + (j + 1) * rb, :] = out
        return compute, store

    def to_proj(col, val):
        proj_ref[:, col:col + tn] = val

    def to_ag(col, val):
        ag_ref[:, col - a0:col - a0 + tn] = val

    def to_z(col, val):
        z_ref[:, col - zc0:col - zc0 + tn] = val

    q0, k0, v0, o0, zm0, a0, g0, zc0 = (j * d for j in range(8))
    assert tn == d and tm == 8 * rb
    mm_order = [(to_proj, q0), (to_proj, k0), (to_ag, a0), (to_ag, g0),
                (to_proj, v0), (to_proj, o0), (to_proj, zm0), (to_z, zc0)]
    units = [(t0, l0) for t0 in range(0, tm, rb) for l0 in range(0, dc, LANES)]
    n_units = (11, 9, 9, 9, 9, 9, 8)
    cuts = [sum(n_units[:j]) for j in range(len(n_units) + 1)]
    assert cuts[-1] == len(units)
    per_row_block = dc // LANES
    work = []
    for j in range(len(n_units)):
        items = [conv_item(units[cuts[j]:cuts[j + 1]])]
        if j >= 1:
            assert cuts[j] >= j * per_row_block
            items.append(ln_item((j - 1) * rb, rb))
        work.append(items)
    done = len(n_units) - 1
    assert cuts[done + 1] >= min(((tm // 2) // rb + 1) * per_row_block, len(units))
    work[done].append(glu_item(0, tm // 2, True))
    work.append([ln_item(done * rb, tm - done * rb), glu_item(tm // 2, tm // 2, False)])

    once = jnp.minimum(i, 0) + 1

    for (mm_store, col), items in zip(mm_order, work):
        def region(_, carry, mm_store=mm_store, col=col, items=items):
            outs = [compute() for compute, _ in items]
            res = matmul(col)
            mm_store(col, res)
            for (_, store), out in zip(items, outs):
                store(out)
            return carry
        lax.fori_loop(0, once, region, 0)


def _proj_conv(x2, g, wt_mlstm, wt_cv, wg_t, bg, conv_w, conv_b, ln_g, ln_b, *, seq, d_conv,
               tm=512, tn=1024, rb=64):
    t, d = x2.shape
    dm = wt_mlstm.shape[0]
    assert (d == d_conv and dm == 5 * d and wt_cv.shape[0] == 3 * d_conv and d % tn == 0
            and seq % tm == 0 and tm % rb == 0)
    n = t // tm
    last = n - 1
    const = lambda i: (0, 0)
    vec = pl.BlockSpec((1, d_conv), const)
    return pl.pallas_call(
        functools.partial(_proj_conv_kernel, tiles_per_seq=seq // tm, tn=tn, rb=rb),
        grid=(n + 1,),
        in_specs=[
            pl.BlockSpec((tm, d), lambda i: (jnp.minimum(i, last), 0)),
            pl.BlockSpec((1, d), const),
            pl.BlockSpec(wt_mlstm.shape, const, pipeline_mode=pl.Buffered(1)),
            pl.BlockSpec(wt_cv.shape, const, pipeline_mode=pl.Buffered(1)),
            pl.BlockSpec((2 * N_HEADS, d), const),
            pl.BlockSpec((2 * N_HEADS, 1), const),
            pl.BlockSpec((CONV_WIDTH, d_conv), const),
            vec, vec, vec,
        ],
        out_specs=[
            pl.BlockSpec((tm, dm), lambda i: (jnp.minimum(i, last), 0)),
            pl.BlockSpec((2 * N_HEADS, tm), lambda i: (0, jnp.minimum(i, last))),
            pl.BlockSpec((tm, d_conv), lambda i: (jnp.maximum(i - 1, 0), 0)),
        ],
        out_shape=[
            jax.ShapeDtypeStruct((t, dm), bf16),
            jax.ShapeDtypeStruct((2 * N_HEADS, t), f32),
            jax.ShapeDtypeStruct((t, d_conv), bf16),
        ],
        scratch_shapes=[
            pltpu.VMEM((tm, d), bf16),
            pltpu.VMEM((tm, 2 * d_conv), bf16),
            pltpu.VMEM((tm, d_conv), bf16),
            pltpu.VMEM((tm + CONV_HALO, d_conv), f32),
            pltpu.VMEM((tm, d_conv), f32),
        ],
        compiler_params=pltpu.CompilerParams(
            dimension_semantics=("arbitrary",), vmem_limit_bytes=VMEM_LIMIT),
        name="proj_conv",
    )(x2, g, wt_mlstm, wt_cv, wg_t, bg, conv_w, conv_b, ln_g, ln_b)


def _cumsum_lanes(x):
    rows, n = x.shape
    tri = (lax.broadcasted_iota(jnp.int32, (n, n), 0)
           <= lax.broadcasted_iota(jnp.int32, (n, n), 1)).astype(bf16)
    hi = x.astype(bf16).astype(f32)
    mid = (x - hi).astype(bf16).astype(f32)
    lo = x - hi - mid
    parts = jnp.dot(jnp.concatenate([hi, mid, lo], axis=0).astype(bf16), tri, preferred_element_type=f32)
    return parts[0:rows] + parts[rows:2 * rows] + parts[2 * rows:3 * rows]


def _gate_chain(gr, m_prev):
    L = gr.shape[1]
    lf = jnp.minimum(gr, 0.0) - jnp.log1p(jnp.exp(-jnp.abs(gr)))
    bfull = _cumsum_lanes(lf)
    ig = gr[0:N_HEADS]
    b = bfull[N_HEADS:2 * N_HEADS]
    b_last = b[:, L - 1:L]
    a = b_last - b + ig
    a_max = jnp.max(a, axis=1, keepdims=True)
    m_new = jnp.maximum(b_last + m_prev, a_max)
    s_old = jnp.exp(b_last + m_prev - m_new)
    w_a = jnp.exp(a - m_new)
    c_row = ig - b
    run = jnp.concatenate([c_row, c_row], axis=0)
    lane = lax.broadcasted_iota(jnp.int32, run.shape, 1)
    shift = 1
    while shift < L:
        run = jnp.maximum(run, jnp.where(lane >= shift, pltpu.roll(run, shift, 1), -jnp.inf))
        shift *= 2
    run = run[0:N_HEADS]
    li = b + m_prev
    m_t = jnp.maximum(li, b + run)
    s_inter = jnp.exp(li - m_t)
    e_neg = jnp.exp(-m_t)
    cols = jnp.concatenate([jnp.transpose(jnp.concatenate([b, w_a], axis=0)),
                            jnp.transpose(jnp.concatenate([m_t, s_inter], axis=0)),
                            jnp.transpose(jnp.concatenate([e_neg, e_neg], axis=0))], axis=1)
    return c_row, w_a, cols, m_prev, m_new, s_old


def _scores(q_ref, k_ref, head_dim):
    return [lax.dot_general(q_ref[:, h * head_dim:(h + 1) * head_dim],
                            k_ref[:, h * head_dim:(h + 1) * head_dim], _NT,
                            preferred_element_type=f32) * head_dim ** -0.5 for h in range(N_HEADS)]


def _store_front(chain, scores, grow_ref, gcol_ref, gsc_ref, s_ref):
    c_row, w_a, cols, m_prev, m_new, s_old = chain
    grow_ref[0:N_HEADS, :] = c_row
    grow_ref[N_HEADS:2 * N_HEADS, :] = w_a
    gcol_ref[:, 0:cols.shape[1]] = cols
    for j, v in enumerate((m_prev, m_new, s_old)):
        gsc_ref[j, 0:N_HEADS, :] = jnp.broadcast_to(v, (N_HEADS, gsc_ref.shape[2]))
    for h in range(N_HEADS):
        s_ref[h] = scores[h]


def _mlstm_out_kernel(q_ref, k_ref, v_ref, o_ref, z_ref, gates_ref, qn_ref, kn_ref, gates_next_ref, mhg_ref,
                      x_ref, cc_ref, wa_ref, wc_ref, fg_ref, out_ref,
                      c_ref, n_ref, hm_ref, grow_ref, gcol_ref, gsc_ref, s_ref,
                      *, head_dim, chunks_per_seq, final_norm):
    L = q_ref.shape[0]
    scale = head_dim ** -0.5
    step = pl.program_id(0)
    wr = step % 2
    rd = 1 - wr

    m_init = jnp.full((N_HEADS, 1), M_INIT, f32)

    @pl.when(step == 0)
    def _():
        hm_ref[...] = jnp.zeros_like(hm_ref)
        _store_front(_gate_chain(gates_ref[...], m_init), _scores(q_ref, k_ref, head_dim),
                     grow_ref, gcol_ref, gsc_ref, s_ref)

    @pl.when(step % chunks_per_seq == 0)
    def _():
        c_ref[...] = jnp.zeros_like(c_ref)
        n_ref[...] = jnp.zeros_like(n_ref)

    c_row = grow_ref[0:N_HEADS, :]
    w_a = grow_ref[N_HEADS:2 * N_HEADS, :]
    cols = gcol_ref[:, 0:6 * N_HEADS]
    m_prev = gsc_ref[0, 0:N_HEADS, 0:1]
    m_new = gsc_ref[1, 0:N_HEADS, 0:1]
    s_old = gsc_ref[2, 0:N_HEADS, 0:1]
    next_chain = _gate_chain(gates_next_ref[...],
                             jnp.where((step + 1) % chunks_per_seq == 0, m_init, m_new))

    row_id = lax.broadcasted_iota(jnp.int32, (L, L), 0)
    col_id = lax.broadcasted_iota(jnp.int32, (L, L), 1)
    causal = col_id <= row_id

    d_out = out_ref.shape[1]
    tn = d_out // N_HEADS

    heads = range(N_HEADS)
    hs = [slice(h * head_dim, (h + 1) * head_dim) for h in heads]
    ys = []

    def out_proj_part(j):
        ol = slice(j * tn, (j + 1) * tn)
        ys.append(jnp.dot(hm_ref[rd], wa_ref[:, ol], preferred_element_type=f32)
                  + jnp.dot(cc_ref[...], wc_ref[:, ol], preferred_element_type=f32))

    o_gate = [_sigmoid(o_ref[:, hs[h]].astype(f32)) for h in heads]
    z_gate = [_silu(z_ref[:, hs[h]].astype(f32)) for h in heads]
    qn = [jnp.sum(q_ref[:, hs[h]].astype(f32) * n_ref[h:h + 1, :], axis=1, keepdims=True) for h in heads]

    s_ = [s_ref[h] for h in heads]
    c_prev = [c_ref[h] for h in heads]
    qc = [jnp.dot(q_ref[:, hs[h]], c_prev[h].astype(bf16), preferred_element_type=f32) for h in heads]

    p_, e_neg_, s_inter_ = [], [], []
    for h in heads:
        b_col = cols[:, h:h + 1]
        m_t = cols[:, 2 * N_HEADS + h:2 * N_HEADS + h + 1]
        g = jnp.where(causal, b_col + c_row[h:h + 1, :], -jnp.inf)
        p_.append(jnp.exp(g - m_t) * s_[h])
        s_inter_.append(cols[:, 3 * N_HEADS + h:3 * N_HEADS + h + 1])
        e_neg_.append(cols[:, 4 * N_HEADS + h:4 * N_HEADS + h + 1])

    pv = [jnp.dot(p_[h].astype(bf16), v_ref[:, hs[h]], preferred_element_type=f32) for h in heads]
    out_proj_part(0)
    out_proj_part(1)

    hm_new, vw = [], []
    for h in heads:
        num = pv[h] + s_inter_[h] * qc[h]
        den = jnp.sum(p_[h], axis=1, keepdims=True) + s_inter_[h] * qn[h]
        hh = num * (1.0 / jnp.maximum(jnp.abs(den), e_neg_[h]))
        hh = o_gate[h] * hh
        hh = hh * lax.rsqrt(jnp.mean(hh * hh, axis=-1, keepdims=True) + EPS)
        hh = hh * mhg_ref[:, hs[h]]
        hm_new.append((hh * z_gate[h]).astype(bf16))
        w_col = cols[:, N_HEADS + h:N_HEADS + h + 1]
        vw.append((v_ref[:, hs[h]].astype(f32) * w_col).astype(bf16))

    out_proj_part(2)
    kv = [lax.dot_general(k_ref[:, hs[h]], vw[h], _TN, preferred_element_type=f32) * scale
          for h in heads]
    out_proj_part(3)
    next_scores = _scores(qn_ref, kn_ref, head_dim)
    n_all = jnp.dot(jnp.concatenate([w_a, w_a], axis=0).astype(bf16), k_ref[...],
                    preferred_element_type=f32) * scale
    c_new =[s_old[h:h + 1, :] * c_prev[h] + kv[h] for h in heads]
    n_new = [s_old[h:h + 1, :] * n_ref[h:h + 1, :] + n_all[h:h + 1, hs[h]] for h in heads]

    y = x_ref[...] + jnp.concatenate(ys, axis=1)
    if final_norm:
        y = y * lax.rsqrt(jnp.mean(y * y, axis=-1, keepdims=True) + EPS) * fg_ref[...]
    out_ref[...] = y
    for h in heads:
        hm_ref[wr, :, hs[h]] = hm_new[h]
        c_ref[h] = c_new[h]
        n_ref[h:h + 1, :] = n_new[h]
    _store_front(next_chain, next_scores, grow_ref, gcol_ref, gsc_ref, s_ref)


def _mlstm_out(proj, gates, mhg, x2, cc, w_a, w_c, fg, *, seq, d_mlstm, final_norm):
    t, d = x2.shape
    L = MLSTM_CHUNK
    nc = seq // L
    n = t // L
    last = n - 1
    hd = d_mlstm // N_HEADS
    assert d % N_HEADS == 0 and seq % L == 0

    def col(j, ahead=0):
        return pl.BlockSpec((L, d_mlstm), lambda s, j=j: (jnp.minimum(s + ahead, last), j))

    def gate_rows(ahead):
        return pl.BlockSpec((2 * N_HEADS, L), lambda s: (0, jnp.minimum(s + ahead, last)))

    prev = lambda s: (jnp.maximum(s - 1, 0), 0)
    const = lambda s: (0, 0)
    return pl.pallas_call(
        functools.partial(_mlstm_out_kernel, head_dim=hd, chunks_per_seq=nc, final_norm=final_norm),
        grid=(n + 1,),
        in_specs=[col(0), col(1), col(2), col(3), col(4), gate_rows(0),
                  col(0, 1), col(1, 1), gate_rows(1),
                  pl.BlockSpec((1, d_mlstm), const),
                  pl.BlockSpec((L, d), prev),
                  pl.BlockSpec((L, cc.shape[1]), prev),
                  pl.BlockSpec(w_a.shape, const, pipeline_mode=pl.Buffered(1)),
                  pl.BlockSpec(w_c.shape, const, pipeline_mode=pl.Buffered(1)),
                  pl.BlockSpec((1, d), const)],
        out_specs=pl.BlockSpec((L, d), prev),
        out_shape=jax.ShapeDtypeStruct((t, d), f32),
        scratch_shapes=[pltpu.VMEM((N_HEADS, hd, hd), f32),
                        pltpu.VMEM((SUBLANES, hd), f32),
                        pltpu.VMEM((2, L, d_mlstm), bf16),
                        pltpu.VMEM((2 * N_HEADS, L), f32),
                        pltpu.VMEM((L, LANES), f32),
                        pltpu.VMEM((3, SUBLANES, LANES), f32),
                        pltpu.VMEM((N_HEADS, L, L), f32)],
        compiler_params=pltpu.CompilerParams(
            dimension_semantics=("arbitrary",), vmem_limit_bytes=VMEM_LIMIT),
        name="mlstm_out",
    )(proj, proj, proj, proj, proj, gates, proj, proj, gates, mhg, x2, cc, w_a, w_c, fg)


def kernel(x, norm_g, w_in, b_gates, mh_norm_g, conv_w, conv_b, conv_ln_g, conv_ln_b, w_out, final_norm_g):
    batch, seq, d_model = x.shape
    depth = norm_g.shape[0]
    d_mlstm = mh_norm_g.shape[1]
    d_conv = conv_b.shape[1]
    n_gate = 2 * N_HEADS
    g0 = 5 * d_mlstm

    h = x.reshape(batch * seq, d_model)
    for l in range(depth):
        w_t = jnp.swapaxes(w_in[l], 0, 1)
        wt_mlstm = w_t[:g0].astype(bf16)
        wt_cv = w_t[g0 + n_gate:].astype(bf16)
        wg_t = w_t[g0:g0 + n_gate].astype(bf16)
        proj, gates, c = _proj_conv(h, norm_g[l][None, :], wt_mlstm, wt_cv, wg_t, b_gates[l][:, None],
                                    conv_w[l], conv_b[l][None, :], conv_ln_g[l][None, :],
                                    conv_ln_b[l][None, :], seq=seq, d_conv=d_conv)
        wo = w_out[l].astype(bf16)
        h = _mlstm_out(proj, gates, mh_norm_g[l][None, :], h, c, wo[:d_mlstm], wo[d_mlstm:],
                       final_norm_g[None, :], seq=seq, d_mlstm=d_mlstm, final_norm=(l == depth - 1))
    return h.reshape(batch, seq, d_model)
```

```python
import functools

import jax
import jax.numpy as jnp
from jax import lax
from jax.experimental import pallas as pl
from jax.experimental.pallas import tpu as pltpu

N_HEADS = 4
CONV_WIDTH = 31
EPS = 1e-6
M_INIT = -1e30

MLSTM_CHUNK = 256
LANES = 128
SUBLANES = 8
CONV_HALO = 32
VMEM_LIMIT = 56 * 1024 * 1024

f32 = jnp.float32
bf16 = jnp.bfloat16

_NT = (((1,), (1,)), ((), ()))
_TN = (((0,), (0,)), ((), ()))


def _zero_bits(x):
    b = pltpu.bitcast(x, jnp.uint32)
    return lax.shift_right_logical(lax.shift_right_logical(b, jnp.uint32(16)), jnp.uint32(16))


def _sigmoid(x):
    return 0.5 * jnp.tanh(0.5 * x) + 0.5


def _silu(x):
    hx = 0.5 * x
    return hx * jnp.tanh(hx) + hx


def _conv_rows(x, w_ref, lanes, rb, dep):
    acc = None
    for r in range(SUBLANES):
        nq = (CONV_WIDTH - 1 - r) // SUBLANES + 1
        lo = 0 if r == 0 else SUBLANES
        p = None
        for q in range(nq):
            j = CONV_WIDTH - 1 - (SUBLANES * q + r)
            start = CONV_HALO - lo - SUBLANES * q
            wj = w_ref[j:j + 1, lanes]
            if dep is not None:
                wj = pltpu.bitcast(pltpu.bitcast(wj, jnp.uint32) | dep, f32)
            term = wj * x[start:start + rb + lo, :]
            p = term if p is None else p + term
        acc = p if r == 0 else acc + p[SUBLANES - r:SUBLANES - r + rb, :]
        dep = _zero_bits(acc[0:1, :])
    return acc, dep


def _proj_conv_kernel(x_ref, g_ref, wm_ref, wc_ref, wg_ref, bg_ref, cw_ref, cb_ref, lng_ref, lnb_ref,
                      proj_ref, gates_ref, c_ref,
                      u_ref, ag_ref, z_ref, buf_ref, acc_ref, *, tiles_per_seq, tn, rb):
    i = pl.program_id(0)
    tm, d = x_ref.shape
    dc = c_ref.shape[1]
    halo = CONV_HALO

    @pl.when(i == 0)
    def _():
        z_ref[...] = jnp.zeros_like(z_ref)
        buf_ref[...] = jnp.zeros_like(buf_ref)

    x = x_ref[...]
    u = x * lax.rsqrt(jnp.mean(x * x, axis=-1, keepdims=True) + EPS) * g_ref[...]
    u_ref[...] = u.astype(bf16)
    gates_ref[...] = lax.dot_general(wg_ref[...], u_ref[...], _NT, preferred_element_type=f32) + bg_ref[...]

    dm = wm_ref.shape[0]

    def matmul(col):
        w = wm_ref[col:col + tn, :] if col < dm else wc_ref[col - dm:col - dm + tn, :]
        return lax.dot_general(u_ref[...], w, _NT, preferred_element_type=f32).astype(bf16)

    def glu_item(r0, n, with_history):
        def compute():
            half = jnp.full((1, dc), 0.5, f32)
            ys = []
            for p0 in range(r0, r0 + n, rb):
                a = ag_ref[p0:p0 + rb, 0:dc].astype(f32)
                g = ag_ref[p0:p0 + rb, dc:2 * dc].astype(f32)
                y = a * (half * jnp.tanh(half * g) + half)
                half = pltpu.bitcast(pltpu.bitcast(half, jnp.uint32) | _zero_bits(y[0:1, :]), f32)
                ys.append(y)
            if with_history:
                tail = buf_ref[tm:tm + halo, :]
                ys.append(jnp.where(i % tiles_per_seq == 0, jnp.zeros_like(tail), tail))
            return ys

        def store(ys):
            for j in range(n // rb):
                buf_ref[halo + r0 + j * rb:halo + r0 + (j + 1) * rb, :] = ys[j]
            if with_history:
                buf_ref[0:halo, :] = ys[-1]
        return compute, store

    def conv_item(units):
        def compute():
            ys, dep = [], None
            for t0, l0 in units:
                x = buf_ref[t0:t0 + halo + rb, l0:l0 + LANES]
                y, dep = _conv_rows(x, cw_ref, slice(l0, l0 + LANES), rb, dep)
                ys.append(y + cb_ref[:, l0:l0 + LANES])
            return ys

        def store(ys):
            for y, (t0, l0) in zip(ys, units):
                acc_ref[t0:t0 + rb, l0:l0 + LANES] = y
        return compute, store

    def ln_item(r0, n):
        def compute():
            outs = []
            gain = lng_ref[...]
            for p0 in range(r0, r0 + n, rb):
                y = acc_ref[p0:p0 + rb, :]
                mu = jnp.mean(y, axis=-1, keepdims=True)
                yc = y - mu
                var = jnp.mean(yc * yc, axis=-1, keepdims=True)
                yn = yc * lax.rsqrt(var + EPS) * gain + lnb_ref[...]
                out = _silu(yn) * _silu(z_ref[p0:p0 + rb, :].astype(f32))
                gain = pltpu.bitcast(pltpu.bitcast(gain, jnp.uint32) | _zero_bits(out[0:1, :]), f32)
                outs.append(out.astype(c_ref.dtype))
            return outs

        def store(outs):
            for j, out in enumerate(outs):
                c_ref[r0 + j * rb:r0 + (j + 1) * rb, :] = out
        return compute, store

    def to_proj(col, val):
        proj_ref[:, col:col + tn] = val

    def to_ag(col, val):
        ag_ref[:, col - a0:col - a0 + tn] = val

    def to_z(col, val):
        z_ref[:, col - zc0:col - zc0 + tn] = val

    q0, k0, v0, o0, zm0, a0, g0, zc0 = (j * d for j in range(8))
    assert tn == d and tm == 8 * rb
    mm_order = [(to_proj, q0), (to_proj, k0), (to_ag, a0), (to_ag, g0),
                (to_proj, v0), (to_proj, o0), (to_proj, zm0), (to_z, zc0)]
    units = [(t0, l0) for t0 in range(0, tm, rb) for l0 in range(0, dc, LANES)]
    n_units = (11, 9, 9, 9, 9, 9, 8)
    cuts = [sum(n_units[:j]) for j in range(len(n_units) + 1)]
    assert cuts[-1] == len(units)
    per_row_block = dc // LANES
    work = []
    for j in range(len(n_units)):
        items = [conv_item(units[cuts[j]:cuts[j + 1]])]
        if j >= 1:
            assert cuts[j] >= j * per_row_block
            items.append(ln_item((j - 1) * rb, rb))
        work.append(items)
    done = len(n_units) - 1
    assert cuts[done + 1] >= min(((tm // 2) // rb + 1) * per_row_block, len(units))
    work[done].append(glu_item(0, tm // 2, True))
    work.append([ln_item(done * rb, tm - done * rb), glu_item(tm // 2, tm // 2, False)])

    once = jnp.minimum(i, 0) + 1

    for (mm_store, col), items in zip(mm_order, work):
        def region(_, carry, mm_store=mm_store, col=col, items=items):
            outs = [compute() for compute, _ in items]
            res = matmul(col)
            mm_store(col, res)
            for (_, store), out in zip(items, outs):
                store(out)
            return carry
        lax.fori_loop(0, once, region, 0)


def _proj_conv(x2, g, wt_mlstm, wt_cv, wg_t, bg, conv_w, conv_b, ln_g, ln_b, *, seq, d_conv,
               tm=512, tn=1024, rb=64):
    t, d = x2.shape
    dm = wt_mlstm.shape[0]
    assert (d == d_conv and dm == 5 * d and wt_cv.shape[0] == 3 * d_conv and d % tn == 0
            and seq % tm == 0 and tm % rb == 0)
    n = t // tm
    last = n - 1
    const = lambda i: (0, 0)
    vec = pl.BlockSpec((1, d_conv), const)
    return pl.pallas_call(
        functools.partial(_proj_conv_kernel, tiles_per_seq=seq // tm, tn=tn, rb=rb),
        grid=(n + 1,),
        in_specs=[
            pl.BlockSpec((tm, d), lambda i: (jnp.minimum(i, last), 0)),
            pl.BlockSpec((1, d), const),
            pl.BlockSpec(wt_mlstm.shape, const, pipeline_mode=pl.Buffered(1)),
            pl.BlockSpec(wt_cv.shape, const, pipeline_mode=pl.Buffered(1)),
            pl.BlockSpec((2 * N_HEADS, d), const),
            pl.BlockSpec((2 * N_HEADS, 1), const),
            pl.BlockSpec((CONV_WIDTH, d_conv), const),
            vec, vec, vec,
        ],
        out_specs=[
            pl.BlockSpec((tm, dm), lambda i: (jnp.minimum(i, last), 0)),
            pl.BlockSpec((2 * N_HEADS, tm), lambda i: (0, jnp.minimum(i, last))),
            pl.BlockSpec((tm, d_conv), lambda i: (jnp.maximum(i - 1, 0), 0)),
        ],
        out_shape=[
            jax.ShapeDtypeStruct((t, dm), bf16),
            jax.ShapeDtypeStruct((2 * N_HEADS, t), f32),
            jax.ShapeDtypeStruct((t, d_conv), bf16),
        ],
        scratch_shapes=[
            pltpu.VMEM((tm, d), bf16),
            pltpu.VMEM((tm, 2 * d_conv), bf16),
            pltpu.VMEM((tm, d_conv), bf16),
            pltpu.VMEM((tm + CONV_HALO, d_conv), f32),
            pltpu.VMEM((tm, d_conv), f32),
        ],
        compiler_params=pltpu.CompilerParams(
            dimension_semantics=("arbitrary",), vmem_limit_bytes=VMEM_LIMIT),
        name="proj_conv",
    )(x2, g, wt_mlstm, wt_cv, wg_t, bg, conv_w, conv_b, ln_g, ln_b)


def _cumsum_lanes(x):
    rows, n = x.shape
    tri = (lax.broadcasted_iota(jnp.int32, (n, n), 0)
           <= lax.broadcasted_iota(jnp.int32, (n, n), 1)).astype(bf16)
    hi = x.astype(bf16).astype(f32)
    mid = (x - hi).astype(bf16).astype(f32)
    lo = x - hi - mid
    parts = jnp.dot(jnp.concatenate([hi, mid, lo], axis=0).astype(bf16), tri, preferred_element_type=f32)
    return parts[0:rows] + parts[rows:2 * rows] + parts[2 * rows:3 * rows]


def _gate_chain(gr, m_prev):
    L = gr.shape[1]
    lf = jnp.minimum(gr, 0.0) - jnp.log1p(jnp.exp(-jnp.abs(gr)))
    bfull = _cumsum_lanes(lf)
    ig = gr[0:N_HEADS]
    b = bfull[N_HEADS:2 * N_HEADS]
    b_last = b[:, L - 1:L]
    a = b_last - b + ig
    a_max = jnp.max(a, axis=1, keepdims=True)
    m_new = jnp.maximum(b_last + m_prev, a_max)
    s_old = jnp.exp(b_last + m_prev - m_new)
    w_a = jnp.exp(a - m_new)
    c_row = ig - b
    run = jnp.concatenate([c_row, c_row], axis=0)
    lane = lax.broadcasted_iota(jnp.int32, run.shape, 1)
    shift = 1
    while shift < L:
        run = jnp.maximum(run, jnp.where(lane >= shift, pltpu.roll(run, shift, 1), -jnp.inf))
        shift *= 2
    run = run[0:N_HEADS]
    li = b + m_prev
    m_t = jnp.maximum(li, b + run)
    s_inter = jnp.exp(li - m_t)
    e_neg = jnp.exp(-m_t)
    cols = jnp.concatenate([jnp.transpose(jnp.concatenate([b, w_a], axis=0)),
                            jnp.transpose(jnp.concatenate([m_t, s_inter], axis=0)),
                            jnp.transpose(jnp.concatenate([e_neg, e_neg], axis=0))], axis=1)
    return c_row, w_a, cols, m_new, s_old


def _scores(q_ref, k_ref, head_dim):
    return [lax.dot_general(q_ref[:, h * head_dim:(h + 1) * head_dim],
                            k_ref[:, h * head_dim:(h + 1) * head_dim], _NT,
                            preferred_element_type=f32) * head_dim ** -0.5 for h in range(N_HEADS)]


def _store_front(chain, scores, grow_ref, gcol_ref, gsc_ref, s_ref):
    c_row, w_a, cols, m_new, s_old = chain
    grow_ref[0:N_HEADS, :] = c_row
    grow_ref[N_HEADS:2 * N_HEADS, :] = w_a
    gcol_ref[:, 0:cols.shape[1]] = cols
    for j, v in enumerate((m_new, s_old)):
        gsc_ref[j, 0:N_HEADS, :] = jnp.broadcast_to(v, (N_HEADS, gsc_ref.shape[2]))
    for h in range(N_HEADS):
        s_ref[h] = scores[h]


def _mlstm_out_kernel(q_ref, k_ref, v_ref, o_ref, z_ref, gates_ref, qn_ref, kn_ref, gates_next_ref, mhg_ref,
                      x_ref, cc_ref, wa_ref, wc_ref, fg_ref, out_ref,
                      c_ref, n_ref, hm_ref, grow_ref, gcol_ref, gsc_ref, s_ref,
                      *, head_dim, chunks_per_seq, final_norm):
    L = q_ref.shape[0]
    scale = head_dim ** -0.5
    step = pl.program_id(0)
    wr = step % 2
    rd = 1 - wr

    m_init = jnp.full((N_HEADS, 1), M_INIT, f32)

    @pl.when(step == 0)
    def _():
        hm_ref[...] = jnp.zeros_like(hm_ref)
        _store_front(_gate_chain(gates_ref[...], m_init), _scores(q_ref, k_ref, head_dim),
                     grow_ref, gcol_ref, gsc_ref, s_ref)

    @pl.when(step % chunks_per_seq == 0)
    def _():
        c_ref[...] = jnp.zeros_like(c_ref)
        n_ref[...] = jnp.zeros_like(n_ref)

    c_row = grow_ref[0:N_HEADS, :]
    w_a = grow_ref[N_HEADS:2 * N_HEADS, :]
    cols = gcol_ref[:, 0:6 * N_HEADS]
    m_new = gsc_ref[0, 0:N_HEADS, 0:1]
    s_old = gsc_ref[1, 0:N_HEADS, 0:1]
    next_chain = _gate_chain(gates_next_ref[...],
                             jnp.where((step + 1) % chunks_per_seq == 0, m_init, m_new))

    row_id = lax.broadcasted_iota(jnp.int32, (L, L), 0)
    col_id = lax.broadcasted_iota(jnp.int32, (L, L), 1)
    causal = col_id <= row_id

    d_out = out_ref.shape[1]
    tn = d_out // N_HEADS

    heads = range(N_HEADS)
    hs = [slice(h * head_dim, (h + 1) * head_dim) for h in heads]
    ys = []

    def out_proj_part(j):
        ol = slice(j * tn, (j + 1) * tn)
        ys.append(jnp.dot(hm_ref[rd], wa_ref[:, ol], preferred_element_type=f32)
                  + jnp.dot(cc_ref[...], wc_ref[:, ol], preferred_element_type=f32))

    o_gate = [_sigmoid(o_ref[:, hs[h]].astype(f32)) for h in heads]
    z_gate = [_silu(z_ref[:, hs[h]].astype(f32)) for h in heads]
    qn = [jnp.sum(q_ref[:, hs[h]].astype(f32) * n_ref[h:h + 1, :], axis=1, keepdims=True) for h in heads]

    s_ = [s_ref[h] for h in heads]
    c_prev = [c_ref[h] for h in heads]
    qc = [jnp.dot(q_ref[:, hs[h]], c_prev[h].astype(bf16), preferred_element_type=f32) for h in heads]

    p_, e_neg_, s_inter_ = [], [], []
    for h in heads:
        b_col = cols[:, h:h + 1]
        m_t = cols[:, 2 * N_HEADS + h:2 * N_HEADS + h + 1]
        g = jnp.where(causal, b_col + c_row[h:h + 1, :], -jnp.inf)
        p_.append(jnp.exp(g - m_t) * s_[h])
        s_inter_.append(cols[:, 3 * N_HEADS + h:3 * N_HEADS + h + 1])
        e_neg_.append(cols[:, 4 * N_HEADS + h:4 * N_HEADS + h + 1])

    pv = [jnp.dot(p_[h].astype(bf16), v_ref[:, hs[h]], preferred_element_type=f32) for h in heads]
    out_proj_part(0)
    out_proj_part(1)

    hm_new, vw = [], []
    for h in heads:
        num = pv[h] + s_inter_[h] * qc[h]
        den = jnp.sum(p_[h], axis=1, keepdims=True) + s_inter_[h] * qn[h]
        hh = num * (1.0 / jnp.maximum(jnp.abs(den), e_neg_[h]))
        hh = o_gate[h] * hh
        hh = hh * lax.rsqrt(jnp.mean(hh * hh, axis=-1, keepdims=True) + EPS)
        hh = hh * mhg_ref[:, hs[h]]
        hm_new.append((hh * z_gate[h]).astype(bf16))
        w_col = cols[:, N_HEADS + h:N_HEADS + h + 1]
        vw.append((v_ref[:, hs[h]].astype(f32) * w_col).astype(bf16))

    out_proj_part(2)
    kv = [lax.dot_general(k_ref[:, hs[h]], vw[h], _TN, preferred_element_type=f32) * scale
          for h in heads]
    out_proj_part(3)
    next_scores = _scores(qn_ref, kn_ref, head_dim)
    n_all = jnp.dot(jnp.concatenate([w_a, w_a], axis=0).astype(bf16), k_ref[...],
                    preferred_element_type=f32) * scale
    c_new = [s_old[h:h + 1, :] * c_prev[h] + kv[h] for h in heads]
    n_new = [s_old[h:h + 1, :] * n_ref[h:h + 1, :] + n_all[h:h + 1, hs[h]] for h in heads]

    y = x_ref[...] + jnp.concatenate(ys, axis=1)
    if final_norm:
        y = y * lax.rsqrt(jnp.mean(y * y, axis=-1, keepdims=True) + EPS) * fg_ref[...]
    out_ref[...] = y
    for h in heads:
        hm_ref[wr, :, hs[h]] = hm_new[h]
        c_ref[h] = c_new[h]
        n_ref[h:h + 1, :] = n_new[h]
    _store_front(next_chain, next_scores, grow_ref, gcol_ref, gsc_ref, s_ref)


def _mlstm_out(proj, gates, mhg, x2, cc, w_a, w_c, fg, *, seq, d_mlstm, final_norm):
    t, d = x2.shape
    L = MLSTM_CHUNK
    nc = seq // L
    n = t // L
    last = n - 1
    hd = d_mlstm // N_HEADS
    assert d % N_HEADS == 0 and seq % L == 0

    def col(j, ahead=0):
        return pl.BlockSpec((L, d_mlstm), lambda s, j=j: (jnp.minimum(s + ahead, last), j))

    def gate_rows(ahead):
        return pl.BlockSpec((2 * N_HEADS, L), lambda s: (0, jnp.minimum(s + ahead, last)))

    prev = lambda s: (jnp.maximum(s - 1, 0), 0)
    const = lambda s: (0, 0)
    return pl.pallas_call(
        functools.partial(_mlstm_out_kernel, head_dim=hd, chunks_per_seq=nc, final_norm=final_norm),
        grid=(n + 1,),
        in_specs=[col(0), col(1), col(2), col(3), col(4), gate_rows(0),
                  col(0, 1), col(1, 1), gate_rows(1),
                  pl.BlockSpec((1, d_mlstm), const),
                  pl.BlockSpec((L, d), prev),
                  pl.BlockSpec((L, cc.shape[1]), prev),
                  pl.BlockSpec(w_a.shape, const, pipeline_mode=pl.Buffered(1)),
                  pl.BlockSpec(w_c.shape, const, pipeline_mode=pl.Buffered(1)),
                  pl.BlockSpec((1, d), const)],
        out_specs=pl.BlockSpec((L, d), prev),
        out_shape=jax.ShapeDtypeStruct((t, d), f32),
        scratch_shapes=[pltpu.VMEM((N_HEADS, hd, hd), f32),
                        pltpu.VMEM((SUBLANES, hd), f32),
                        pltpu.VMEM((2, L, d_mlstm), bf16),
                        pltpu.VMEM((2 * N_HEADS, L), f32),
                        pltpu.VMEM((L, LANES), f32),
                        pltpu.VMEM((2, SUBLANES, LANES), f32),
                        pltpu.VMEM((N_HEADS, L, L), f32)],
        compiler_params=pltpu.CompilerParams(
            dimension_semantics=("arbitrary",), vmem_limit_bytes=VMEM_LIMIT),
        name="mlstm_out",
    )(proj, proj, proj, proj, proj, gates, proj, proj, gates, mhg, x2, cc, w_a, w_c, fg)


def kernel(x, norm_g, w_in, b_gates, mh_norm_g, conv_w, conv_b, conv_ln_g, conv_ln_b, w_out, final_norm_g):
    batch, seq, d_model = x.shape
    depth = norm_g.shape[0]
    d_mlstm = mh_norm_g.shape[1]
    d_conv = conv_b.shape[1]
    n_gate = 2 * N_HEADS
    g0 = 5 * d_mlstm

    h = x.reshape(batch * seq, d_model)
    for l in range(depth):
        w_t = jnp.swapaxes(w_in[l], 0, 1)
        wt_mlstm = w_t[:g0].astype(bf16)
        wt_cv = w_t[g0 + n_gate:].astype(bf16)
        wg_t = w_t[g0:g0 + n_gate].astype(bf16)
        proj, gates, c = _proj_conv(h, norm_g[l][None, :], wt_mlstm, wt_cv, wg_t, b_gates[l][:, None],
                                    conv_w[l], conv_b[l][None, :], conv_ln_g[l][None, :],
                                    conv_ln_b[l][None, :], seq=seq, d_conv=d_conv)
        wo = w_out[l].astype(bf16)
        h = _mlstm_out(proj, gates, mh_norm_g[l][None, :], h, c, wo[:d_mlstm], wo[d_mlstm:],
                       final_norm_g[None, :], seq=seq, d_mlstm=d_mlstm, final_norm=(l == depth - 1))
    return h.reshape(batch, seq, d_model)
```

```python
import functools

import jax
import jax.numpy as jnp
from jax import lax
from jax.experimental import pallas as pl
from jax.experimental.pallas import tpu as pltpu

N_HEADS = 4
CONV_WIDTH = 31
EPS = 1e-6
M_INIT = -1e30

MLSTM_CHUNK = 256
LANES = 128
SUBLANES = 8
CONV_HALO = 32
VMEM_LIMIT = 56 * 1024 * 1024

f32 = jnp.float32
bf16 = jnp.bfloat16

_NT = (((1,), (1,)), ((), ()))
_TN = (((0,), (0,)), ((), ()))


def _zero_bits(x):
    b = pltpu.bitcast(x, jnp.uint32)
    return lax.shift_right_logical(lax.shift_right_logical(b, jnp.uint32(16)), jnp.uint32(16))


def _sigmoid(x):
    return 0.5 * jnp.tanh(0.5 * x) + 0.5


def _silu(x):
    hx = 0.5 * x
    return hx * jnp.tanh(hx) + hx


def _conv_rows(x, w_ref, lanes, rb, dep):
    acc = None
    for r in range(SUBLANES):
        nq = (CONV_WIDTH - 1 - r) // SUBLANES + 1
        lo = 0 if r == 0 else SUBLANES
        p = None
        for q in range(nq):
            j = CONV_WIDTH - 1 - (SUBLANES * q + r)
            start = CONV_HALO - lo - SUBLANES * q
            wj = w_ref[j:j + 1, lanes]
            if dep is not None:
                wj = pltpu.bitcast(pltpu.bitcast(wj, jnp.uint32) | dep, f32)
            term = wj * x[start:start + rb + lo, :]
            p = term if p is None else p + term
        acc = p if r == 0 else acc + p[SUBLANES - r:SUBLANES - r + rb, :]
        dep = _zero_bits(acc[0:1, :])
    return acc, dep


def _proj_conv_kernel(x_ref, g_ref, wm_ref, wc_ref, wg_ref, bg_ref, cw_ref, cb_ref, lng_ref, lnb_ref,
                      proj_ref, gates_ref, c_ref,
                      u_ref, ag_ref, z_ref, buf_ref, acc_ref, *, tiles_per_seq, tn, rb):
    i = pl.program_id(0)
    tm, d = x_ref.shape
    dc = c_ref.shape[1]
    halo = CONV_HALO

    @pl.when(i == 0)
    def _():
        z_ref[...] = jnp.zeros_like(z_ref)
        buf_ref[...] = jnp.zeros_like(buf_ref)

    x = x_ref[...]
    u = x * lax.rsqrt(jnp.mean(x * x, axis=-1, keepdims=True) + EPS) * g_ref[...]
    u_ref[...] = u.astype(bf16)
    gates_ref[...] = lax.dot_general(wg_ref[...], u_ref[...], _NT, preferred_element_type=f32) + bg_ref[...]

    dm = wm_ref.shape[0]

    def matmul(col):
        w = wm_ref[col:col + tn, :] if col < dm else wc_ref[col - dm:col - dm + tn, :]
        return lax.dot_general(u_ref[...], w, _NT, preferred_element_type=f32).astype(bf16)

    def glu_item(r0, n, with_history):
        def compute():
            half = jnp.full((1, dc), 0.5, f32)
            ys = []
            for p0 in range(r0, r0 + n, rb):
                a = ag_ref[p0:p0 + rb, 0:dc].astype(f32)
                g = ag_ref[p0:p0 + rb, dc:2 * dc].astype(f32)
                y = a * (half * jnp.tanh(half * g) + half)
                half = pltpu.bitcast(pltpu.bitcast(half, jnp.uint32) | _zero_bits(y[0:1, :]), f32)
                ys.append(y)
            if with_history:
                tail = buf_ref[tm:tm + halo, :]
                ys.append(jnp.where(i % tiles_per_seq == 0, jnp.zeros_like(tail), tail))
            return ys

        def store(ys):
            for j in range(n // rb):
                buf_ref[halo + r0 + j * rb:halo + r0 + (j + 1) * rb, :] = ys[j]
            if with_history:
                buf_ref[0:halo, :] = ys[-1]
        return compute, store

    def conv_item(units):
        def compute():
            ys, dep = [], None
            for t0, l0 in units:
                x = buf_ref[t0:t0 + halo + rb, l0:l0 + LANES]
                y, dep = _conv_rows(x, cw_ref, slice(l0, l0 + LANES), rb, dep)
                ys.append(y + cb_ref[:, l0:l0 + LANES])
            return ys

        def store(ys):
            for y, (t0, l0) in zip(ys, units):
                acc_ref[t0:t0 + rb, l0:l0 + LANES] = y
        return compute, store

    def ln_item(r0, n):
        def compute():
            outs = []
            gain = lng_ref[...]
            for p0 in range(r0, r0 + n, rb):
                y = acc_ref[p0:p0 + rb, :]
                mu = jnp.mean(y, axis=-1, keepdims=True)
                yc = y - mu
                var = jnp.mean(yc * yc, axis=-1, keepdims=True)
                yn = yc * lax.rsqrt(var + EPS) * gain + lnb_ref[...]
                out = _silu(yn) * _silu(z_ref[p0:p0 + rb, :].astype(f32))
                gain = pltpu.bitcast(pltpu.bitcast(gain, jnp.uint32) | _zero_bits(out[0:1, :]), f32)
                outs.append(out.astype(c_ref.dtype))
            return outs

        def store(outs):
            for j, out in enumerate(outs):
                c_ref[r0 + j * rb:r0 + (j + 1) * rb, :] = out
        return compute, store

    def to_proj(col, val):
        proj_ref[:, col:col + tn] = val

    def to_ag(col, val):
        ag_ref[:, col - a0:col - a0 + tn] = val

    def to_z(col, val):
        z_ref[:, col - zc0:col - zc0 + tn] = val

    q0, k0, v0, o0, zm0, a0, g0, zc0 = (j * d for j in range(8))
    assert tn == d and tm == 8 * rb
    mm_order = [(to_proj, q0), (to_proj, k0), (to_ag, a0), (to_ag, g0),
                (to_proj, v0), (to_proj, o0), (to_proj, zm0), (to_z, zc0)]
    units = [(t0, l0) for t0 in range(0, tm, rb) for l0 in range(0, dc, LANES)]
    n_units = (11, 9, 9, 9, 9, 9, 8)
    cuts = [sum(n_units[:j]) for j in range(len(n_units) + 1)]
    assert cuts[-1] == len(units)
    per_row_block = dc // LANES
    work = []
    for j in range(len(n_units)):
        items = [conv_item(units[cuts[j]:cuts[j + 1]])]
        if j >= 1:
            assert cuts[j] >= j * per_row_block
            items.append(ln_item((j - 1) * rb, rb))
        work.append(items)
    done = len(n_units) - 1
    assert cuts[done + 1] >= min(((tm // 2) // rb + 1) * per_row_block, len(units))
    work[done].append(glu_item(0, tm // 2, True))
    work.append([ln_item(done * rb, tm - done * rb), glu_item(tm // 2, tm // 2, False)])

    once = jnp.minimum(i, 0) + 1

    for (mm_store, col), items in zip(mm_order, work):
        def region(_, carry, mm_store=mm_store, col=col, items=items):
            outs = [compute() for compute, _ in items]
            res = matmul(col)
            mm_store(col, res)
            for (_, store), out in zip(items, outs):
                store(out)
            return carry
        lax.fori_loop(0, once, region, 0)


def _proj_conv(x2, g, w_t, bg, conv_w, conv_b, ln_g, ln_b, *, seq, d_conv, dm, tm=512, tn=1024, rb=64):
    t, d = x2.shape
    n_gate = 2 * N_HEADS
    assert (d == d_conv and dm == 5 * d and w_t.shape[0] == dm + n_gate + 3 * d_conv and d % tn == 0
            and seq % tm == 0 and tm % rb == 0)
    n = t // tm
    last = n - 1
    const = lambda i: (0, 0)
    vec = pl.BlockSpec((1, d_conv), const)
    return pl.pallas_call(
        functools.partial(_proj_conv_kernel, tiles_per_seq=seq // tm, tn=tn, rb=rb),
        grid=(n + 1,),
        in_specs=[
            pl.BlockSpec((tm, d), lambda i: (jnp.minimum(i, last), 0)),
            pl.BlockSpec((1, d), const),
            pl.BlockSpec((pl.Element(dm), pl.Element(d)), const, pipeline_mode=pl.Buffered(1)),
            pl.BlockSpec((pl.Element(3 * d_conv), pl.Element(d)), lambda i: (dm + n_gate, 0),
                         pipeline_mode=pl.Buffered(1)),
            pl.BlockSpec((pl.Element(n_gate), pl.Element(d)), lambda i: (dm, 0)),
            pl.BlockSpec((2 * N_HEADS, 1), const),
            pl.BlockSpec((CONV_WIDTH, d_conv), const),
            vec, vec, vec,
        ],
        out_specs=[
            pl.BlockSpec((tm, dm), lambda i: (jnp.minimum(i, last), 0)),
            pl.BlockSpec((2 * N_HEADS, tm), lambda i: (0, jnp.minimum(i, last))),
            pl.BlockSpec((tm, d_conv), lambda i: (jnp.maximum(i - 1, 0), 0)),
        ],
        out_shape=[
            jax.ShapeDtypeStruct((t, dm), bf16),
            jax.ShapeDtypeStruct((2 * N_HEADS, t), f32),
            jax.ShapeDtypeStruct((t, d_conv), bf16),
        ],
        scratch_shapes=[
            pltpu.VMEM((tm, d), bf16),
            pltpu.VMEM((tm, 2 * d_conv), bf16),
            pltpu.VMEM((tm, d_conv), bf16),
            pltpu.VMEM((tm + CONV_HALO, d_conv), f32),
            pltpu.VMEM((tm, d_conv), f32),
        ],
        compiler_params=pltpu.CompilerParams(
            dimension_semantics=("arbitrary",), vmem_limit_bytes=VMEM_LIMIT),
        name="proj_conv",
    )(x2, g, w_t, w_t, w_t, bg, conv_w, conv_b, ln_g, ln_b)


def _cumsum_lanes(x):
    rows, n = x.shape
    tri = (lax.broadcasted_iota(jnp.int32, (n, n), 0)
           <= lax.broadcasted_iota(jnp.int32, (n, n), 1)).astype(bf16)
    hi = x.astype(bf16).astype(f32)
    mid = (x - hi).astype(bf16).astype(f32)
    lo = x - hi - mid
    parts = jnp.dot(jnp.concatenate([hi, mid, lo], axis=0).astype(bf16), tri, preferred_element_type=f32)
    return parts[0:rows] + parts[rows:2 * rows] + parts[2 * rows:3 * rows]


def _gate_chain(gr, m_prev):
    L = gr.shape[1]
    lf = jnp.minimum(gr, 0.0) - jnp.log1p(jnp.exp(-jnp.abs(gr)))
    bfull = _cumsum_lanes(lf)
    ig = gr[0:N_HEADS]
    b = bfull[N_HEADS:2 * N_HEADS]
    b_last = b[:, L - 1:L]
    a = b_last - b + ig
    a_max = jnp.max(a, axis=1, keepdims=True)
    m_new = jnp.maximum(b_last + m_prev, a_max)
    s_old = jnp.exp(b_last + m_prev - m_new)
    w_a = jnp.exp(a - m_new)
    c_row = ig - b
    run = jnp.concatenate([c_row, c_row], axis=0)
    lane = lax.broadcasted_iota(jnp.int32, run.shape, 1)
    shift = 1
    while shift < L:
        run = jnp.maximum(run, jnp.where(lane >= shift, pltpu.roll(run, shift, 1), -jnp.inf))
        shift *= 2
    run = run[0:N_HEADS]
    li = b + m_prev
    m_t = jnp.maximum(li, b + run)
    s_inter = jnp.exp(li - m_t)
    e_neg = jnp.exp(-m_t)
    cols = jnp.concatenate([jnp.transpose(jnp.concatenate([b, w_a], axis=0)),
                            jnp.transpose(jnp.concatenate([m_t, s_inter], axis=0)),
                            jnp.transpose(jnp.concatenate([e_neg, e_neg], axis=0))], axis=1)
    return c_row, w_a, cols, m_new, s_old


def _scores(q_ref, k_ref, head_dim):
    return [lax.dot_general(q_ref[:, h * head_dim:(h + 1) * head_dim],
                            k_ref[:, h * head_dim:(h + 1) * head_dim], _NT,
                            preferred_element_type=f32) * head_dim ** -0.5 for h in range(N_HEADS)]


def _store_front(chain, scores, grow_ref, gcol_ref, gsc_ref, s_ref):
    c_row, w_a, cols, m_new, s_old = chain
    grow_ref[0:N_HEADS, :] = c_row
    grow_ref[N_HEADS:2 * N_HEADS, :] = w_a
    gcol_ref[:, 0:cols.shape[1]] = cols
    for j, v in enumerate((m_new, s_old)):
        gsc_ref[j, 0:N_HEADS, :] = jnp.broadcast_to(v, (N_HEADS, gsc_ref.shape[2]))
    for h in range(N_HEADS):
        s_ref[h] = scores[h]


def _mlstm_out_kernel(q_ref, k_ref, v_ref, o_ref, z_ref, gates_ref, qn_ref, kn_ref, gates_next_ref, mhg_ref,
                      x_ref, cc_ref, wa_ref, wc_ref, fg_ref, out_ref,
                      c_ref, n_ref, hm_ref, grow_ref, gcol_ref, gsc_ref, s_ref,
                      *, head_dim, chunks_per_seq, final_norm):
    L = q_ref.shape[0]
    scale = head_dim ** -0.5
    step = pl.program_id(0)
    wr = step % 2
    rd = 1 - wr

    m_init = jnp.full((N_HEADS, 1), M_INIT, f32)

    @pl.when(step == 0)
    def _():
        hm_ref[...] = jnp.zeros_like(hm_ref)
        _store_front(_gate_chain(gates_ref[...], m_init), _scores(q_ref, k_ref, head_dim),
                     grow_ref, gcol_ref, gsc_ref, s_ref)

    @pl.when(step % chunks_per_seq == 0)
    def _():
        c_ref[...] = jnp.zeros_like(c_ref)
        n_ref[...] = jnp.zeros_like(n_ref)

    c_row = grow_ref[0:N_HEADS, :]
    w_a = grow_ref[N_HEADS:2 * N_HEADS, :]
    cols = gcol_ref[:, 0:6 * N_HEADS]
    m_new = gsc_ref[0, 0:N_HEADS, 0:1]
    s_old = gsc_ref[1, 0:N_HEADS, 0:1]
    next_chain = _gate_chain(gates_next_ref[...],
                             jnp.where((step + 1) % chunks_per_seq == 0, m_init, m_new))

    row_id = lax.broadcasted_iota(jnp.int32, (L, L), 0)
    col_id = lax.broadcasted_iota(jnp.int32, (L, L), 1)
    causal = col_id <= row_id

    d_out = out_ref.shape[1]
    tn = d_out // N_HEADS

    heads = range(N_HEADS)
    hs = [slice(h * head_dim, (h + 1) * head_dim) for h in heads]
    ys = []

    def out_proj_part(j):
        ol = slice(j * tn, (j + 1) * tn)
        ys.append(jnp.dot(hm_ref[rd], wa_ref[:, ol], preferred_element_type=f32)
                  + jnp.dot(cc_ref[...], wc_ref[:, ol], preferred_element_type=f32))

    o_gate = [_sigmoid(o_ref[:, hs[h]].astype(f32)) for h in heads]
    z_gate = [_silu(z_ref[:, hs[h]].astype(f32)) for h in heads]
    qn = [jnp.sum(q_ref[:, hs[h]].astype(f32) * n_ref[h:h + 1, :], axis=1, keepdims=True) for h in heads]

    s_ = [s_ref[h] for h in heads]
    c_prev = [c_ref[h] for h in heads]
    qc = [jnp.dot(q_ref[:, hs[h]], c_prev[h].astype(bf16), preferred_element_type=f32) for h in heads]

    p_, e_neg_, s_inter_ = [], [], []
    for h in heads:
        b_col = cols[:, h:h + 1]
        m_t = cols[:, 2 * N_HEADS + h:2 * N_HEADS + h + 1]
        g = jnp.where(causal, b_col + c_row[h:h + 1, :], -jnp.inf)
        p_.append(jnp.exp(g - m_t) * s_[h])
        s_inter_.append(cols[:, 3 * N_HEADS + h:3 * N_HEADS + h + 1])
        e_neg_.append(cols[:, 4 * N_HEADS + h:4 * N_HEADS + h + 1])

    pv = [jnp.dot(p_[h].astype(bf16), v_ref[:, hs[h]], preferred_element_type=f32) for h in heads]
    out_proj_part(0)
    out_proj_part(1)

    hm_new, vw = [], []
    for h in heads:
        num = pv[h] + s_inter_[h] * qc[h]
        den = jnp.sum(p_[h], axis=1, keepdims=True) + s_inter_[h] * qn[h]
        hh = num * (1.0 / jnp.maximum(jnp.abs(den), e_neg_[h]))
        hh = o_gate[h] * hh
        hh = hh * lax.rsqrt(jnp.mean(hh * hh, axis=-1, keepdims=True) + EPS)
        hh = hh * mhg_ref[:, hs[h]]
        hm_new.append((hh * z_gate[h]).astype(bf16))
        w_col = cols[:, N_HEADS + h:N_HEADS + h + 1]
        vw.append((v_ref[:, hs[h]].astype(f32) * w_col).astype(bf16))

    out_proj_part(2)
    kv = [lax.dot_general(k_ref[:, hs[h]], vw[h], _TN, preferred_element_type=f32) * scale
          for h in heads]
    out_proj_part(3)
    next_scores = _scores(qn_ref, kn_ref, head_dim)
    n_all = jnp.dot(jnp.concatenate([w_a, w_a], axis=0).astype(bf16), k_ref[...],
                    preferred_element_type=f32) * scale
    c_new = [s_old[h:h + 1, :] * c_prev[h] + kv[h] for h in heads]
    n_new = [s_old[h:h + 1, :] * n_ref[h:h + 1, :] + n_all[h:h + 1, hs[h]] for h in heads]

    y = x_ref[...] + jnp.concatenate(ys, axis=1)
    if final_norm:
        y = y * lax.rsqrt(jnp.mean(y * y, axis=-1, keepdims=True) + EPS) * fg_ref[...]
    out_ref[...] = y
    for h in heads:
        hm_ref[wr, :, hs[h]] = hm_new[h]
        c_ref[h] = c_new[h]
        n_ref[h:h + 1, :] = n_new[h]
    _store_front(next_chain, next_scores, grow_ref, gcol_ref, gsc_ref, s_ref)


def _mlstm_out(proj, gates, mhg, x2, cc, w_a, w_c, fg, *, seq, d_mlstm, final_norm):
    t, d = x2.shape
    L = MLSTM_CHUNK
    nc = seq // L
    n = t // L
    last = n - 1
    hd = d_mlstm // N_HEADS
    assert d % N_HEADS == 0 and seq % L == 0

    def col(j, ahead=0):
        return pl.BlockSpec((L, d_mlstm), lambda s, j=j: (jnp.minimum(s + ahead, last), j))

    def gate_rows(ahead):
        return pl.BlockSpec((2 * N_HEADS, L), lambda s: (0, jnp.minimum(s + ahead, last)))

    prev = lambda s: (jnp.maximum(s - 1, 0), 0)
    const = lambda s: (0, 0)
    return pl.pallas_call(
        functools.partial(_mlstm_out_kernel, head_dim=hd, chunks_per_seq=nc, final_norm=final_norm),
        grid=(n + 1,),
        in_specs=[col(0), col(1), col(2), col(3), col(4), gate_rows(0),
                  col(0, 1), col(1, 1), gate_rows(1),
                  pl.BlockSpec((1, d_mlstm), const),
                  pl.BlockSpec((L, d), prev),
                  pl.BlockSpec((L, cc.shape[1]), prev),
                  pl.BlockSpec(w_a.shape, const, pipeline_mode=pl.Buffered(1)),
                  pl.BlockSpec(w_c.shape, const, pipeline_mode=pl.Buffered(1)),
                  pl.BlockSpec((1, d), const)],
        out_specs=pl.BlockSpec((L, d), prev),
        out_shape=jax.ShapeDtypeStruct((t, d), f32),
        scratch_shapes=[pltpu.VMEM((N_HEADS, hd, hd), f32),
                        pltpu.VMEM((SUBLANES, hd), f32),
                        pltpu.VMEM((2, L, d_mlstm), bf16),
                        pltpu.VMEM((2 * N_HEADS, L), f32),
                        pltpu.VMEM((L, LANES), f32),
                        pltpu.VMEM((2, SUBLANES, LANES), f32),
                        pltpu.VMEM((N_HEADS, L, L), f32)],
        compiler_params=pltpu.CompilerParams(
            dimension_semantics=("arbitrary",), vmem_limit_bytes=VMEM_LIMIT),
        name="mlstm_out",
    )(proj, proj, proj, proj, proj, gates, proj, proj, gates, mhg, x2, cc, w_a, w_c, fg)


def kernel(x, norm_g, w_in, b_gates, mh_norm_g, conv_w, conv_b, conv_ln_g, conv_ln_b, w_out, final_norm_g):
    batch, seq, d_model = x.shape
    depth = norm_g.shape[0]
    d_mlstm = mh_norm_g.shape[1]
    d_conv = conv_b.shape[1]
    n_gate = 2 * N_HEADS
    g0 = 5 * d_mlstm

    h = x.reshape(batch * seq, d_model)
    for l in range(depth):
        w_t = jnp.swapaxes(w_in[l], 0, 1).astype(bf16)
        proj, gates, c = _proj_conv(h, norm_g[l][None, :], w_t, b_gates[l][:, None],
                                    conv_w[l], conv_b[l][None, :], conv_ln_g[l][None, :],
                                    conv_ln_b[l][None, :], seq=seq, d_conv=d_conv, dm=g0)
        wo = w_out[l].astype(bf16)
        h = _mlstm_out(proj, gates, mh_norm_g[l][None, :], h, c, wo[:d_mlstm], wo[d_mlstm:],
                       final_norm_g[None, :], seq=seq, d_mlstm=d_mlstm, final_norm=(l == depth - 1))
    return h.reshape(batch, seq, d_model)
```

```python
import functools

import jax
import jax.numpy as jnp
from jax import lax
from jax.experimental import pallas as pl
from jax.experimental.pallas import tpu as pltpu

N_HEADS = 4
CONV_WIDTH = 31
EPS = 1e-6
M_INIT = -1e30

MLSTM_CHUNK = 256
LANES = 128
SUBLANES = 8
CONV_HALO = 32
VMEM_LIMIT = 56 * 1024 * 1024

f32 = jnp.float32
bf16 = jnp.bfloat16

_NT = (((1,), (1,)), ((), ()))
_TN = (((0,), (0,)), ((), ()))


def _zero_bits(x):
    b = pltpu.bitcast(x, jnp.uint32)
    return lax.shift_right_logical(lax.shift_right_logical(b, jnp.uint32(16)), jnp.uint32(16))


def _sigmoid(x):
    return 0.5 * jnp.tanh(0.5 * x) + 0.5


def _silu(x):
    hx = 0.5 * x
    return hx * jnp.tanh(hx) + hx


def _conv_rows(x, w_ref, lanes, rb, dep):
    acc = None
    for r in range(SUBLANES):
        nq = (CONV_WIDTH - 1 - r) // SUBLANES + 1
        lo = 0 if r == 0 else SUBLANES
        p = None
        for q in range(nq):
            j = CONV_WIDTH - 1 - (SUBLANES * q + r)
            start = CONV_HALO - lo - SUBLANES * q
            wj = w_ref[j:j + 1, lanes]
            if dep is not None:
                wj = pltpu.bitcast(pltpu.bitcast(wj, jnp.uint32) | dep, f32)
            term = wj * x[start:start + rb + lo, :]
            p = term if p is None else p + term
        acc = p if r == 0 else acc + p[SUBLANES - r:SUBLANES - r + rb, :]
        dep = _zero_bits(acc[0:1, :])
    return acc, dep


def _proj_conv_kernel(x_ref, g_ref, wm_ref, wc_ref, wg_ref, bg_ref, cw_ref, cb_ref, lng_ref, lnb_ref,
                      proj_ref, gates_ref, c_ref,
                      u_ref, ag_ref, z_ref, buf_ref, acc_ref, *, tiles_per_seq, tn, rb):
    i = pl.program_id(0)
    tm, d = x_ref.shape
    dc = c_ref.shape[1]
    halo = CONV_HALO

    @pl.when(i == 0)
    def _():
        z_ref[...] = jnp.zeros_like(z_ref)
        buf_ref[...] = jnp.zeros_like(buf_ref)

    x = x_ref[...]
    u = x * lax.rsqrt(jnp.mean(x * x, axis=-1, keepdims=True) + EPS) * g_ref[...]
    u_ref[...] = u.astype(bf16)
    gates_ref[...] = lax.dot_general(wg_ref[...], u_ref[...], _NT, preferred_element_type=f32) + bg_ref[...]

    dm = wm_ref.shape[0]

    def matmul(col):
        w = wm_ref[col:col + tn, :] if col < dm else wc_ref[col - dm:col - dm + tn, :]
        return lax.dot_general(u_ref[...], w, _NT, preferred_element_type=f32).astype(bf16)

    def glu_item(r0, n, with_history):
        def compute():
            half = jnp.full((1, dc), 0.5, f32)
            ys = []
            for p0 in range(r0, r0 + n, rb):
                a = ag_ref[p0:p0 + rb, 0:dc].astype(f32)
                g = ag_ref[p0:p0 + rb, dc:2 * dc].astype(f32)
                y = a * (half * jnp.tanh(half * g) + half)
                half = pltpu.bitcast(pltpu.bitcast(half, jnp.uint32) | _zero_bits(y[0:1, :]), f32)
                ys.append(y)
            if with_history:
                tail = buf_ref[tm:tm + halo, :]
                ys.append(jnp.where(i % tiles_per_seq == 0, jnp.zeros_like(tail), tail))
            return ys

        def store(ys):
            for j in range(n // rb):
                buf_ref[halo + r0 + j * rb:halo + r0 + (j + 1) * rb, :] = ys[j]
            if with_history:
                buf_ref[0:halo, :] = ys[-1]
        return compute, store

    def conv_item(units):
        def compute():
            ys, dep = [], None
            for t0, l0 in units:
                x = buf_ref[t0:t0 + halo + rb, l0:l0 + LANES]
                y, dep = _conv_rows(x, cw_ref, slice(l0, l0 + LANES), rb, dep)
                ys.append(y + cb_ref[:, l0:l0 + LANES])
            return ys

        def store(ys):
            for y, (t0, l0) in zip(ys, units):
                acc_ref[t0:t0 + rb, l0:l0 + LANES] = y
        return compute, store

    def ln_item(r0, n):
        def compute():
            outs = []
            gain = lng_ref[...]
            for p0 in range(r0, r0 + n, rb):
                y = acc_ref[p0:p0 + rb, :]
                mu = jnp.mean(y, axis=-1, keepdims=True)
                yc = y - mu
                var = jnp.mean(yc * yc, axis=-1, keepdims=True)
                yn = yc * lax.rsqrt(var + EPS) * gain + lnb_ref[...]
                out = _silu(yn) * _silu(z_ref[p0:p0 + rb, :].astype(f32))
                gain = pltpu.bitcast(pltpu.bitcast(gain, jnp.uint32) | _zero_bits(out[0:1, :]), f32)
                outs.append(out.astype(c_ref.dtype))
            return outs

        def store(outs):
            for j, out in enumerate(outs):
                c_ref[r0 + j * rb:r0 + (j + 1) * rb, :] = out
        return compute, store

    def to_proj(col, val):
        proj_ref[:, col:col + tn] = val

    def to_ag(col, val):
        ag_ref[:, col - a0:col - a0 + tn] = val

    def to_z(col, val):
        z_ref[:, col - zc0:col - zc0 + tn] = val

    q0, k0, v0, o0, zm0, a0, g0, zc0 = (j * d for j in range(8))
    assert tn == d and tm == 8 * rb
    mm_order = [(to_proj, q0), (to_proj, k0), (to_ag, a0), (to_ag, g0),
                (to_proj, v0), (to_proj, o0), (to_proj, zm0), (to_z, zc0)]
    units = [(t0, l0) for t0 in range(0, tm, rb) for l0 in range(0, dc, LANES)]
    n_units = (11, 9, 9, 9, 9, 9, 8)
    cuts = [sum(n_units[:j]) for j in range(len(n_units) + 1)]
    assert cuts[-1] == len(units)
    per_row_block = dc // LANES
    work = []
    for j in range(len(n_units)):
        items = [conv_item(units[cuts[j]:cuts[j + 1]])]
        if j >= 1:
            assert cuts[j] >= j * per_row_block
            items.append(ln_item((j - 1) * rb, rb))
        work.append(items)
    done = len(n_units) - 1
    assert cuts[done + 1] >= min(((tm // 2) // rb + 1) * per_row_block, len(units))
    work[done].append(glu_item(0, tm // 2, True))
    work.append([ln_item(done * rb, tm - done * rb), glu_item(tm // 2, tm // 2, False)])

    once = jnp.minimum(i, 0) + 1

    for (mm_store, col), items in zip(mm_order, work):
        def region(_, carry, mm_store=mm_store, col=col, items=items):
            outs = [compute() for compute, _ in items]
            res = matmul(col)
            mm_store(col, res)
            for (_, store), out in zip(items, outs):
                store(out)
            return carry
        lax.fori_loop(0, once, region, 0)


def _proj_conv(x2, g, w_t, bg, conv_w, conv_b, ln_g, ln_b, *, seq, d_conv, dm, tm=512, tn=1024, rb=64):
    t, d = x2.shape
    n_gate = 2 * N_HEADS
    assert (d == d_conv and dm == 5 * d and w_t.shape[0] == dm + n_gate + 3 * d_conv and d % tn == 0
            and seq % tm == 0 and tm % rb == 0)
    n = t // tm
    last = n - 1
    const = lambda i: (0, 0)
    vec = pl.BlockSpec((1, d_conv), const)
    return pl.pallas_call(
        functools.partial(_proj_conv_kernel, tiles_per_seq=seq // tm, tn=tn, rb=rb),
        grid=(n + 1,),
        in_specs=[
            pl.BlockSpec((tm, d), lambda i: (jnp.minimum(i, last), 0)),
            pl.BlockSpec((1, d), const),
            pl.BlockSpec((pl.Element(dm), pl.Element(d)), const, pipeline_mode=pl.Buffered(1)),
            pl.BlockSpec((pl.Element(3 * d_conv), pl.Element(d)), lambda i: (dm + n_gate, 0),
                         pipeline_mode=pl.Buffered(1)),
            pl.BlockSpec((pl.Element(n_gate), pl.Element(d)), lambda i: (dm, 0)),
            pl.BlockSpec((2 * N_HEADS, 1), const),
            pl.BlockSpec((CONV_WIDTH, d_conv), const),
            vec, vec, vec,
        ],
        out_specs=[
            pl.BlockSpec((tm, dm), lambda i: (jnp.minimum(i, last), 0)),
            pl.BlockSpec((2 * N_HEADS, tm), lambda i: (0, jnp.minimum(i, last))),
            pl.BlockSpec((tm, d_conv), lambda i: (jnp.maximum(i - 1, 0), 0)),
        ],
        out_shape=[
            jax.ShapeDtypeStruct((t, dm), bf16),
            jax.ShapeDtypeStruct((2 * N_HEADS, t), f32),
            jax.ShapeDtypeStruct((t, d_conv), bf16),
        ],
        scratch_shapes=[
            pltpu.VMEM((tm, d), bf16),
            pltpu.VMEM((tm, 2 * d_conv), bf16),
            pltpu.VMEM((tm, d_conv), bf16),
            pltpu.VMEM((tm + CONV_HALO, d_conv), f32),
            pltpu.VMEM((tm, d_conv), f32),
        ],
        compiler_params=pltpu.CompilerParams(
            dimension_semantics=("arbitrary",), vmem_limit_bytes=VMEM_LIMIT),
        name="proj_conv",
    )(x2, g, w_t, w_t, w_t, bg, conv_w, conv_b, ln_g, ln_b)


def _cumsum_lanes(x):
    rows, n = x.shape
    tri = (lax.broadcasted_iota(jnp.int32, (n, n), 0)
           <= lax.broadcasted_iota(jnp.int32, (n, n), 1)).astype(bf16)
    hi = x.astype(bf16).astype(f32)
    mid = (x - hi).astype(bf16).astype(f32)
    lo = x - hi - mid
    parts = jnp.dot(jnp.concatenate([hi, mid, lo], axis=0).astype(bf16), tri, preferred_element_type=f32)
    return parts[0:rows] + parts[rows:2 * rows] + parts[2 * rows:3 * rows]


def _gate_chain(gr, m_prev):
    L = gr.shape[1]
    lf = jnp.minimum(gr, 0.0) - jnp.log1p(jnp.exp(-jnp.abs(gr)))
    bfull = _cumsum_lanes(lf)
    ig = gr[0:N_HEADS]
    b = bfull[N_HEADS:2 * N_HEADS]
    b_last = b[:, L - 1:L]
    a = b_last - b + ig
    a_max = jnp.max(a, axis=1, keepdims=True)
    m_new = jnp.maximum(b_last + m_prev, a_max)
    s_old = jnp.exp(b_last + m_prev - m_new)
    w_a = jnp.exp(a - m_new)
    c_row = ig - b
    run = jnp.concatenate([c_row, c_row], axis=0)
    lane = lax.broadcasted_iota(jnp.int32, run.shape, 1)
    shift = 1
    while shift < L:
        run = jnp.maximum(run, jnp.where(lane >= shift, pltpu.roll(run, shift, 1), -jnp.inf))
        shift *= 2
    run = run[0:N_HEADS]
    li = b + m_prev
    m_t = jnp.maximum(li, b + run)
    s_inter = jnp.exp(li - m_t)
    e_neg = jnp.exp(-m_t)
    cols = jnp.concatenate([jnp.transpose(jnp.concatenate([b, w_a], axis=0)),
                            jnp.transpose(jnp.concatenate([m_t, s_inter], axis=0)),
                            jnp.transpose(jnp.concatenate([e_neg, e_neg], axis=0))], axis=1)
    return c_row, w_a, cols, m_new, s_old


def _scores(q_ref, k_ref, head_dim):
    return [lax.dot_general(q_ref[:, h * head_dim:(h + 1) * head_dim],
                            k_ref[:, h * head_dim:(h + 1) * head_dim], _NT,
                            preferred_element_type=f32) * head_dim ** -0.5 for h in range(N_HEADS)]


def _store_front(chain, scores, grow_ref, gcol_ref, gsc_ref, s_ref):
    c_row, w_a, cols, m_new, s_old = chain
    grow_ref[0:N_HEADS, :] = c_row
    grow_ref[N_HEADS:2 * N_HEADS, :] = w_a
    gcol_ref[:, 0:cols.shape[1]] = cols
    for j, v in enumerate((m_new, s_old)):
        gsc_ref[j, 0:N_HEADS, :] = jnp.broadcast_to(v, (N_HEADS, gsc_ref.shape[2]))
    for h in range(N_HEADS):
        s_ref[h] = scores[h]


def _mlstm_out_kernel(q_ref, k_ref, v_ref, o_ref, z_ref, gates_ref, qn_ref, kn_ref, gates_next_ref, mhg_ref,
                      x_ref, cc_ref, wa_ref, wc_ref, fg_ref, out_ref,
                      c_ref, n_ref, hm_ref, grow_ref, gcol_ref, gsc_ref, s_ref,
                      *, head_dim, chunks_per_seq, final_norm):
    L = q_ref.shape[0]
    scale = head_dim ** -0.5
    step = pl.program_id(0)
    wr = step % 2
    rd = 1 - wr

    m_init = jnp.full((N_HEADS, 1), M_INIT, f32)

    @pl.when(step == 0)
    def _():
        hm_ref[...] = jnp.zeros_like(hm_ref)
        _store_front(_gate_chain(gates_ref[...], m_init), _scores(q_ref, k_ref, head_dim),
                     grow_ref, gcol_ref, gsc_ref, s_ref)

    @pl.when(step % chunks_per_seq == 0)
    def _():
        c_ref[...] = jnp.zeros_like(c_ref)
        n_ref[...] = jnp.zeros_like(n_ref)

    c_row = grow_ref[0:N_HEADS, :]
    w_a = grow_ref[N_HEADS:2 * N_HEADS, :]
    cols = gcol_ref[:, 0:6 * N_HEADS]
    m_new = gsc_ref[0, 0:N_HEADS, 0:1]
    s_old = gsc_ref[1, 0:N_HEADS, 0:1]
    next_chain = _gate_chain(gates_next_ref[...],
                             jnp.where((step + 1) % chunks_per_seq == 0, m_init, m_new))

    row_id = lax.broadcasted_iota(jnp.int32, (L, L), 0)
    col_id = lax.broadcasted_iota(jnp.int32, (L, L), 1)
    causal = col_id <= row_id

    d_out = out_ref.shape[1]
    tn = d_out // N_HEADS

    heads = range(N_HEADS)
    hs = [slice(h * head_dim, (h + 1) * head_dim) for h in heads]
    ys = []

    def out_proj_part(j):
        ol = slice(j * tn, (j + 1) * tn)
        ys.append(jnp.dot(hm_ref[rd], wa_ref[:, ol], preferred_element_type=f32)
                  + jnp.dot(cc_ref[...], wc_ref[:, ol], preferred_element_type=f32))

    o_gate = [_sigmoid(o_ref[:, hs[h]].astype(f32)) for h in heads]
    z_gate = [_silu(z_ref[:, hs[h]].astype(f32)) for h in heads]
    qn = [jnp.sum(q_ref[:, hs[h]].astype(f32) * n_ref[h:h + 1, :], axis=1, keepdims=True) for h in heads]

    s_ = [s_ref[h] for h in heads]
    c_prev = [c_ref[h] for h in heads]
    qc = [jnp.dot(q_ref[:, hs[h]], c_prev[h].astype(bf16), preferred_element_type=f32) for h in heads]
    next_scores = _scores(qn_ref, kn_ref, head_dim)
    out_proj_part(0)

    p_, e_neg_, s_inter_ = [], [], []
    for h in heads:
        b_col = cols[:, h:h + 1]
        m_t = cols[:, 2 * N_HEADS + h:2 * N_HEADS + h + 1]
        g = jnp.where(causal, b_col + c_row[h:h + 1, :], -jnp.inf)
        p_.append(jnp.exp(g - m_t) * s_[h])
        s_inter_.append(cols[:, 3 * N_HEADS + h:3 * N_HEADS + h + 1])
        e_neg_.append(cols[:, 4 * N_HEADS + h:4 * N_HEADS + h + 1])

    pv = [jnp.dot(p_[h].astype(bf16), v_ref[:, hs[h]], preferred_element_type=f32) for h in heads]
    out_proj_part(1)

    hm_new, vw = [], []
    for h in heads:
        num = pv[h] + s_inter_[h] * qc[h]
        den = jnp.sum(p_[h], axis=1, keepdims=True) + s_inter_[h] * qn[h]
        hh = num * (1.0 / jnp.maximum(jnp.abs(den), e_neg_[h]))
        hh = o_gate[h] * hh
        hh = hh * lax.rsqrt(jnp.mean(hh * hh, axis=-1, keepdims=True) + EPS)
        hh = hh * mhg_ref[:, hs[h]]
        hm_new.append((hh * z_gate[h]).astype(bf16))
        w_col = cols[:, N_HEADS + h:N_HEADS + h + 1]
        vw.append((v_ref[:, hs[h]].astype(f32) * w_col).astype(bf16))

    out_proj_part(2)
    kv = [lax.dot_general(k_ref[:, hs[h]], vw[h], _TN, preferred_element_type=f32) * scale
          for h in heads]
    out_proj_part(3)
    n_all = jnp.dot(jnp.concatenate([w_a, w_a], axis=0).astype(bf16), k_ref[...],
                    preferred_element_type=f32) * scale
    c_new = [s_old[h:h + 1, :] * c_prev[h] + kv[h] for h in heads]
    n_new = [s_old[h:h + 1, :] * n_ref[h:h + 1, :] + n_all[h:h + 1, hs[h]] for h in heads]

    y = x_ref[...] + jnp.concatenate(ys, axis=1)
    if final_norm:
        y = y * lax.rsqrt(jnp.mean(y * y, axis=-1, keepdims=True) + EPS) * fg_ref[...]
    out_ref[...] = y
    for h in heads:
        hm_ref[wr, :, hs[h]] = hm_new[h]
        c_ref[h] = c_new[h]
        n_ref[h:h + 1, :] = n_new[h]
    _store_front(next_chain, next_scores, grow_ref, gcol_ref, gsc_ref, s_ref)


def _mlstm_out(proj, gates, mhg, x2, cc, w_a, w_c, fg, *, seq, d_mlstm, final_norm):
    t, d = x2.shape
    L = MLSTM_CHUNK
    nc = seq // L
    n = t // L
    last = n - 1
    hd = d_mlstm // N_HEADS
    assert d % N_HEADS == 0 and seq % L == 0

    def col(j, ahead=0):
        return pl.BlockSpec((L, d_mlstm), lambda s, j=j: (jnp.minimum(s + ahead, last), j))

    def gate_rows(ahead):
        return pl.BlockSpec((2 * N_HEADS, L), lambda s: (0, jnp.minimum(s + ahead, last)))

    prev = lambda s: (jnp.maximum(s - 1, 0), 0)
    const = lambda s: (0, 0)
    return pl.pallas_call(
        functools.partial(_mlstm_out_kernel, head_dim=hd, chunks_per_seq=nc, final_norm=final_norm),
        grid=(n + 1,),
        in_specs=[col(0), col(1), col(2), col(3), col(4), gate_rows(0),
                  col(0, 1), col(1, 1), gate_rows(1),
                  pl.BlockSpec((1, d_mlstm), const),
                  pl.BlockSpec((L, d), prev),
                  pl.BlockSpec((L, cc.shape[1]), prev),
                  pl.BlockSpec(w_a.shape, const, pipeline_mode=pl.Buffered(1)),
                  pl.BlockSpec(w_c.shape, const, pipeline_mode=pl.Buffered(1)),
                  pl.BlockSpec((1, d), const)],
        out_specs=pl.BlockSpec((L, d), prev),
        out_shape=jax.ShapeDtypeStruct((t, d), f32),
        scratch_shapes=[pltpu.VMEM((N_HEADS, hd, hd), f32),
                        pltpu.VMEM((SUBLANES, hd), f32),
                        pltpu.VMEM((2, L, d_mlstm), bf16),
                        pltpu.VMEM((2 * N_HEADS, L), f32),
                        pltpu.VMEM((L, LANES), f32),
                        pltpu.VMEM((2, SUBLANES, LANES), f32),
                        pltpu.VMEM((N_HEADS, L, L), f32)],
        compiler_params=pltpu.CompilerParams(
            dimension_semantics=("arbitrary",), vmem_limit_bytes=VMEM_LIMIT),
        name="mlstm_out",
    )(proj, proj, proj, proj, proj, gates, proj, proj, gates, mhg, x2, cc, w_a, w_c, fg)


def kernel(x, norm_g, w_in, b_gates, mh_norm_g, conv_w, conv_b, conv_ln_g, conv_ln_b, w_out, final_norm_g):
    batch, seq, d_model = x.shape
    depth = norm_g.shape[0]
    d_mlstm = mh_norm_g.shape[1]
    d_conv = conv_b.shape[1]
    n_gate = 2 * N_HEADS
    g0 = 5 * d_mlstm

    h = x.reshape(batch * seq, d_model)
    for l in range(depth):
        w_t = jnp.swapaxes(w_in[l], 0, 1).astype(bf16)
        proj, gates, c = _proj_conv(h, norm_g[l][None, :], w_t, b_gates[l][:, None],
                                    conv_w[l], conv_b[l][None, :], conv_ln_g[l][None, :],
                                    conv_ln_b[l][None, :], seq=seq, d_conv=d_conv, dm=g0)
        wo = w_out[l].astype(bf16)
        h = _mlstm_out(proj, gates, mh_norm_g[l][None, :], h, c, wo[:d_mlstm], wo[d_mlstm:],
                       final_norm_g[None, :], seq=seq, d_mlstm=d_mlstm, final_norm=(l == depth - 1))
    return h.reshape(batch, seq, d_model)
```

```python
import functools

import jax
import jax.numpy as jnp
from jax import lax
from jax.experimental import pallas as pl
from jax.experimental.pallas import tpu as pltpu

N_HEADS = 4
CONV_WIDTH = 31
EPS = 1e-6
M_INIT = -1e30

MLSTM_CHUNK = 256
LANES = 128
SUBLANES = 8
CONV_HALO = 32
VMEM_LIMIT = 56 * 1024 * 1024

f32 = jnp.float32
bf16 = jnp.bfloat16

_NT = (((1,), (1,)), ((), ()))
_TN = (((0,), (0,)), ((), ()))


def _zero_bits(x):
    b = pltpu.bitcast(x, jnp.uint32)
    return lax.shift_right_logical(lax.shift_right_logical(b, jnp.uint32(16)), jnp.uint32(16))


def _sigmoid(x):
    return 0.5 * jnp.tanh(0.5 * x) + 0.5


def _silu(x):
    hx = 0.5 * x
    return hx * jnp.tanh(hx) + hx


def _conv_rows(x, w_ref, lanes, rb, dep):
    acc = None
    for r in range(SUBLANES):
        nq = (CONV_WIDTH - 1 - r) // SUBLANES + 1
        lo = 0 if r == 0 else SUBLANES
        p = None
        for q in range(nq):
            j = CONV_WIDTH - 1 - (SUBLANES * q + r)
            start = CONV_HALO - lo - SUBLANES * q
            wj = w_ref[j:j + 1, lanes]
            if dep is not None:
                wj = pltpu.bitcast(pltpu.bitcast(wj, jnp.uint32) | dep, f32)
            term = wj * x[start:start + rb + lo, :]
            p = term if p is None else p + term
        acc = p if r == 0 else acc + p[SUBLANES - r:SUBLANES - r + rb, :]
        dep = _zero_bits(acc[0:1, :])
    return acc, dep


def _proj_conv_kernel(x_ref, g_ref, wm_ref, wc_ref, wg_ref, bg_ref, cw_ref, cb_ref, lng_ref, lnb_ref,
                      proj_ref, gates_ref, c_ref,
                      u_ref, ag_ref, z_ref, buf_ref, acc_ref, *, tiles_per_seq, tn, rb):
    i = pl.program_id(0)
    tm, d = x_ref.shape
    dc = c_ref.shape[1]
    halo = CONV_HALO

    @pl.when(i == 0)
    def _():
        z_ref[...] = jnp.zeros_like(z_ref)
        buf_ref[...] = jnp.zeros_like(buf_ref)

    x = x_ref[...]
    u = x * lax.rsqrt(jnp.mean(x * x, axis=-1, keepdims=True) + EPS) * g_ref[...]
    u_ref[...] = u.astype(bf16)
    gates_ref[...] = lax.dot_general(wg_ref[...], u_ref[...], _NT, preferred_element_type=f32) + bg_ref[...]

    dm = wm_ref.shape[0]

    def matmul(col):
        w = wm_ref[col:col + tn, :] if col < dm else wc_ref[col - dm:col - dm + tn, :]
        return lax.dot_general(u_ref[...], w, _NT, preferred_element_type=f32).astype(bf16)

    def glu_item(r0, n, with_history):
        def compute():
            half = jnp.full((1, dc), 0.5, f32)
            ys = []
            for p0 in range(r0, r0 + n, rb):
                a = ag_ref[p0:p0 + rb, 0:dc].astype(f32)
                g = ag_ref[p0:p0 + rb, dc:2 * dc].astype(f32)
                y = a * (half * jnp.tanh(half * g) + half)
                half = pltpu.bitcast(pltpu.bitcast(half, jnp.uint32) | _zero_bits(y[0:1, :]), f32)
                ys.append(y)
            if with_history:
                tail = buf_ref[tm:tm + halo, :]
                ys.append(jnp.where(i % tiles_per_seq == 0, jnp.zeros_like(tail), tail))
            return ys

        def store(ys):
            for j in range(n // rb):
                buf_ref[halo + r0 + j * rb:halo + r0 + (j + 1) * rb, :] = ys[j]
            if with_history:
                buf_ref[0:halo, :] = ys[-1]
        return compute, store

    def conv_item(units):
        def compute():
            ys, dep = [], None
            for t0, l0 in units:
                x = buf_ref[t0:t0 + halo + rb, l0:l0 + LANES]
                y, dep = _conv_rows(x, cw_ref, slice(l0, l0 + LANES), rb, dep)
                ys.append(y + cb_ref[:, l0:l0 + LANES])
            return ys

        def store(ys):
            for y, (t0, l0) in zip(ys, units):
                acc_ref[t0:t0 + rb, l0:l0 + LANES] = y
        return compute, store

    def ln_item(r0, n):
        def compute():
            outs = []
            gain = lng_ref[...]
            for p0 in range(r0, r0 + n, rb):
                y = acc_ref[p0:p0 + rb, :]
                mu = jnp.mean(y, axis=-1, keepdims=True)
                yc = y - mu
                var = jnp.mean(yc * yc, axis=-1, keepdims=True)
                yn = yc * lax.rsqrt(var + EPS) * gain + lnb_ref[...]
                out = _silu(yn) * _silu(z_ref[p0:p0 + rb, :].astype(f32))
                gain = pltpu.bitcast(pltpu.bitcast(gain, jnp.uint32) | _zero_bits(out[0:1, :]), f32)
                outs.append(out.astype(c_ref.dtype))
            return outs

        def store(outs):
            for j, out in enumerate(outs):
                c_ref[r0 + j * rb:r0 + (j + 1) * rb, :] = out
        return compute, store

    def to_proj(col, val):
        proj_ref[:, col:col + tn] = val

    def to_ag(col, val):
        ag_ref[:, col - a0:col - a0 + tn] = val

    def to_z(col, val):
        z_ref[:, col - zc0:col - zc0 + tn] = val

    q0, k0, v0, o0, zm0, a0, g0, zc0 = (j * d for j in range(8))
    assert tn == d and tm == 8 * rb
    mm_order = [(to_proj, q0), (to_proj, k0), (to_ag, a0), (to_ag, g0),
                (to_proj, v0), (to_proj, o0), (to_proj, zm0), (to_z, zc0)]
    units = [(t0, l0) for t0 in range(0, tm, rb) for l0 in range(0, dc, LANES)]
    n_units = (11, 9, 9, 9, 9, 9, 8)
    cuts = [sum(n_units[:j]) for j in range(len(n_units) + 1)]
    assert cuts[-1] == len(units)
    per_row_block = dc // LANES
    work = []
    for j in range(len(n_units)):
        items = [conv_item(units[cuts[j]:cuts[j + 1]])]
        if j >= 1:
            assert cuts[j] >= j * per_row_block
            items.append(ln_item((j - 1) * rb, rb))
        work.append(items)
    done = len(n_units) - 1
    assert cuts[done + 1] >= min(((tm // 2) // rb + 1) * per_row_block, len(units))
    work[done].append(glu_item(0, tm // 2, True))
    work.append([ln_item(done * rb, tm - done * rb), glu_item(tm // 2, tm // 2, False)])

    once = jnp.minimum(i, 0) + 1

    for (mm_store, col), items in zip(mm_order, work):
        def region(_, carry, mm_store=mm_store, col=col, items=items):
            outs = [compute() for compute, _ in items]
            res = matmul(col)
            mm_store(col, res)
            for (_, store), out in zip(items, outs):
                store(out)
            return carry
        lax.fori_loop(0, once, region, 0)


def _proj_conv(x2, g, w_t, bg, conv_w, conv_b, ln_g, ln_b, *, seq, d_conv, dm, tm=512, tn=1024, rb=64):
    t, d = x2.shape
    n_gate = 2 * N_HEADS
    assert (d == d_conv and dm == 5 * d and w_t.shape[0] == dm + n_gate + 3 * d_conv and d % tn == 0
            and seq % tm == 0 and tm % rb == 0)
    n = t // tm
    last = n - 1
    const = lambda i: (0, 0)
    vec = pl.BlockSpec((1, d_conv), const)
    return pl.pallas_call(
        functools.partial(_proj_conv_kernel, tiles_per_seq=seq // tm, tn=tn, rb=rb),
        grid=(n + 1,),
        in_specs=[
            pl.BlockSpec((tm, d), lambda i: (jnp.minimum(i, last), 0)),
            pl.BlockSpec((1, d), const),
            pl.BlockSpec((pl.Element(dm), pl.Element(d)), const, pipeline_mode=pl.Buffered(1)),
            pl.BlockSpec((pl.Element(3 * d_conv), pl.Element(d)), lambda i: (dm + n_gate, 0),
                         pipeline_mode=pl.Buffered(1)),
            pl.BlockSpec((pl.Element(n_gate), pl.Element(d)), lambda i: (dm, 0)),
            pl.BlockSpec((2 * N_HEADS, 1), const),
            pl.BlockSpec((CONV_WIDTH, d_conv), const),
            vec, vec, vec,
        ],
        out_specs=[
            pl.BlockSpec((tm, dm), lambda i: (jnp.minimum(i, last), 0)),
            pl.BlockSpec((2 * N_HEADS, tm), lambda i: (0, jnp.minimum(i, last))),
            pl.BlockSpec((tm, d_conv), lambda i: (jnp.maximum(i - 1, 0), 0)),
        ],
        out_shape=[
            jax.ShapeDtypeStruct((t, dm), bf16),
            jax.ShapeDtypeStruct((2 * N_HEADS, t), f32),
            jax.ShapeDtypeStruct((t, d_conv), bf16),
        ],
        scratch_shapes=[
            pltpu.VMEM((tm, d), bf16),
            pltpu.VMEM((tm, 2 * d_conv), bf16),
            pltpu.VMEM((tm, d_conv), bf16),
            pltpu.VMEM((tm + CONV_HALO, d_conv), f32),
            pltpu.VMEM((tm, d_conv), f32),
        ],
        compiler_params=pltpu.CompilerParams(
            dimension_semantics=("arbitrary",), vmem_limit_bytes=VMEM_LIMIT),
        name="proj_conv",
    )(x2, g, w_t, w_t, w_t, bg, conv_w, conv_b, ln_g, ln_b)


def _cumsum_lanes(x):
    rows, n = x.shape
    tri = (lax.broadcasted_iota(jnp.int32, (n, n), 0)
           <= lax.broadcasted_iota(jnp.int32, (n, n), 1)).astype(bf16)
    hi = x.astype(bf16).astype(f32)
    mid = (x - hi).astype(bf16).astype(f32)
    lo = x - hi - mid
    parts = jnp.dot(jnp.concatenate([hi, mid, lo], axis=0).astype(bf16), tri, preferred_element_type=f32)
    return parts[0:rows] + parts[rows:2 * rows] + parts[2 * rows:3 * rows]


def _gate_chain(gr, m_prev):
    L = gr.shape[1]
    lf = jnp.minimum(gr, 0.0) - jnp.log1p(jnp.exp(-jnp.abs(gr)))
    bfull = _cumsum_lanes(lf)
    ig = gr[0:N_HEADS]
    b = bfull[N_HEADS:2 * N_HEADS]
    b_last = b[:, L - 1:L]
    a = b_last - b + ig
    a_max = jnp.max(a, axis=1, keepdims=True)
    m_new = jnp.maximum(b_last + m_prev, a_max)
    s_old = jnp.exp(b_last + m_prev - m_new)
    w_a = jnp.exp(a - m_new)
    c_row = ig - b
    run = jnp.concatenate([c_row, c_row], axis=0)
    lane = lax.broadcasted_iota(jnp.int32, run.shape, 1)
    shift = 1
    while shift < L:
        run = jnp.maximum(run, jnp.where(lane >= shift, pltpu.roll(run, shift, 1), -jnp.inf))
        shift *= 2
    run = run[0:N_HEADS]
    li = b + m_prev
    m_t = jnp.maximum(li, b + run)
    s_inter = jnp.exp(li - m_t)
    e_neg = jnp.exp(-m_t)
    cols = jnp.concatenate([jnp.transpose(jnp.concatenate([b, w_a], axis=0)),
                            jnp.transpose(jnp.concatenate([m_t, s_inter], axis=0)),
                            jnp.transpose(jnp.concatenate([e_neg, e_neg], axis=0))], axis=1)
    return c_row, w_a, cols, m_new, s_old


def _scores(q_ref, k_ref, head_dim):
    return [lax.dot_general(q_ref[:, h * head_dim:(h + 1) * head_dim],
                            k_ref[:, h * head_dim:(h + 1) * head_dim], _NT,
                            preferred_element_type=f32) * head_dim ** -0.5 for h in range(N_HEADS)]


def _store_front(chain, scores, grow_ref, gcol_ref, gsc_ref, s_ref):
    c_row, w_a, cols, m_new, s_old = chain
    grow_ref[0:N_HEADS, :] = c_row
    grow_ref[N_HEADS:2 * N_HEADS, :] = w_a
    gcol_ref[:, 0:cols.shape[1]] = cols
    for j, v in enumerate((m_new, s_old)):
        gsc_ref[j, 0:N_HEADS, :] = jnp.broadcast_to(v, (N_HEADS, gsc_ref.shape[2]))
    for h in range(N_HEADS):
        s_ref[h] = scores[h]


def _mlstm_out_kernel(q_ref, k_ref, v_ref, o_ref, z_ref, gates_ref, qn_ref, kn_ref, gates_next_ref, mhg_ref,
                      x_ref, cc_ref, wa_ref, wc_ref, fg_ref, out_ref,
                      c_ref, n_ref, hm_ref, grow_ref, gcol_ref, gsc_ref, s_ref,
                      *, head_dim, chunks_per_seq, final_norm):
    L = q_ref.shape[0]
    scale = head_dim ** -0.5
    step = pl.program_id(0)
    wr = step % 2
    rd = 1 - wr

    m_init = jnp.full((N_HEADS, 1), M_INIT, f32)

    @pl.when(step == 0)
    def _():
        hm_ref[...] = jnp.zeros_like(hm_ref)
        _store_front(_gate_chain(gates_ref[...], m_init), _scores(q_ref, k_ref, head_dim),
                     grow_ref, gcol_ref, gsc_ref, s_ref)

    @pl.when(step % chunks_per_seq == 0)
    def _():
        c_ref[...] = jnp.zeros_like(c_ref)
        n_ref[...] = jnp.zeros_like(n_ref)

    c_row = grow_ref[0:N_HEADS, :]
    w_a = grow_ref[N_HEADS:2 * N_HEADS, :]
    cols = gcol_ref[:, 0:6 * N_HEADS]
    m_new = gsc_ref[0, 0:N_HEADS, 0:1]
    s_old = gsc_ref[1, 0:N_HEADS, 0:1]
    next_chain = _gate_chain(gates_next_ref[...],
                             jnp.where((step + 1) % chunks_per_seq == 0, m_init, m_new))

    row_id = lax.broadcasted_iota(jnp.int32, (L, L), 0)
    col_id = lax.broadcasted_iota(jnp.int32, (L, L), 1)
    causal = col_id <= row_id

    d_out = out_ref.shape[1]
    tn = d_out // N_HEADS

    heads = range(N_HEADS)
    hs = [slice(h * head_dim, (h + 1) * head_dim) for h in heads]
    ys = []

    def out_proj_part(j):
        ol = slice(j * tn, (j + 1) * tn)
        ys.append(jnp.dot(hm_ref[rd], wa_ref[:, ol], preferred_element_type=f32)
                  + jnp.dot(cc_ref[...], wc_ref[:, ol], preferred_element_type=f32))

    o_gate = [_sigmoid(o_ref[:, hs[h]].astype(f32)) for h in heads]
    z_gate = [_silu(z_ref[:, hs[h]].astype(f32)) for h in heads]
    qn = [jnp.sum(q_ref[:, hs[h]].astype(f32) * n_ref[h:h + 1, :], axis=1, keepdims=True) for h in heads]

    s_ = [s_ref[h] for h in heads]
    c_prev = [c_ref[h] for h in heads]
    qc = [jnp.dot(q_ref[:, hs[h]], c_prev[h].astype(bf16), preferred_element_type=f32) for h in heads]
    next_scores = _scores(qn_ref, kn_ref, head_dim)
    vw = [(v_ref[:, hs[h]].astype(f32) * cols[:, N_HEADS + h:N_HEADS + h + 1]).astype(bf16) for h in heads]
    kv = [lax.dot_general(k_ref[:, hs[h]], vw[h], _TN, preferred_element_type=f32) * scale
          for h in heads]
    n_all = jnp.dot(jnp.concatenate([w_a, w_a], axis=0).astype(bf16), k_ref[...],
                    preferred_element_type=f32) * scale
    c_new = [s_old[h:h + 1, :] * c_prev[h] + kv[h] for h in heads]
    n_new = [s_old[h:h + 1, :] * n_ref[h:h + 1, :] + n_all[h:h + 1, hs[h]] for h in heads]
    out_proj_part(0)

    p_, e_neg_, s_inter_ = [], [], []
    for h in heads:
        b_col = cols[:, h:h + 1]
        m_t = cols[:, 2 * N_HEADS + h:2 * N_HEADS + h + 1]
        g = jnp.where(causal, b_col + c_row[h:h + 1, :], -jnp.inf)
        p_.append(jnp.exp(g - m_t) * s_[h])
        s_inter_.append(cols[:, 3 * N_HEADS + h:3 * N_HEADS + h + 1])
        e_neg_.append(cols[:, 4 * N_HEADS + h:4 * N_HEADS + h + 1])

    pv = [jnp.dot(p_[h].astype(bf16), v_ref[:, hs[h]], preferred_element_type=f32) for h in heads]
    out_proj_part(1)

    hm_new = []
    for h in heads:
        num = pv[h] + s_inter_[h] * qc[h]
        den = jnp.sum(p_[h], axis=1, keepdims=True) + s_inter_[h] * qn[h]
        hh = num * (1.0 / jnp.maximum(jnp.abs(den), e_neg_[h]))
        hh = o_gate[h] * hh
        hh = hh * lax.rsqrt(jnp.mean(hh * hh, axis=-1, keepdims=True) + EPS)
        hh = hh * mhg_ref[:, hs[h]]
        hm_new.append((hh * z_gate[h]).astype(bf16))

    out_proj_part(2)
    out_proj_part(3)

    y = x_ref[...] + jnp.concatenate(ys, axis=1)
    if final_norm:
        y = y * lax.rsqrt(jnp.mean(y * y, axis=-1, keepdims=True) + EPS) * fg_ref[...]
    out_ref[...] = y
    for h in heads:
        hm_ref[wr, :, hs[h]] = hm_new[h]
        c_ref[h] = c_new[h]
        n_ref[h:h + 1, :] = n_new[h]
    _store_front(next_chain, next_scores, grow_ref, gcol_ref, gsc_ref, s_ref)


def _mlstm_out(proj, gates, mhg, x2, cc, w_a, w_c, fg, *, seq, d_mlstm, final_norm):
    t, d = x2.shape
    L = MLSTM_CHUNK
    nc = seq // L
    n = t // L
    last = n - 1
    hd = d_mlstm // N_HEADS
    assert d % N_HEADS == 0 and seq % L == 0

    def col(j, ahead=0):
        return pl.BlockSpec((L, d_mlstm), lambda s, j=j: (jnp.minimum(s + ahead, last), j))

    def gate_rows(ahead):
        return pl.BlockSpec((2 * N_HEADS, L), lambda s: (0, jnp.minimum(s + ahead, last)))

    prev = lambda s: (jnp.maximum(s - 1, 0), 0)
    const = lambda s: (0, 0)
    return pl.pallas_call(
        functools.partial(_mlstm_out_kernel, head_dim=hd, chunks_per_seq=nc, final_norm=final_norm),
        grid=(n + 1,),
        in_specs=[col(0), col(1), col(2), col(3), col(4), gate_rows(0),
                  col(0, 1), col(1, 1), gate_rows(1),
                  pl.BlockSpec((1, d_mlstm), const),
                  pl.BlockSpec((L, d), prev),
                  pl.BlockSpec((L, cc.shape[1]), prev),
                  pl.BlockSpec(w_a.shape, const, pipeline_mode=pl.Buffered(1)),
                  pl.BlockSpec(w_c.shape, const, pipeline_mode=pl.Buffered(1)),
                  pl.BlockSpec((1, d), const)],
        out_specs=pl.BlockSpec((L, d), prev),
        out_shape=jax.ShapeDtypeStruct((t, d), f32),
        scratch_shapes=[pltpu.VMEM((N_HEADS, hd, hd), f32),
                        pltpu.VMEM((SUBLANES, hd), f32),
                        pltpu.VMEM((2, L, d_mlstm), bf16),
                        pltpu.VMEM((2 * N_HEADS, L), f32),
                        pltpu.VMEM((L, LANES), f32),
                        pltpu.VMEM((2, SUBLANES, LANES), f32),
                        pltpu.VMEM((N_HEADS, L, L), f32)],
        compiler_params=pltpu.CompilerParams(
            dimension_semantics=("arbitrary",), vmem_limit_bytes=VMEM_LIMIT),
        name="mlstm_out",
    )(proj, proj, proj, proj, proj, gates, proj, proj, gates, mhg, x2, cc, w_a, w_c, fg)


def kernel(x, norm_g, w_in, b_gates, mh_norm_g, conv_w, conv_b, conv_ln_g, conv_ln_b, w_out, final_norm_g):
    batch, seq, d_model = x.shape
    depth = norm_g.shape[0]
    d_mlstm = mh_norm_g.shape[1]
    d_conv = conv_b.shape[1]
    n_gate = 2 * N_HEADS
    g0 = 5 * d_mlstm

    h = x.reshape(batch * seq, d_model)
    for l in range(depth):
        w_t = jnp.swapaxes(w_in[l], 0, 1).astype(bf16)
        proj, gates, c = _proj_conv(h, norm_g[l][None, :], w_t, b_gates[l][:, None],
                                    conv_w[l], conv_b[l][None, :], conv_ln_g[l][None, :],
                                    conv_ln_b[l][None, :], seq=seq, d_conv=d_conv, dm=g0)
        wo = w_out[l].astype(bf16)
        h = _mlstm_out(proj, gates, mh_norm_g[l][None, :], h, c, wo[:d_mlstm], wo[d_mlstm:],
                       final_norm_g[None, :], seq=seq, d_mlstm=d_mlstm, final_norm=(l == depth - 1))
    return h.reshape(batch, seq, d_model)
```

```python
import functools

import jax
import jax.numpy as jnp
from jax import lax
from jax.experimental import pallas as pl
from jax.experimental.pallas import tpu as pltpu

N_HEADS = 4
CONV_WIDTH = 31
EPS = 1e-6
M_INIT = -1e30

MLSTM_CHUNK = 256
LANES = 128
SUBLANES = 8
CONV_HALO = 32
VMEM_LIMIT = 56 * 1024 * 1024

f32 = jnp.float32
bf16 = jnp.bfloat16

_NT = (((1,), (1,)), ((), ()))
_TN = (((0,), (0,)), ((), ()))


def _zero_bits(x):
    b = pltpu.bitcast(x, jnp.uint32)
    return lax.shift_right_logical(lax.shift_right_logical(b, jnp.uint32(16)), jnp.uint32(16))


def _sigmoid(x):
    return 0.5 * jnp.tanh(0.5 * x) + 0.5


def _silu(x):
    hx = 0.5 * x
    return hx * jnp.tanh(hx) + hx


def _conv_rows(x, w_ref, lanes, rb, dep):
    acc = None
    for r in range(SUBLANES):
        nq = (CONV_WIDTH - 1 - r) // SUBLANES + 1
        lo = 0 if r == 0 else SUBLANES
        p = None
        for q in range(nq):
            j = CONV_WIDTH - 1 - (SUBLANES * q + r)
            start = CONV_HALO - lo - SUBLANES * q
            wj = w_ref[j:j + 1, lanes]
            if dep is not None:
                wj = pltpu.bitcast(pltpu.bitcast(wj, jnp.uint32) | dep, f32)
            term = wj * x[start:start + rb + lo, :]
            p = term if p is None else p + term
        acc = p if r == 0 else acc + p[SUBLANES - r:SUBLANES - r + rb, :]
        dep = _zero_bits(acc[0:1, :])
    return acc, dep


def _proj_conv_kernel(x_ref, g_ref, wm_ref, wc_ref, wg_ref, bg_ref, cw_ref, cb_ref, lng_ref, lnb_ref,
                      proj_ref, gates_ref, c_ref,
                      u_ref, ag_ref, z_ref, buf_ref, acc_ref, *, tiles_per_seq, tn, rb):
    i = pl.program_id(0)
    tm, d = x_ref.shape
    dc = c_ref.shape[1]
    halo = CONV_HALO

    @pl.when(i == 0)
    def _():
        z_ref[...] = jnp.zeros_like(z_ref)
        buf_ref[...] = jnp.zeros_like(buf_ref)

    x = x_ref[...]
    u = x * lax.rsqrt(jnp.mean(x * x, axis=-1, keepdims=True) + EPS) * g_ref[...]
    u_ref[...] = u.astype(bf16)
    gates_ref[...] = lax.dot_general(wg_ref[...], u_ref[...], _NT, preferred_element_type=f32) + bg_ref[...]

    dm = wm_ref.shape[0]

    def matmul(col):
        w = wm_ref[col:col + tn, :] if col < dm else wc_ref[col - dm:col - dm + tn, :]
        return lax.dot_general(u_ref[...], w, _NT, preferred_element_type=f32).astype(bf16)

    def glu_item(r0, n, with_history):
        def compute():
            half = jnp.full((1, dc), 0.5, f32)
            ys = []
            for p0 in range(r0, r0 + n, rb):
                a = ag_ref[p0:p0 + rb, 0:dc].astype(f32)
                g = ag_ref[p0:p0 + rb, dc:2 * dc].astype(f32)
                y = a * (half * jnp.tanh(half * g) + half)
                half = pltpu.bitcast(pltpu.bitcast(half, jnp.uint32) | _zero_bits(y[0:1, :]), f32)
                ys.append(y)
            if with_history:
                tail = buf_ref[tm:tm + halo, :]
                ys.append(jnp.where(i % tiles_per_seq == 0, jnp.zeros_like(tail), tail))
            return ys

        def store(ys):
            for j in range(n // rb):
                buf_ref[halo + r0 + j * rb:halo + r0 + (j + 1) * rb, :] = ys[j]
            if with_history:
                buf_ref[0:halo, :] = ys[-1]
        return compute, store

    def conv_item(units):
        def compute():
            ys, dep = [], None
            for t0, l0 in units:
                x = buf_ref[t0:t0 + halo + rb, l0:l0 + LANES]
                y, dep = _conv_rows(x, cw_ref, slice(l0, l0 + LANES), rb, dep)
                ys.append(y + cb_ref[:, l0:l0 + LANES])
            return ys

        def store(ys):
            for y, (t0, l0) in zip(ys, units):
                acc_ref[t0:t0 + rb, l0:l0 + LANES] = y
        return compute, store

    def ln_item(r0, n):
        def compute():
            outs = []
            gain = lng_ref[...]
            for p0 in range(r0, r0 + n, rb):
                y = acc_ref[p0:p0 + rb, :]
                mu = jnp.mean(y, axis=-1, keepdims=True)
                yc = y - mu
                var = jnp.mean(yc * yc, axis=-1, keepdims=True)
                yn = yc * lax.rsqrt(var + EPS) * gain + lnb_ref[...]
                out = _silu(yn) * _silu(z_ref[p0:p0 + rb, :].astype(f32))
                gain = pltpu.bitcast(pltpu.bitcast(gain, jnp.uint32) | _zero_bits(out[0:1, :]), f32)
                outs.append(out.astype(c_ref.dtype))
            return outs

        def store(outs):
            for j, out in enumerate(outs):
                c_ref[r0 + j * rb:r0 + (j + 1) * rb, :] = out
        return compute, store

    def to_proj(col, val):
        proj_ref[:, col:col + tn] = val

    def to_ag(col, val):
        ag_ref[:, col - a0:col - a0 + tn] = val

    def to_z(col, val):
        z_ref[:, col - zc0:col - zc0 + tn] = val

    q0, k0, v0, o0, zm0, a0, g0, zc0 = (j * d for j in range(8))
    assert tn == d and tm == 8 * rb
    mm_order = [(to_proj, q0), (to_proj, k0), (to_ag, a0), (to_ag, g0),
                (to_proj, v0), (to_proj, o0), (to_proj, zm0), (to_z, zc0)]
    units = [(t0, l0) for t0 in range(0, tm, rb) for l0 in range(0, dc, LANES)]
    n_units = (11, 9, 9, 9, 9, 9, 8)
    cuts = [sum(n_units[:j]) for j in range(len(n_units) + 1)]
    assert cuts[-1] == len(units)
    per_row_block = dc // LANES
    work = []
    for j in range(len(n_units)):
        items = [conv_item(units[cuts[j]:cuts[j + 1]])]
        if j >= 1:
            assert cuts[j] >= j * per_row_block
            items.append(ln_item((j - 1) * rb, rb))
        work.append(items)
    done = len(n_units) - 1
    assert cuts[done + 1] >= min(((tm // 2) // rb + 1) * per_row_block, len(units))
    work[done].append(glu_item(0, tm // 2, True))
    work.append([ln_item(done * rb, tm - done * rb), glu_item(tm // 2, tm // 2, False)])

    once = jnp.minimum(i, 0) + 1

    for (mm_store, col), items in zip(mm_order, work):
        def region(_, carry, mm_store=mm_store, col=col, items=items):
            outs = [compute() for compute, _ in items]
            res = matmul(col)
            mm_store(col, res)
            for (_, store), out in zip(items, outs):
                store(out)
            return carry
        lax.fori_loop(0, once, region, 0)


def _proj_conv(x2, g, w_t, bg, conv_w, conv_b, ln_g, ln_b, *, seq, d_conv, dm, tm=512, tn=1024, rb=64):
    t, d = x2.shape
    n_gate = 2 * N_HEADS
    assert (d == d_conv and dm == 5 * d and w_t.shape[0] == dm + n_gate + 3 * d_conv and d % tn == 0
            and seq % tm == 0 and tm % rb == 0)
    n = t // tm
    last = n - 1
    const = lambda i: (0, 0)
    vec = pl.BlockSpec((1, d_conv), const)
    return pl.pallas_call(
        functools.partial(_proj_conv_kernel, tiles_per_seq=seq // tm, tn=tn, rb=rb),
        grid=(n + 1,),
        in_specs=[
            pl.BlockSpec((tm, d), lambda i: (jnp.minimum(i, last), 0)),
            pl.BlockSpec((1, d), const),
            pl.BlockSpec((pl.Element(dm), pl.Element(d)), const, pipeline_mode=pl.Buffered(1)),
            pl.BlockSpec((pl.Element(3 * d_conv), pl.Element(d)), lambda i: (dm + n_gate, 0),
                         pipeline_mode=pl.Buffered(1)),
            pl.BlockSpec((pl.Element(n_gate), pl.Element(d)), lambda i: (dm, 0)),
            pl.BlockSpec((2 * N_HEADS, 1), const),
            pl.BlockSpec((CONV_WIDTH, d_conv), const),
            vec, vec, vec,
        ],
        out_specs=[
            pl.BlockSpec((tm, dm), lambda i: (jnp.minimum(i, last), 0)),
            pl.BlockSpec((2 * N_HEADS, tm), lambda i: (0, jnp.minimum(i, last))),
            pl.BlockSpec((tm, d_conv), lambda i: (jnp.maximum(i - 1, 0), 0)),
        ],
        out_shape=[
            jax.ShapeDtypeStruct((t, dm), bf16),
            jax.ShapeDtypeStruct((2 * N_HEADS, t), f32),
            jax.ShapeDtypeStruct((t, d_conv), bf16),
        ],
        scratch_shapes=[
            pltpu.VMEM((tm, d), bf16),
            pltpu.VMEM((tm, 2 * d_conv), bf16),
            pltpu.VMEM((tm, d_conv), bf16),
            pltpu.VMEM((tm + CONV_HALO, d_conv), f32),
            pltpu.VMEM((tm, d_conv), f32),
        ],
        compiler_params=pltpu.CompilerParams(
            dimension_semantics=("arbitrary",), vmem_limit_bytes=VMEM_LIMIT),
        name="proj_conv",
    )(x2, g, w_t, w_t, w_t, bg, conv_w, conv_b, ln_g, ln_b)


def _cumsum_lanes(x):
    rows, n = x.shape
    tri = (lax.broadcasted_iota(jnp.int32, (n, n), 0)
           <= lax.broadcasted_iota(jnp.int32, (n, n), 1)).astype(bf16)
    hi = x.astype(bf16).astype(f32)
    mid = (x - hi).astype(bf16).astype(f32)
    lo = x - hi - mid
    parts = jnp.dot(jnp.concatenate([hi, mid, lo], axis=0).astype(bf16), tri, preferred_element_type=f32)
    return parts[0:rows] + parts[rows:2 * rows] + parts[2 * rows:3 * rows]


def _gate_chain(gr, m_prev):
    L = gr.shape[1]
    lf = jnp.minimum(gr, 0.0) - jnp.log1p(jnp.exp(-jnp.abs(gr)))
    bfull = _cumsum_lanes(lf)
    ig = gr[0:N_HEADS]
    b = bfull[N_HEADS:2 * N_HEADS]
    b_last = b[:, L - 1:L]
    a = b_last - b + ig
    a_max = jnp.max(a, axis=1, keepdims=True)
    m_new = jnp.maximum(b_last + m_prev, a_max)
    s_old = jnp.exp(b_last + m_prev - m_new)
    w_a = jnp.exp(a - m_new)
    c_row = ig - b
    run = jnp.concatenate([c_row, c_row], axis=0)
    lane = lax.broadcasted_iota(jnp.int32, run.shape, 1)
    shift = 1
    while shift < L:
        run = jnp.maximum(run, jnp.where(lane >= shift, pltpu.roll(run, shift, 1), -jnp.inf))
        shift *= 2
    run = run[0:N_HEADS]
    li = b + m_prev
    m_t = jnp.maximum(li, b + run)
    s_inter = jnp.exp(li - m_t)
    e_neg = jnp.exp(-m_t)
    cols = jnp.concatenate([jnp.transpose(jnp.concatenate([b, w_a], axis=0)),
                            jnp.transpose(jnp.concatenate([m_t, s_inter], axis=0)),
                            jnp.transpose(jnp.concatenate([e_neg, e_neg], axis=0))], axis=1)
    return c_row, w_a, cols, m_new, s_old


def _scores(q_ref, k_ref, head_dim):
    return [lax.dot_general(q_ref[:, h * head_dim:(h + 1) * head_dim],
                            k_ref[:, h * head_dim:(h + 1) * head_dim], _NT,
                            preferred_element_type=f32) * head_dim ** -0.5 for h in range(N_HEADS)]


def _store_front(chain, scores, grow_ref, gcol_ref, gsc_ref, s_ref):
    c_row, w_a, cols, m_new, s_old = chain
    grow_ref[0:N_HEADS, :] = c_row
    grow_ref[N_HEADS:2 * N_HEADS, :] = w_a
    gcol_ref[:, 0:cols.shape[1]] = cols
    for j, v in enumerate((m_new, s_old)):
        gsc_ref[j, 0:N_HEADS, :] = jnp.broadcast_to(v, (N_HEADS, gsc_ref.shape[2]))
    for h in range(N_HEADS):
        s_ref[h] = scores[h]


def _mlstm_out_kernel(q_ref, k_ref, v_ref, o_ref, z_ref, gates_ref, qn_ref, kn_ref, gates_next_ref, mhg_ref,
                      x_ref, cc_ref, wa_ref, wc_ref, fg_ref, out_ref,
                      c_ref, n_ref, hm_ref, grow_ref, gcol_ref, gsc_ref, s_ref,
                      *, head_dim, chunks_per_seq, final_norm):
    L = q_ref.shape[0]
    scale = head_dim ** -0.5
    step = pl.program_id(0)
    wr = step % 2
    rd = 1 - wr

    m_init = jnp.full((N_HEADS, 1), M_INIT, f32)

    @pl.when(step == 0)
    def _():
        hm_ref[...] = jnp.zeros_like(hm_ref)
        _store_front(_gate_chain(gates_ref[...], m_init), _scores(q_ref, k_ref, head_dim),
                     grow_ref, gcol_ref, gsc_ref, s_ref)

    @pl.when(step % chunks_per_seq == 0)
    def _():
        c_ref[...] = jnp.zeros_like(c_ref)
        n_ref[...] = jnp.zeros_like(n_ref)

    c_row = grow_ref[0:N_HEADS, :]
    w_a = grow_ref[N_HEADS:2 * N_HEADS, :]
    cols = gcol_ref[:, 0:6 * N_HEADS]
    m_new = gsc_ref[0, 0:N_HEADS, 0:1]
    s_old = gsc_ref[1, 0:N_HEADS, 0:1]
    next_chain = _gate_chain(gates_next_ref[...],
                             jnp.where((step + 1) % chunks_per_seq == 0, m_init, m_new))

    row_id = lax.broadcasted_iota(jnp.int32, (L, L), 0)
    col_id = lax.broadcasted_iota(jnp.int32, (L, L), 1)
    causal = col_id <= row_id

    d_out = out_ref.shape[1]
    tn = d_out // N_HEADS

    heads = range(N_HEADS)
    hs = [slice(h * head_dim, (h + 1) * head_dim) for h in heads]
    ys = []

    def out_proj_part(j):
        ol = slice(j * tn, (j + 1) * tn)
        ys.append(jnp.dot(hm_ref[rd], wa_ref[:, ol], preferred_element_type=f32)
                  + jnp.dot(cc_ref[...], wc_ref[:, ol], preferred_element_type=f32))

    o_gate = [_sigmoid(o_ref[:, hs[h]].astype(f32)) for h in heads]
    z_gate = [_silu(z_ref[:, hs[h]].astype(f32)) for h in heads]
    qn = [jnp.sum(q_ref[:, hs[h]].astype(f32) * n_ref[h:h + 1, :], axis=1, keepdims=True) for h in heads]

    s_ = [s_ref[h] for h in heads]
    c_prev = [c_ref[h] for h in heads]
    qc = [jnp.dot(q_ref[:, hs[h]], c_prev[h].astype(bf16), preferred_element_type=f32) for h in heads]
    next_scores = _scores(qn_ref, kn_ref, head_dim)
    n_all = jnp.dot(jnp.concatenate([w_a, w_a], axis=0).astype(bf16), k_ref[...],
                    preferred_element_type=f32) * scale
    out_proj_part(0)

    p_, e_neg_, s_inter_ = [], [], []
    for h in heads:
        b_col = cols[:, h:h + 1]
        m_t = cols[:, 2 * N_HEADS + h:2 * N_HEADS + h + 1]
        g = jnp.where(causal, b_col + c_row[h:h + 1, :], -jnp.inf)
        p_.append(jnp.exp(g - m_t) * s_[h])
        s_inter_.append(cols[:, 3 * N_HEADS + h:3 * N_HEADS + h + 1])
        e_neg_.append(cols[:, 4 * N_HEADS + h:4 * N_HEADS + h + 1])

    pv = [jnp.dot(p_[h].astype(bf16), v_ref[:, hs[h]], preferred_element_type=f32) for h in heads]
    out_proj_part(1)

    hm_new, vw = [], []
    for h in heads:
        num = pv[h] + s_inter_[h] * qc[h]
        den = jnp.sum(p_[h], axis=1, keepdims=True) + s_inter_[h] * qn[h]
        hh = num * (1.0 / jnp.maximum(jnp.abs(den), e_neg_[h]))
        hh = o_gate[h] * hh
        hh = hh * lax.rsqrt(jnp.mean(hh * hh, axis=-1, keepdims=True) + EPS)
        hh = hh * mhg_ref[:, hs[h]]
        hm_new.append((hh * z_gate[h]).astype(bf16))
        w_col = cols[:, N_HEADS + h:N_HEADS + h + 1]
        vw.append((v_ref[:, hs[h]].astype(f32) * w_col).astype(bf16))

    out_proj_part(2)
    kv = [lax.dot_general(k_ref[:, hs[h]], vw[h], _TN, preferred_element_type=f32) * scale
          for h in heads]
    out_proj_part(3)
    c_new = [s_old[h:h + 1, :] * c_prev[h] + kv[h] for h in heads]
    n_new = [s_old[h:h + 1, :] * n_ref[h:h + 1, :] + n_all[h:h + 1, hs[h]] for h in heads]

    y = x_ref[...] + jnp.concatenate(ys, axis=1)
    if final_norm:
        y = y * lax.rsqrt(jnp.mean(y * y, axis=-1, keepdims=True) + EPS) * fg_ref[...]
    out_ref[...] = y
    for h in heads:
        hm_ref[wr, :, hs[h]] = hm_new[h]
        c_ref[h] = c_new[h]
        n_ref[h:h + 1, :] = n_new[h]
    _store_front(next_chain, next_scores, grow_ref, gcol_ref, gsc_ref, s_ref)


def _mlstm_out(proj, gates, mhg, x2, cc, w_a, w_c, fg, *, seq, d_mlstm, final_norm):
    t, d = x2.shape
    L = MLSTM_CHUNK
    nc = seq // L
    n = t // L
    last = n - 1
    hd = d_mlstm // N_HEADS
    assert d % N_HEADS == 0 and seq % L == 0

    def col(j, ahead=0):
        return pl.BlockSpec((L, d_mlstm), lambda s, j=j: (jnp.minimum(s + ahead, last), j))

    def gate_rows(ahead):
        return pl.BlockSpec((2 * N_HEADS, L), lambda s: (0, jnp.minimum(s + ahead, last)))

    prev = lambda s: (jnp.maximum(s - 1, 0), 0)
    const = lambda s: (0, 0)
    return pl.pallas_call(
        functools.partial(_mlstm_out_kernel, head_dim=hd, chunks_per_seq=nc, final_norm=final_norm),
        grid=(n + 1,),
        in_specs=[col(0), col(1), col(2), col(3), col(4), gate_rows(0),
                  col(0, 1), col(1, 1), gate_rows(1),
                  pl.BlockSpec((1, d_mlstm), const),
                  pl.BlockSpec((L, d), prev),
                  pl.BlockSpec((L, cc.shape[1]), prev),
                  pl.BlockSpec(w_a.shape, const, pipeline_mode=pl.Buffered(1)),
                  pl.BlockSpec(w_c.shape, const, pipeline_mode=pl.Buffered(1)),
                  pl.BlockSpec((1, d), const)],
        out_specs=pl.BlockSpec((L, d), prev),
        out_shape=jax.ShapeDtypeStruct((t, d), f32),
        scratch_shapes=[pltpu.VMEM((N_HEADS, hd, hd), f32),
                        pltpu.VMEM((SUBLANES, hd), f32),
                        pltpu.VMEM((2, L, d_mlstm), bf16),
                        pltpu.VMEM((2 * N_HEADS, L), f32),
                        pltpu.VMEM((L, LANES), f32),
                        pltpu.VMEM((2, SUBLANES, LANES), f32),
                        pltpu.VMEM((N_HEADS, L, L), f32)],
        compiler_params=pltpu.CompilerParams(
            dimension_semantics=("arbitrary",), vmem_limit_bytes=VMEM_LIMIT),
        name="mlstm_out",
    )(proj, proj, proj, proj, proj, gates, proj, proj, gates, mhg, x2, cc, w_a, w_c, fg)


def kernel(x, norm_g, w_in, b_gates, mh_norm_g, conv_w, conv_b, conv_ln_g, conv_ln_b, w_out, final_norm_g):
    batch, seq, d_model = x.shape
    depth = norm_g.shape[0]
    d_mlstm = mh_norm_g.shape[1]
    d_conv = conv_b.shape[1]
    n_gate = 2 * N_HEADS
    g0 = 5 * d_mlstm

    h = x.reshape(batch * seq, d_model)
    for l in range(depth):
        w_t = jnp.swapaxes(w_in[l], 0, 1).astype(bf16)
        proj, gates, c = _proj_conv(h, norm_g[l][None, :], w_t, b_gates[l][:, None],
                                    conv_w[l], conv_b[l][None, :], conv_ln_g[l][None, :],
                                    conv_ln_b[l][None, :], seq=seq, d_conv=d_conv, dm=g0)
        wo = w_out[l].astype(bf16)
        h = _mlstm_out(proj, gates, mh_norm_g[l][None, :], h, c, wo[:d_mlstm], wo[d_mlstm:],
                       final_norm_g[None, :], seq=seq, d_mlstm=d_mlstm, final_norm=(l == depth - 1))
    return h.reshape(batch, seq, d_model)
```

```python
import functools

import jax
import jax.numpy as jnp
from jax import lax
from jax.experimental import pallas as pl
from jax.experimental.pallas import tpu as pltpu

N_HEADS = 4
CONV_WIDTH = 31
EPS = 1e-6
M_INIT = -1e30

MLSTM_CHUNK = 256
LANES = 128
SUBLANES = 8
CONV_HALO = 32
VMEM_LIMIT = 56 * 1024 * 1024

f32 = jnp.float32
bf16 = jnp.bfloat16

_NT = (((1,), (1,)), ((), ()))
_TN = (((0,), (0,)), ((), ()))


def _zero_bits(x):
    b = pltpu.bitcast(x, jnp.uint32)
    return lax.shift_right_logical(lax.shift_right_logical(b, jnp.uint32(16)), jnp.uint32(16))


def _sigmoid(x):
    return 0.5 * jnp.tanh(0.5 * x) + 0.5


def _silu(x):
    hx = 0.5 * x
    return hx * jnp.tanh(hx) + hx


def _conv_rows(x, w_ref, lanes, rb, dep):
    acc = None
    for r in range(SUBLANES):
        nq = (CONV_WIDTH - 1 - r) // SUBLANES + 1
        lo = 0 if r == 0 else SUBLANES
        p = None
        for q in range(nq):
            j = CONV_WIDTH - 1 - (SUBLANES * q + r)
            start = CONV_HALO - lo - SUBLANES * q
            wj = w_ref[j:j + 1, lanes]
            if dep is not None:
                wj = pltpu.bitcast(pltpu.bitcast(wj, jnp.uint32) | dep, f32)
            term = wj * x[start:start + rb + lo, :]
            p = term if p is None else p + term
        acc = p if r == 0 else acc + p[SUBLANES - r:SUBLANES - r + rb, :]
        dep = _zero_bits(acc[0:1, :])
    return acc, dep


def _proj_conv_kernel(x_ref, g_ref, wm_ref, wc_ref, wg_ref, bg_ref, cw_ref, cb_ref, lng_ref, lnb_ref,
                      proj_ref, gates_ref, c_ref,
                      u_ref, ag_ref, z_ref, buf_ref, acc_ref, *, tiles_per_seq, tn, rb):
    i = pl.program_id(0)
    tm, d = x_ref.shape
    dc = c_ref.shape[1]
    halo = CONV_HALO

    @pl.when(i == 0)
    def _():
        z_ref[...] = jnp.zeros_like(z_ref)
        buf_ref[...] = jnp.zeros_like(buf_ref)

    x = x_ref[...]
    u = x * lax.rsqrt(jnp.mean(x * x, axis=-1, keepdims=True) + EPS) * g_ref[...]
    u_ref[...] = u.astype(bf16)
    gates_ref[...] = lax.dot_general(wg_ref[...], u_ref[...], _NT, preferred_element_type=f32) + bg_ref[...]

    dm = wm_ref.shape[0]

    def matmul(col):
        w = wm_ref[col:col + tn, :] if col < dm else wc_ref[col - dm:col - dm + tn, :]
        return lax.dot_general(u_ref[...], w, _NT, preferred_element_type=f32).astype(bf16)

    def glu_item(r0, n, with_history):
        def compute():
            half = jnp.full((1, dc), 0.5, f32)
            ys = []
            for p0 in range(r0, r0 + n, rb):
                a = ag_ref[p0:p0 + rb, 0:dc].astype(f32)
                g = ag_ref[p0:p0 + rb, dc:2 * dc].astype(f32)
                y = a * (half * jnp.tanh(half * g) + half)
                half = pltpu.bitcast(pltpu.bitcast(half, jnp.uint32) | _zero_bits(y[0:1, :]), f32)
                ys.append(y)
            if with_history:
                tail = buf_ref[tm:tm + halo, :]
                ys.append(jnp.where(i % tiles_per_seq == 0, jnp.zeros_like(tail), tail))
            return ys

        def store(ys):
            for j in range(n // rb):
                buf_ref[halo + r0 + j * rb:halo + r0 + (j + 1) * rb, :] = ys[j]
            if with_history:
                buf_ref[0:halo, :] = ys[-1]
        return compute, store

    def conv_item(units):
        def compute():
            ys, dep = [], None
            for t0, l0 in units:
                x = buf_ref[t0:t0 + halo + rb, l0:l0 + LANES]
                y, dep = _conv_rows(x, cw_ref, slice(l0, l0 + LANES), rb, dep)
                ys.append(y + cb_ref[:, l0:l0 + LANES])
            return ys

        def store(ys):
            for y, (t0, l0) in zip(ys, units):
                acc_ref[t0:t0 + rb, l0:l0 + LANES] = y
        return compute, store

    def ln_item(r0, n):
        def compute():
            outs = []
            gain = lng_ref[...]
            for p0 in range(r0, r0 + n, rb):
                y = acc_ref[p0:p0 + rb, :]
                mu = jnp.mean(y, axis=-1, keepdims=True)
                yc = y - mu
                var = jnp.mean(yc * yc, axis=-1, keepdims=True)
                yn = yc * lax.rsqrt(var + EPS) * gain + lnb_ref[...]
                out = _silu(yn) * _silu(z_ref[p0:p0 + rb, :].astype(f32))
                gain = pltpu.bitcast(pltpu.bitcast(gain, jnp.uint32) | _zero_bits(out[0:1, :]), f32)
                outs.append(out.astype(c_ref.dtype))
            return outs

        def store(outs):
            for j, out in enumerate(outs):
                c_ref[r0 + j * rb:r0 + (j + 1) * rb, :] = out
        return compute, store

    def to_proj(col, val):
        proj_ref[:, col:col + tn] = val

    def to_ag(col, val):
        ag_ref[:, col - a0:col - a0 + tn] = val

    def to_z(col, val):
        z_ref[:, col - zc0:col - zc0 + tn] = val

    q0, k0, v0, o0, zm0, a0, g0, zc0 = (j * d for j in range(8))
    assert tn == d and tm == 8 * rb
    mm_order = [(to_proj, q0), (to_proj, k0), (to_ag, a0), (to_ag, g0),
                (to_proj, v0), (to_proj, o0), (to_proj, zm0), (to_z, zc0)]
    units = [(t0, l0) for t0 in range(0, tm, rb) for l0 in range(0, dc, LANES)]
    n_units = (11, 9, 9, 9, 9, 9, 8)
    cuts = [sum(n_units[:j]) for j in range(len(n_units) + 1)]
    assert cuts[-1] == len(units)
    per_row_block = dc // LANES
    work = []
    for j in range(len(n_units)):
        items = [conv_item(units[cuts[j]:cuts[j + 1]])]
        if j >= 1:
            assert cuts[j] >= j * per_row_block
            items.append(ln_item((j - 1) * rb, rb))
        work.append(items)
    done = len(n_units) - 1
    assert cuts[done + 1] >= min(((tm // 2) // rb + 1) * per_row_block, len(units))
    work[done].append(glu_item(0, tm // 2, True))
    work.append([ln_item(done * rb, tm - done * rb), glu_item(tm // 2, tm // 2, False)])

    once = jnp.minimum(i, 0) + 1

    for (mm_store, col), items in zip(mm_order, work):
        def region(_, carry, mm_store=mm_store, col=col, items=items):
            outs = [compute() for compute, _ in items]
            res = matmul(col)
            mm_store(col, res)
            for (_, store), out in zip(items, outs):
                store(out)
            return carry
        lax.fori_loop(0, once, region, 0)


def _proj_conv(x2, g, w_t, bg, conv_w, conv_b, ln_g, ln_b, *, seq, d_conv, dm, tm=512, tn=1024, rb=64):
    t, d = x2.shape
    n_gate = 2 * N_HEADS
    assert (d == d_conv and dm == 5 * d and w_t.shape[0] == dm + n_gate + 3 * d_conv and d % tn == 0
            and seq % tm == 0 and tm % rb == 0)
    n = t // tm
    last = n - 1
    const = lambda i: (0, 0)
    vec = pl.BlockSpec((1, d_conv), const)
    return pl.pallas_call(
        functools.partial(_proj_conv_kernel, tiles_per_seq=seq // tm, tn=tn, rb=rb),
        grid=(n + 1,),
        in_specs=[
            pl.BlockSpec((tm, d), lambda i: (jnp.minimum(i, last), 0)),
            pl.BlockSpec((1, d), const),
            pl.BlockSpec((pl.Element(dm), pl.Element(d)), const, pipeline_mode=pl.Buffered(1)),
            pl.BlockSpec((pl.Element(3 * d_conv), pl.Element(d)), lambda i: (dm + n_gate, 0),
                         pipeline_mode=pl.Buffered(1)),
            pl.BlockSpec((pl.Element(n_gate), pl.Element(d)), lambda i: (dm, 0)),
            pl.BlockSpec((2 * N_HEADS, 1), const),
            pl.BlockSpec((CONV_WIDTH, d_conv), const),
            vec, vec, vec,
        ],
        out_specs=[
            pl.BlockSpec((tm, dm), lambda i: (jnp.minimum(i, last), 0)),
            pl.BlockSpec((2 * N_HEADS, tm), lambda i: (0, jnp.minimum(i, last))),
            pl.BlockSpec((tm, d_conv), lambda i: (jnp.maximum(i - 1, 0), 0)),
        ],
        out_shape=[
            jax.ShapeDtypeStruct((t, dm), bf16),
            jax.ShapeDtypeStruct((2 * N_HEADS, t), f32),
            jax.ShapeDtypeStruct((t, d_conv), bf16),
        ],
        scratch_shapes=[
            pltpu.VMEM((tm, d), bf16),
            pltpu.VMEM((tm, 2 * d_conv), bf16),
            pltpu.VMEM((tm, d_conv), bf16),
            pltpu.VMEM((tm + CONV_HALO, d_conv), f32),
            pltpu.VMEM((tm, d_conv), f32),
        ],
        compiler_params=pltpu.CompilerParams(
            dimension_semantics=("arbitrary",), vmem_limit_bytes=VMEM_LIMIT),
        name="proj_conv",
    )(x2, g, w_t, w_t, w_t, bg, conv_w, conv_b, ln_g, ln_b)


def _cumsum_lanes(x):
    rows, n = x.shape
    tri = (lax.broadcasted_iota(jnp.int32, (n, n), 0)
           <= lax.broadcasted_iota(jnp.int32, (n, n), 1)).astype(bf16)
    hi = x.astype(bf16).astype(f32)
    mid = (x - hi).astype(bf16).astype(f32)
    lo = x - hi - mid
    parts = jnp.dot(jnp.concatenate([hi, mid, lo], axis=0).astype(bf16), tri, preferred_element_type=f32)
    return parts[0:rows] + parts[rows:2 * rows] + parts[2 * rows:3 * rows]


def _gate_chain(gr, m_prev):
    L = gr.shape[1]
    lf = jnp.minimum(gr, 0.0) - jnp.log1p(jnp.exp(-jnp.abs(gr)))
    bfull = _cumsum_lanes(lf)
    ig = gr[0:N_HEADS]
    b = bfull[N_HEADS:2 * N_HEADS]
    b_last = b[:, L - 1:L]
    a = b_last - b + ig
    a_max = jnp.max(a, axis=1, keepdims=True)
    m_new = jnp.maximum(b_last + m_prev, a_max)
    s_old = jnp.exp(b_last + m_prev - m_new)
    w_a = jnp.exp(a - m_new)
    c_row = ig - b
    run = jnp.concatenate([c_row, c_row], axis=0)
    lane = lax.broadcasted_iota(jnp.int32, run.shape, 1)
    shift = 1
    while shift < L:
        run = jnp.maximum(run, jnp.where(lane >= shift, pltpu.roll(run, shift, 1), -jnp.inf))
        shift *= 2
    run = run[0:N_HEADS]
    li = b + m_prev
    m_t = jnp.maximum(li, b + run)
    s_inter = jnp.exp(li - m_t)
    e_neg = jnp.exp(-m_t)
    cols = jnp.concatenate([jnp.transpose(jnp.concatenate([b, w_a], axis=0)),
                            jnp.transpose(jnp.concatenate([m_t, s_inter], axis=0)),
                            jnp.transpose(jnp.concatenate([e_neg, e_neg], axis=0))], axis=1)
    return c_row, w_a, cols, m_new, s_old


def _scores(q_ref, k_ref, head_dim):
    return [lax.dot_general(q_ref[:, h * head_dim:(h + 1) * head_dim],
                            k_ref[:, h * head_dim:(h + 1) * head_dim], _NT,
                            preferred_element_type=f32) * head_dim ** -0.5 for h in range(N_HEADS)]


def _store_front(chain, scores, grow_ref, gcol_ref, gsc_ref, s_ref):
    c_row, w_a, cols, m_new, s_old = chain
    grow_ref[0:N_HEADS, :] = c_row
    grow_ref[N_HEADS:2 * N_HEADS, :] = w_a
    gcol_ref[:, 0:cols.shape[1]] = cols
    for j, v in enumerate((m_new, s_old)):
        gsc_ref[j, 0:N_HEADS, :] = jnp.broadcast_to(v, (N_HEADS, gsc_ref.shape[2]))
    for h in range(N_HEADS):
        s_ref[h] = scores[h]


def _mlstm_out_kernel(q_ref, k_ref, v_ref, o_ref, z_ref, gates_ref, qn_ref, kn_ref, gates_next_ref, mhg_ref,
                      x_ref, cc_ref, wa_ref, wc_ref, fg_ref, out_ref,
                      c_ref, n_ref, hm_ref, grow_ref, gcol_ref, gsc_ref, s_ref,
                      *, head_dim, chunks_per_seq, final_norm):
    L = q_ref.shape[0]
    scale = head_dim ** -0.5
    step = pl.program_id(0)
    wr = step % 2
    rd = 1 - wr

    m_init = jnp.full((N_HEADS, 1), M_INIT, f32)

    @pl.when(step == 0)
    def _():
        hm_ref[...] = jnp.zeros_like(hm_ref)
        _store_front(_gate_chain(gates_ref[...], m_init), _scores(q_ref, k_ref, head_dim),
                     grow_ref, gcol_ref, gsc_ref, s_ref)

    @pl.when(step % chunks_per_seq == 0)
    def _():
        c_ref[...] = jnp.zeros_like(c_ref)
        n_ref[...] = jnp.zeros_like(n_ref)

    c_row = grow_ref[0:N_HEADS, :]
    w_a = grow_ref[N_HEADS:2 * N_HEADS, :]
    cols = gcol_ref[:, 0:6 * N_HEADS]
    m_new = gsc_ref[0, 0:N_HEADS, 0:1]
    s_old = gsc_ref[1, 0:N_HEADS, 0:1]
    next_chain = _gate_chain(gates_next_ref[...],
                             jnp.where((step + 1) % chunks_per_seq == 0, m_init, m_new))

    row_id = lax.broadcasted_iota(jnp.int32, (L, L), 0)
    col_id = lax.broadcasted_iota(jnp.int32, (L, L), 1)
    causal = col_id <= row_id

    d_out = out_ref.shape[1]
    tn = d_out // N_HEADS

    heads = range(N_HEADS)
    hs = [slice(h * head_dim, (h + 1) * head_dim) for h in heads]
    ys = []

    def out_proj_part(j):
        ol = slice(j * tn, (j + 1) * tn)
        ys.append(jnp.dot(hm_ref[rd], wa_ref[:, ol], preferred_element_type=f32)
                  + jnp.dot(cc_ref[...], wc_ref[:, ol], preferred_element_type=f32))

    o_gate = [_sigmoid(o_ref[:, hs[h]].astype(f32)) for h in heads]
    z_gate = [_silu(z_ref[:, hs[h]].astype(f32)) for h in heads]
    qn = [jnp.sum(q_ref[:, hs[h]].astype(f32) * n_ref[h:h + 1, :], axis=1, keepdims=True) for h in heads]

    s_ = [s_ref[h] for h in heads]
    c_prev = [c_ref[h] for h in heads]
    qc = [jnp.dot(q_ref[:, hs[h]], c_prev[h].astype(bf16), preferred_element_type=f32) for h in heads]
    out_proj_part(0)
    next_scores = _scores(qn_ref, kn_ref, head_dim)

    p_, e_neg_, s_inter_ = [], [], []
    for h in heads:
        b_col = cols[:, h:h + 1]
        m_t = cols[:, 2 * N_HEADS + h:2 * N_HEADS + h + 1]
        g = jnp.where(causal, b_col + c_row[h:h + 1, :], -jnp.inf)
        p_.append(jnp.exp(g - m_t) * s_[h])
        s_inter_.append(cols[:, 3 * N_HEADS + h:3 * N_HEADS + h + 1])
        e_neg_.append(cols[:, 4 * N_HEADS + h:4 * N_HEADS + h + 1])

    pv = [jnp.dot(p_[h].astype(bf16), v_ref[:, hs[h]], preferred_element_type=f32) for h in heads]
    out_proj_part(1)

    hm_new, vw = [], []
    for h in heads:
        num = pv[h] + s_inter_[h] * qc[h]
        den = jnp.sum(p_[h], axis=1, keepdims=True) + s_inter_[h] * qn[h]
        hh = num * (1.0 / jnp.maximum(jnp.abs(den), e_neg_[h]))
        hh = o_gate[h] * hh
        hh = hh * lax.rsqrt(jnp.mean(hh * hh, axis=-1, keepdims=True) + EPS)
        hh = hh * mhg_ref[:, hs[h]]
        hm_new.append((hh * z_gate[h]).astype(bf16))
        w_col = cols[:, N_HEADS + h:N_HEADS + h + 1]
        vw.append((v_ref[:, hs[h]].astype(f32) * w_col).astype(bf16))

    out_proj_part(2)
    kv = [lax.dot_general(k_ref[:, hs[h]], vw[h], _TN, preferred_element_type=f32) * scale
          for h in heads]
    out_proj_part(3)
    n_all = jnp.dot(jnp.concatenate([w_a, w_a], axis=0).astype(bf16), k_ref[...],
                    preferred_element_type=f32) * scale
    c_new = [s_old[h:h + 1, :] * c_prev[h] + kv[h] for h in heads]
    n_new = [s_old[h:h + 1, :] * n_ref[h:h + 1, :] + n_all[h:h + 1, hs[h]] for h in heads]

    y = x_ref[...] + jnp.concatenate(ys, axis=1)
    if final_norm:
        y = y * lax.rsqrt(jnp.mean(y * y, axis=-1, keepdims=True) + EPS) * fg_ref[...]
    out_ref[...] = y
    for h in heads:
        hm_ref[wr, :, hs[h]] = hm_new[h]
        c_ref[h] = c_new[h]
        n_ref[h:h + 1, :] = n_new[h]
    _store_front(next_chain, next_scores, grow_ref, gcol_ref, gsc_ref, s_ref)


def _mlstm_out(proj, gates, mhg, x2, cc, w_a, w_c, fg, *, seq, d_mlstm, final_norm):
    t, d = x2.shape
    L = MLSTM_CHUNK
    nc = seq // L
    n = t // L
    last = n - 1
    hd = d_mlstm // N_HEADS
    assert d % N_HEADS == 0 and seq % L == 0

    def col(j, ahead=0):
        return pl.BlockSpec((L, d_mlstm), lambda s, j=j: (jnp.minimum(s + ahead, last), j))

    def gate_rows(ahead):
        return pl.BlockSpec((2 * N_HEADS, L), lambda s: (0, jnp.minimum(s + ahead, last)))

    prev = lambda s: (jnp.maximum(s - 1, 0), 0)
    const = lambda s: (0, 0)
    return pl.pallas_call(
        functools.partial(_mlstm_out_kernel, head_dim=hd, chunks_per_seq=nc, final_norm=final_norm),
        grid=(n + 1,),
        in_specs=[col(0), col(1), col(2), col(3), col(4), gate_rows(0),
                  col(0, 1), col(1, 1), gate_rows(1),
                  pl.BlockSpec((1, d_mlstm), const),
                  pl.BlockSpec((L, d), prev),
                  pl.BlockSpec((L, cc.shape[1]), prev),
                  pl.BlockSpec(w_a.shape, const, pipeline_mode=pl.Buffered(1)),
                  pl.BlockSpec(w_c.shape, const, pipeline_mode=pl.Buffered(1)),
                  pl.BlockSpec((1, d), const)],
        out_specs=pl.BlockSpec((L, d), prev),
        out_shape=jax.ShapeDtypeStruct((t, d), f32),
        scratch_shapes=[pltpu.VMEM((N_HEADS, hd, hd), f32),
                        pltpu.VMEM((SUBLANES, hd), f32),
                        pltpu.VMEM((2, L, d_mlstm), bf16),
                        pltpu.VMEM((2 * N_HEADS, L), f32),
                        pltpu.VMEM((L, LANES), f32),
                        pltpu.VMEM((2, SUBLANES, LANES), f32),
                        pltpu.VMEM((N_HEADS, L, L), f32)],
        compiler_params=pltpu.CompilerParams(
            dimension_semantics=("arbitrary",), vmem_limit_bytes=VMEM_LIMIT),
        name="mlstm_out",
    )(proj, proj, proj, proj, proj, gates, proj, proj, gates, mhg, x2, cc, w_a, w_c, fg)


def kernel(x, norm_g, w_in, b_gates, mh_norm_g, conv_w, conv_b, conv_ln_g, conv_ln_b, w_out, final_norm_g):
    batch, seq, d_model = x.shape
    depth = norm_g.shape[0]
    d_mlstm = mh_norm_g.shape[1]
    d_conv = conv_b.shape[1]
    n_gate = 2 * N_HEADS
    g0 = 5 * d_mlstm

    h = x.reshape(batch * seq, d_model)
    for l in range(depth):
        w_t = jnp.swapaxes(w_in[l], 0, 1).astype(bf16)
        proj, gates, c = _proj_conv(h, norm_g[l][None, :], w_t, b_gates[l][:, None],
                                    conv_w[l], conv_b[l][None, :], conv_ln_g[l][None, :],
                                    conv_ln_b[l][None, :], seq=seq, d_conv=d_conv, dm=g0)
        wo = w_out[l].astype(bf16)
        h = _mlstm_out(proj, gates, mh_norm_g[l][None, :], h, c, wo[:d_mlstm], wo[d_mlstm:],
                       final_norm_g[None, :], seq=seq, d_mlstm=d_mlstm, final_norm=(l == depth - 1))
    return h.reshape(batch, seq, d_model)
```

```python
import functools

import jax
import jax.numpy as jnp
from jax import lax
from jax.experimental import pallas as pl
from jax.experimental.pallas import tpu as pltpu

N_HEADS = 4
CONV_WIDTH = 31
EPS = 1e-6
M_INIT = -1e30

MLSTM_CHUNK = 256
LANES = 128
SUBLANES = 8
CONV_HALO = 32
VMEM_LIMIT = 56 * 1024 * 1024

f32 = jnp.float32
bf16 = jnp.bfloat16

_NT = (((1,), (1,)), ((), ()))
_TN = (((0,), (0,)), ((), ()))


def _zero_bits(x):
    b = pltpu.bitcast(x, jnp.uint32)
    return lax.shift_right_logical(lax.shift_right_logical(b, jnp.uint32(16)), jnp.uint32(16))


def _sigmoid(x):
    return 0.5 * jnp.tanh(0.5 * x) + 0.5


def _silu(x):
    hx = 0.5 * x
    return hx * jnp.tanh(hx) + hx


def _conv_rows(x, w_ref, lanes, rb, dep):
    acc = None
    for r in range(SUBLANES):
        nq = (CONV_WIDTH - 1 - r) // SUBLANES + 1
        lo = 0 if r == 0 else SUBLANES
        p = None
        for q in range(nq):
            j = CONV_WIDTH - 1 - (SUBLANES * q + r)
            start = CONV_HALO - lo - SUBLANES * q
            wj = w_ref[j:j + 1, lanes]
            if dep is not None:
                wj = pltpu.bitcast(pltpu.bitcast(wj, jnp.uint32) | dep, f32)
            term = wj * x[start:start + rb + lo, :]
            p = term if p is None else p + term
        acc = p if r == 0 else acc + p[SUBLANES - r:SUBLANES - r + rb, :]
        dep = _zero_bits(acc[0:1, :])
    return acc, dep


def _proj_conv_kernel(x_ref, g_ref, wm_ref, wc_ref, wg_ref, bg_ref, cw_ref, cb_ref, lng_ref, lnb_ref,
                      proj_ref, gates_ref, c_ref,
                      u_ref, ag_ref, z_ref, buf_ref, acc_ref, *, tiles_per_seq, tn, rb):
    i = pl.program_id(0)
    tm, d = x_ref.shape
    dc = c_ref.shape[1]
    halo = CONV_HALO

    @pl.when(i == 0)
    def _():
        z_ref[...] = jnp.zeros_like(z_ref)
        buf_ref[...] = jnp.zeros_like(buf_ref)

    x = x_ref[...]
    u = x * lax.rsqrt(jnp.mean(x * x, axis=-1, keepdims=True) + EPS) * g_ref[...]
    u_ref[...] = u.astype(bf16)
    gates_ref[...] = lax.dot_general(wg_ref[...], u_ref[...], _NT, preferred_element_type=f32) + bg_ref[...]

    dm = wm_ref.shape[0]

    def matmul(col):
        w = wm_ref[col:col + tn, :] if col < dm else wc_ref[col - dm:col - dm + tn, :]
        return lax.dot_general(u_ref[...], w, _NT, preferred_element_type=f32).astype(bf16)

    def glu_item(r0, n, with_history):
        def compute():
            half = jnp.full((1, dc), 0.5, f32)
            ys = []
            for p0 in range(r0, r0 + n, rb):
                a = ag_ref[p0:p0 + rb, 0:dc].astype(f32)
                g = ag_ref[p0:p0 + rb, dc:2 * dc].astype(f32)
                y = a * (half * jnp.tanh(half * g) + half)
                half = pltpu.bitcast(pltpu.bitcast(half, jnp.uint32) | _zero_bits(y[0:1, :]), f32)
                ys.append(y)
            if with_history:
                tail = buf_ref[tm:tm + halo, :]
                ys.append(jnp.where(i % tiles_per_seq == 0, jnp.zeros_like(tail), tail))
            return ys

        def store(ys):
            for j in range(n // rb):
                buf_ref[halo + r0 + j * rb:halo + r0 + (j + 1) * rb, :] = ys[j]
            if with_history:
                buf_ref[0:halo, :] = ys[-1]
        return compute, store

    def conv_item(units):
        def compute():
            ys, dep = [], None
            for t0, l0 in units:
                x = buf_ref[t0:t0 + halo + rb, l0:l0 + LANES]
                y, dep = _conv_rows(x, cw_ref, slice(l0, l0 + LANES), rb, dep)
                ys.append(y + cb_ref[:, l0:l0 + LANES])
            return ys

        def store(ys):
            for y, (t0, l0) in zip(ys, units):
                acc_ref[t0:t0 + rb, l0:l0 + LANES] = y
        return compute, store

    def ln_item(r0, n):
        def compute():
            outs = []
            gain = lng_ref[...]
            for p0 in range(r0, r0 + n, rb):
                y = acc_ref[p0:p0 + rb, :]
                mu = jnp.mean(y, axis=-1, keepdims=True)
                yc = y - mu
                var = jnp.mean(yc * yc, axis=-1, keepdims=True)
                yn = yc * lax.rsqrt(var + EPS) * gain + lnb_ref[...]
                out = _silu(yn) * _silu(z_ref[p0:p0 + rb, :].astype(f32))
                gain = pltpu.bitcast(pltpu.bitcast(gain, jnp.uint32) | _zero_bits(out[0:1, :]), f32)
                outs.append(out.astype(c_ref.dtype))
            return outs

        def store(outs):
            for j, out in enumerate(outs):
                c_ref[r0 + j * rb:r0 + (j + 1) * rb, :] = out
        return compute, store

    def to_proj(col, val):
        proj_ref[:, col:col + tn] = val

    def to_ag(col, val):
        ag_ref[:, col - a0:col - a0 + tn] = val

    def to_z(col, val):
        z_ref[:, col - zc0:col - zc0 + tn] = val

    q0, k0, v0, o0, zm0, a0, g0, zc0 = (j * d for j in range(8))
    assert tn == d and tm == 8 * rb
    mm_order = [(to_proj, q0), (to_proj, k0), (to_ag, a0), (to_ag, g0),
                (to_proj, v0), (to_proj, o0), (to_proj, zm0), (to_z, zc0)]
    units = [(t0, l0) for t0 in range(0, tm, rb) for l0 in range(0, dc, LANES)]
    n_units = (11, 9, 9, 9, 9, 9, 8)
    cuts = [sum(n_units[:j]) for j in range(len(n_units) + 1)]
    assert cuts[-1] == len(units)
    per_row_block = dc // LANES
    work = []
    for j in range(len(n_units)):
        items = [conv_item(units[cuts[j]:cuts[j + 1]])]
        if j >= 1:
            assert cuts[j] >= j * per_row_block
            items.append(ln_item((j - 1) * rb, rb))
        work.append(items)
    done = len(n_units) - 1
    assert cuts[done + 1] >= min(((tm // 2) // rb + 1) * per_row_block, len(units))
    work[done].append(glu_item(0, tm // 2, True))
    work.append([ln_item(done * rb, tm - done * rb), glu_item(tm // 2, tm // 2, False)])

    once = jnp.minimum(i, 0) + 1

    for (mm_store, col), items in zip(mm_order, work):
        def region(_, carry, mm_store=mm_store, col=col, items=items):
            outs = [compute() for compute, _ in items]
            res = matmul(col)
            mm_store(col, res)
            for (_, store), out in zip(items, outs):
                store(out)
            return carry
        lax.fori_loop(0, once, region, 0)


def _proj_conv(x2, g, w_t, bg, conv_w, conv_b, ln_g, ln_b, *, seq, d_conv, dm, tm=512, tn=1024, rb=64):
    t, d = x2.shape
    n_gate = 2 * N_HEADS
    assert (d == d_conv and dm == 5 * d and w_t.shape[0] == dm + n_gate + 3 * d_conv and d % tn == 0
            and seq % tm == 0 and tm % rb == 0)
    n = t // tm
    last = n - 1
    const = lambda i: (0, 0)
    vec = pl.BlockSpec((1, d_conv), const)
    return pl.pallas_call(
        functools.partial(_proj_conv_kernel, tiles_per_seq=seq // tm, tn=tn, rb=rb),
        grid=(n + 1,),
        in_specs=[
            pl.BlockSpec((tm, d), lambda i: (jnp.minimum(i, last), 0)),
            pl.BlockSpec((1, d), const),
            pl.BlockSpec((pl.Element(dm), pl.Element(d)), const, pipeline_mode=pl.Buffered(1)),
            pl.BlockSpec((pl.Element(3 * d_conv), pl.Element(d)), lambda i: (dm + n_gate, 0),
                         pipeline_mode=pl.Buffered(1)),
            pl.BlockSpec((pl.Element(n_gate), pl.Element(d)), lambda i: (dm, 0)),
            pl.BlockSpec((2 * N_HEADS, 1), const),
            pl.BlockSpec((CONV_WIDTH, d_conv), const),
            vec, vec, vec,
        ],
        out_specs=[
            pl.BlockSpec((tm, dm), lambda i: (jnp.minimum(i, last), 0)),
            pl.BlockSpec((2 * N_HEADS, tm), lambda i: (0, jnp.minimum(i, last))),
            pl.BlockSpec((tm, d_conv), lambda i: (jnp.maximum(i - 1, 0), 0)),
        ],
        out_shape=[
            jax.ShapeDtypeStruct((t, dm), bf16),
            jax.ShapeDtypeStruct((2 * N_HEADS, t), f32),
            jax.ShapeDtypeStruct((t, d_conv), bf16),
        ],
        scratch_shapes=[
            pltpu.VMEM((tm, d), bf16),
            pltpu.VMEM((tm, 2 * d_conv), bf16),
            pltpu.VMEM((tm, d_conv), bf16),
            pltpu.VMEM((tm + CONV_HALO, d_conv), f32),
            pltpu.VMEM((tm, d_conv), f32),
        ],
        compiler_params=pltpu.CompilerParams(
            dimension_semantics=("arbitrary",), vmem_limit_bytes=VMEM_LIMIT,
            allow_input_fusion=[False, False, True, True, True, False, False, False, False, False]),
        name="proj_conv",
    )(x2, g, w_t, w_t, w_t, bg, conv_w, conv_b, ln_g, ln_b)


def _cumsum_lanes(x):
    rows, n = x.shape
    tri = (lax.broadcasted_iota(jnp.int32, (n, n), 0)
           <= lax.broadcasted_iota(jnp.int32, (n, n), 1)).astype(bf16)
    hi = x.astype(bf16).astype(f32)
    mid = (x - hi).astype(bf16).astype(f32)
    lo = x - hi - mid
    parts = jnp.dot(jnp.concatenate([hi, mid, lo], axis=0).astype(bf16), tri, preferred_element_type=f32)
    return parts[0:rows] + parts[rows:2 * rows] + parts[2 * rows:3 * rows]


def _gate_chain(gr, m_prev):
    L = gr.shape[1]
    lf = jnp.minimum(gr, 0.0) - jnp.log1p(jnp.exp(-jnp.abs(gr)))
    bfull = _cumsum_lanes(lf)
    ig = gr[0:N_HEADS]
    b = bfull[N_HEADS:2 * N_HEADS]
    b_last = b[:, L - 1:L]
    a = b_last - b + ig
    a_max = jnp.max(a, axis=1, keepdims=True)
    m_new = jnp.maximum(b_last + m_prev, a_max)
    s_old = jnp.exp(b_last + m_prev - m_new)
    w_a = jnp.exp(a - m_new)
    c_row = ig - b
    run = jnp.concatenate([c_row, c_row], axis=0)
    lane = lax.broadcasted_iota(jnp.int32, run.shape, 1)
    shift = 1
    while shift < L:
        run = jnp.maximum(run, jnp.where(lane >= shift, pltpu.roll(run, shift, 1), -jnp.inf))
        shift *= 2
    run = run[0:N_HEADS]
    li = b + m_prev
    m_t = jnp.maximum(li, b + run)
    s_inter = jnp.exp(li - m_t)
    e_neg = jnp.exp(-m_t)
    cols = jnp.concatenate([jnp.transpose(jnp.concatenate([b, w_a], axis=0)),
                            jnp.transpose(jnp.concatenate([m_t, s_inter], axis=0)),
                            jnp.transpose(jnp.concatenate([e_neg, e_neg], axis=0))], axis=1)
    return c_row, w_a, cols, m_new, s_old


def _scores(q_ref, k_ref, head_dim):
    return [lax.dot_general(q_ref[:, h * head_dim:(h + 1) * head_dim],
                            k_ref[:, h * head_dim:(h + 1) * head_dim], _NT,
                            preferred_element_type=f32) * head_dim ** -0.5 for h in range(N_HEADS)]


def _store_front(chain, scores, grow_ref, gcol_ref, gsc_ref, s_ref):
    c_row, w_a, cols, m_new, s_old = chain
    grow_ref[0:N_HEADS, :] = c_row
    grow_ref[N_HEADS:2 * N_HEADS, :] = w_a
    gcol_ref[:, 0:cols.shape[1]] = cols
    for j, v in enumerate((m_new, s_old)):
        gsc_ref[j, 0:N_HEADS, :] = jnp.broadcast_to(v, (N_HEADS, gsc_ref.shape[2]))
    for h in range(N_HEADS):
        s_ref[h] = scores[h]


def _mlstm_out_kernel(q_ref, k_ref, v_ref, o_ref, z_ref, gates_ref, qn_ref, kn_ref, gates_next_ref, mhg_ref,
                      x_ref, cc_ref, wa_ref, wc_ref, fg_ref, out_ref,
                      c_ref, n_ref, hm_ref, grow_ref, gcol_ref, gsc_ref, s_ref,
                      *, head_dim, chunks_per_seq, final_norm):
    L = q_ref.shape[0]
    scale = head_dim ** -0.5
    step = pl.program_id(0)
    wr = step % 2
    rd = 1 - wr

    m_init = jnp.full((N_HEADS, 1), M_INIT, f32)

    @pl.when(step == 0)
    def _():
        hm_ref[...] = jnp.zeros_like(hm_ref)
        _store_front(_gate_chain(gates_ref[...], m_init), _scores(q_ref, k_ref, head_dim),
                     grow_ref, gcol_ref, gsc_ref, s_ref)

    @pl.when(step % chunks_per_seq == 0)
    def _():
        c_ref[...] = jnp.zeros_like(c_ref)
        n_ref[...] = jnp.zeros_like(n_ref)

    c_row = grow_ref[0:N_HEADS, :]
    w_a = grow_ref[N_HEADS:2 * N_HEADS, :]
    cols = gcol_ref[:, 0:6 * N_HEADS]
    m_new = gsc_ref[0, 0:N_HEADS, 0:1]
    s_old = gsc_ref[1, 0:N_HEADS, 0:1]
    next_chain = _gate_chain(gates_next_ref[...],
                             jnp.where((step + 1) % chunks_per_seq == 0, m_init, m_new))

    row_id = lax.broadcasted_iota(jnp.int32, (L, L), 0)
    col_id = lax.broadcasted_iota(jnp.int32, (L, L), 1)
    causal = col_id <= row_id

    d_out = out_ref.shape[1]
    tn = d_out // N_HEADS

    heads = range(N_HEADS)
    hs = [slice(h * head_dim, (h + 1) * head_dim) for h in heads]
    ys = []

    def out_proj_part(j):
        ol = slice(j * tn, (j + 1) * tn)
        ys.append(jnp.dot(hm_ref[rd], wa_ref[:, ol], preferred_element_type=f32)
                  + jnp.dot(cc_ref[...], wc_ref[:, ol], preferred_element_type=f32))

    o_gate = [_sigmoid(o_ref[:, hs[h]].astype(f32)) for h in heads]
    z_gate = [_silu(z_ref[:, hs[h]].astype(f32)) for h in heads]
    qn = [jnp.sum(q_ref[:, hs[h]].astype(f32) * n_ref[h:h + 1, :], axis=1, keepdims=True) for h in heads]

    s_ = [s_ref[h] for h in heads]
    c_prev = [c_ref[h] for h in heads]
    qc = [jnp.dot(q_ref[:, hs[h]], c_prev[h].astype(bf16), preferred_element_type=f32) for h in heads]
    next_scores = _scores(qn_ref, kn_ref, head_dim)
    out_proj_part(0)

    p_, e_neg_, s_inter_ = [], [], []
    for h in heads:
        b_col = cols[:, h:h + 1]
        m_t = cols[:, 2 * N_HEADS + h:2 * N_HEADS + h + 1]
        g = jnp.where(causal, b_col + c_row[h:h + 1, :], -jnp.inf)
        p_.append(jnp.exp(g - m_t) * s_[h])
        s_inter_.append(cols[:, 3 * N_HEADS + h:3 * N_HEADS + h + 1])
        e_neg_.append(cols[:, 4 * N_HEADS + h:4 * N_HEADS + h + 1])

    pv = [jnp.dot(p_[h].astype(bf16), v_ref[:, hs[h]], preferred_element_type=f32) for h in heads]
    out_proj_part(1)

    hm_new, vw = [], []
    for h in heads:
        num = pv[h] + s_inter_[h] * qc[h]
        den = jnp.sum(p_[h], axis=1, keepdims=True) + s_inter_[h] * qn[h]
        hh = num * (1.0 / jnp.maximum(jnp.abs(den), e_neg_[h]))
        hh = o_gate[h] * hh
        hh = hh * lax.rsqrt(jnp.mean(hh * hh, axis=-1, keepdims=True) + EPS)
        hh = hh * mhg_ref[:, hs[h]]
        hm_new.append((hh * z_gate[h]).astype(bf16))
        w_col = cols[:, N_HEADS + h:N_HEADS + h + 1]
        vw.append((v_ref[:, hs[h]].astype(f32) * w_col).astype(bf16))

    out_proj_part(2)
    kv = [lax.dot_general(k_ref[:, hs[h]], vw[h], _TN, preferred_element_type=f32) * scale
          for h in heads]
    out_proj_part(3)
    n_all = jnp.dot(jnp.concatenate([w_a, w_a], axis=0).astype(bf16), k_ref[...],
                    preferred_element_type=f32) * scale
    c_new = [s_old[h:h + 1, :] * c_prev[h] + kv[h] for h in heads]
    n_new = [s_old[h:h + 1, :] * n_ref[h:h + 1, :] + n_all[h:h + 1, hs[h]] for h in heads]

    y = x_ref[...] + jnp.concatenate(ys, axis=1)
    if final_norm:
        y = y * lax.rsqrt(jnp.mean(y * y, axis=-1, keepdims=True) + EPS) * fg_ref[...]
    out_ref[...] = y
    for h in heads:
        hm_ref[wr, :, hs[h]] = hm_new[h]
        c_ref[h] = c_new[h]
        n_ref[h:h + 1, :] = n_new[h]
    _store_front(next_chain, next_scores, grow_ref, gcol_ref, gsc_ref, s_ref)


def _mlstm_out(proj, gates, mhg, x2, cc, w_a, w_c, fg, *, seq, d_mlstm, final_norm):
    t, d = x2.shape
    L = MLSTM_CHUNK
    nc = seq // L
    n = t // L
    last = n - 1
    hd = d_mlstm // N_HEADS
    assert d % N_HEADS == 0 and seq % L == 0

    def col(j, ahead=0):
        return pl.BlockSpec((L, d_mlstm), lambda s, j=j: (jnp.minimum(s + ahead, last), j))

    def gate_rows(ahead):
        return pl.BlockSpec((2 * N_HEADS, L), lambda s: (0, jnp.minimum(s + ahead, last)))

    prev = lambda s: (jnp.maximum(s - 1, 0), 0)
    const = lambda s: (0, 0)
    return pl.pallas_call(
        functools.partial(_mlstm_out_kernel, head_dim=hd, chunks_per_seq=nc, final_norm=final_norm),
        grid=(n + 1,),
        in_specs=[col(0), col(1), col(2), col(3), col(4), gate_rows(0),
                  col(0, 1), col(1, 1), gate_rows(1),
                  pl.BlockSpec((1, d_mlstm), const),
                  pl.BlockSpec((L, d), prev),
                  pl.BlockSpec((L, cc.shape[1]), prev),
                  pl.BlockSpec(w_a.shape, const, pipeline_mode=pl.Buffered(1)),
                  pl.BlockSpec(w_c.shape, const, pipeline_mode=pl.Buffered(1)),
                  pl.BlockSpec((1, d), const)],
        out_specs=pl.BlockSpec((L, d), prev),
        out_shape=jax.ShapeDtypeStruct((t, d), f32),
        scratch_shapes=[pltpu.VMEM((N_HEADS, hd, hd), f32),
                        pltpu.VMEM((SUBLANES, hd), f32),
                        pltpu.VMEM((2, L, d_mlstm), bf16),
                        pltpu.VMEM((2 * N_HEADS, L), f32),
                        pltpu.VMEM((L, LANES), f32),
                        pltpu.VMEM((2, SUBLANES, LANES), f32),
                        pltpu.VMEM((N_HEADS, L, L), f32)],
        compiler_params=pltpu.CompilerParams(
            dimension_semantics=("arbitrary",), vmem_limit_bytes=VMEM_LIMIT),
        name="mlstm_out",
    )(proj, proj, proj, proj, proj, gates, proj, proj, gates, mhg, x2, cc, w_a, w_c, fg)


def kernel(x, norm_g, w_in, b_gates, mh_norm_g, conv_w, conv_b, conv_ln_g, conv_ln_b, w_out, final_norm_g):
    batch, seq, d_model = x.shape
    depth = norm_g.shape[0]
    d_mlstm = mh_norm_g.shape[1]
    d_conv = conv_b.shape[1]
    n_gate = 2 * N_HEADS
    g0 = 5 * d_mlstm

    h = x.reshape(batch * seq, d_model)
    for l in range(depth):
        w_t = jnp.swapaxes(w_in[l], 0, 1).astype(bf16)
        proj, gates, c = _proj_conv(h, norm_g[l][None, :], w_t, b_gates[l][:, None],
                                    conv_w[l], conv_b[l][None, :], conv_ln_g[l][None, :],
                                    conv_ln_b[l][None, :], seq=seq, d_conv=d_conv, dm=g0)
        wo = w_out[l].astype(bf16)
        h = _mlstm_out(proj, gates, mh_norm_g[l][None, :], h, c, wo[:d_mlstm], wo[d_mlstm:],
                       final_norm_g[None, :], seq=seq, d_mlstm=d_mlstm, final_norm=(l == depth - 1))
    return h.reshape(batch, seq, d_model)
```
